```python
import jax, jax.numpy as jnp
from jax import lax
import numpy as np

D_MODEL = 1024
BATCH = 4
SEQ = 8192
DEPTH = 1
DEC_BATCH = 32
DEC_SEQ = 16
PAST_LEN = 4096

CHUNK = 64
H_M = 4
DH_M = 256
W_M = H_M * DH_M
H_F = 16
DH_F = 64
W_F = H_F * DH_F
Q_BLOCK = 128
D_PLE = 256
PEER_HEADS = 8
PEER_KEYS = 128
PEER_N = PEER_KEYS * PEER_KEYS
PEER_DQ = 256
PEER_TOPK = 16
PEER_BLOCK = 128
EPS = 1e-6
_IN_SIZES = (W_M, W_M, W_M, W_M, H_M, H_M, W_F, W_F, W_F, H_F, D_MODEL, D_MODEL)
IN_COLS = sum(_IN_SIZES)

kernel_name = 'hybrid_mlstm_fox_peer_stream_step'


def rmsnorm(x, g):
    xf = x.astype(jnp.float32)
    y = xf * lax.rsqrt(jnp.mean(xf * xf, axis=-1, keepdims=True) + EPS)
    return (y * g.astype(jnp.float32)).astype(x.dtype)


def split_cols(z):
    offs = []
    acc = 0
    for s in _IN_SIZES[:-1]:
        acc += s
        offs.append(acc)
    return jnp.split(z, offs, axis=-1)


def mlstm_chunkwise(q, k, v, i_pre, log_f, C0, n0, m0):
    f32 = jnp.float32
    B, T = q.shape[0], q.shape[1]
    L = min(CHUNK, T)
    nc = T // L

    def to_chunks(z):
        return jnp.moveaxis(z.reshape((B, nc, L) + z.shape[2:]), 1, 0)

    xs = (to_chunks(q.astype(f32)), to_chunks(k.astype(f32) * DH_M ** -0.5), to_chunks(v.astype(f32)),
          to_chunks(i_pre.astype(f32)), to_chunks(log_f.astype(f32)))
    tril = jnp.tril(jnp.ones((L, L), dtype=bool))

    def step(carry, inp):
        C, n, m = carry
        qc, kc, vc, ic, fc = inp
        ic = jnp.swapaxes(ic, 1, 2)
        b = jnp.cumsum(jnp.swapaxes(fc, 1, 2), axis=-1)
        D = jnp.where(tril, b[..., :, None] - b[..., None, :] + ic[..., None, :], -jnp.inf)
        g = b + m[..., None]
        mt = jnp.maximum(g, jnp.max(D, axis=-1))
        wD = jnp.exp(D - mt[..., None])
        wg = jnp.exp(g - mt)
        qk = jnp.einsum('blhd,bshd->bhls', qc, kc) * wD
        num = wg[..., None] * jnp.einsum('blhd,bhde->bhle', qc, C) + jnp.einsum('bhls,bshe->bhle', qk, vc)
        dot = wg * jnp.einsum('blhd,bhd->bhl', qc, n) + jnp.sum(qk, axis=-1)
        h = num / jnp.maximum(jnp.abs(dot), jnp.exp(-mt))[..., None]
        m_new = mt[..., -1]
        wC = jnp.exp(b[..., -1] + m - m_new)
        ws = jnp.exp(b[..., -1:] - b + ic - m_new[..., None])
        C_new = wC[..., None, None] * C + jnp.einsum('bhs,bshd,bshe->bhde', ws, kc, vc)
        n_new = wC[..., None] * n + jnp.einsum('bhs,bshd->bhd', ws, kc)
        return (C_new, n_new, m_new), jnp.swapaxes(h, 1, 2)

    (C, n, m), hs = lax.scan(step, (C0.astype(f32), n0.astype(f32), m0.astype(f32)), xs)
    h = jnp.moveaxis(hs, 0, 1).reshape(B, T, H_M, DH_M)
    return h.astype(q.dtype), C.astype(C0.dtype), n.astype(n0.dtype), m.astype(m0.dtype)


def fox_block(qb, cq, q_pos, k, v, ck, k_pos):
    s = jnp.einsum('bqhd,bshd->bhqs', qb.astype(jnp.float32), k) * DH_F ** -0.5
    s = s + jnp.swapaxes(cq, 1, 2)[..., :, None] - jnp.swapaxes(ck, 1, 2)[..., None, :]
    s = jnp.where(k_pos[None, :] <= q_pos[:, None], s, -jnp.inf)
    p = jax.nn.softmax(s, axis=-1)
    return jnp.einsum('bhqs,bshd->bqhd', p, v).astype(qb.dtype)


def peer(x, w_q, keys1, keys2, u_tab, v_tab):
    shp = x.shape
    xt = x.reshape(-1, D_MODEL)
    n = xt.shape[0]
    pad = (-n) % PEER_BLOCK
    blocks = jnp.pad(xt, ((0, pad), (0, 0))).reshape(-1, PEER_BLOCK, D_MODEL)
    half = PEER_DQ // 2
    k1 = keys1.astype(jnp.float32)
    k2 = keys2.astype(jnp.float32)

    def one(xb):
        q = (xb @ w_q).reshape(PEER_BLOCK, PEER_HEADS, PEER_DQ).astype(jnp.float32)
        s1 = jnp.einsum('thd,hkd->thk', q[..., :half], k1)
        s2 = jnp.einsum('thd,hkd->thk', q[..., half:], k2)
        v1, i1 = lax.top_k(s1, PEER_TOPK)
        v2, i2 = lax.top_k(s2, PEER_TOPK)
        cand = (v1[..., :, None] + v2[..., None, :]).reshape(PEER_BLOCK, PEER_HEADS, PEER_TOPK * PEER_TOPK)
        sc, j = lax.top_k(cand, PEER_TOPK)
        idx = (jnp.take_along_axis(i1, j // PEER_TOPK, axis=-1) * PEER_KEYS
               + jnp.take_along_axis(i2, j % PEER_TOPK, axis=-1))
        g = jax.nn.softmax(sc, axis=-1)
        act = jax.nn.gelu(jnp.einsum('thkd,td->thk', u_tab[idx], xb).astype(jnp.float32), approximate=False)
        return jnp.einsum('thk,thkd->td', (g * act).astype(xb.dtype), v_tab[idx])

    out = lax.map(one, blocks).reshape(-1, D_MODEL)[:n]
    return out.reshape(shp).astype(x.dtype)


def trunk_layer(h, p, lw, mstate, fox_cache):
    f32 = jnp.float32
    B, T = h.shape[0], h.shape[1]
    a = rmsnorm(h, lw['norm_mix'])
    z = a @ lw['w_in']
    qm, km, vm, om, im, fm, qf, kf, vf, ff, gm, gf = split_cols(z)
    i_pre = im.astype(f32) + lw['mlstm_b_i'].astype(f32)
    log_fm = jax.nn.log_sigmoid(fm.astype(f32) + lw['mlstm_b_f'].astype(f32))
    y_m, C, n, m = mlstm_chunkwise(qm.reshape(B, T, H_M, DH_M), km.reshape(B, T, H_M, DH_M),
                                   vm.reshape(B, T, H_M, DH_M), i_pre, log_fm, mstate[0], mstate[1], mstate[2])
    y_m = jax.nn.sigmoid(om) * y_m.reshape(B, T, W_M)
    log_ff = jax.nn.log_sigmoid(ff.astype(f32) + lw['fox_b_f'].astype(f32))
    q = qf.reshape(B, T, H_F, DH_F)
    k = kf.reshape(B, T, H_F, DH_F)
    v = vf.reshape(B, T, H_F, DH_F)
    if fox_cache is None:
        c = jnp.cumsum(log_ff, axis=1)
        k32, v32 = k.astype(f32), v.astype(f32)
        kpos = jnp.arange(T)
        nb = T // Q_BLOCK
        qs = jnp.moveaxis(q.reshape(B, nb, Q_BLOCK, H_F, DH_F), 1, 0)
        cs = jnp.moveaxis(c.reshape(B, nb, Q_BLOCK, H_F), 1, 0)
        ps = kpos.reshape(nb, Q_BLOCK)
        yb = lax.map(lambda t: fox_block(t[0], t[1], t[2], k32, v32, c, kpos), (qs, cs, ps))
        y_f = jnp.moveaxis(yb, 0, 1).reshape(B, T, W_F)
    else:
        ck, cv, clf = fox_cache
        P = ck.shape[1]
        k_all = jnp.concatenate([ck.astype(f32), k.astype(f32)], axis=1)
        v_all = jnp.concatenate([cv.astype(f32), v.astype(f32)], axis=1)
        c_all = jnp.cumsum(jnp.concatenate([clf.astype(f32), log_ff], axis=1), axis=1)
        y_f = fox_block(q, c_all[:, P:], P + jnp.arange(T), k_all, v_all, c_all, jnp.arange(P + T))
        y_f = y_f.reshape(B, T, W_F)
    merge = jax.nn.sigmoid(gm) * (y_m @ lw['w_br_m']) + jax.nn.sigmoid(gf) * (y_f @ lw['w_br_f'])
    h = h + merge @ lw['w_out']
    h = h + peer(rmsnorm(h, lw['norm_ffn']), lw['peer_w_q'], lw['peer_keys1'], lw['peer_keys2'],
                 lw['peer_u'], lw['peer_v'])
    e = rmsnorm(h, lw['norm_ple'])
    h = h + jax.nn.sigmoid(e @ lw['w_ple_gate']) * (p @ lw['w_ple_proj'])
    return h, (k, v, log_ff.astype(h.dtype), C, n, m)


def setup_inputs(seed: int = 0) -> dict:
    key = jax.random.key(seed)
    ks = jax.random.split(key, 32)
    nrm = lambda i, shape, s: jax.random.normal(ks[i], shape, jnp.float32) * s
    D = D_MODEL
    return {
        'x_prompt': nrm(0, (BATCH, SEQ, D), 1.0),
        'x_sample': nrm(1, (DEC_BATCH, DEC_SEQ, D), 1.0),
        'p_prompt': nrm(2, (DEPTH, BATCH, SEQ, D_PLE), 1.0),
        'p_sample': nrm(3, (DEPTH, DEC_BATCH, DEC_SEQ, D_PLE), 1.0),
        'cache_fox_k': nrm(4, (DEPTH, DEC_BATCH, PAST_LEN, H_F, DH_F), 1.0),
        'cache_fox_v': nrm(5, (DEPTH, DEC_BATCH, PAST_LEN, H_F, DH_F), 1.0),
        'cache_fox_logf': jax.nn.log_sigmoid(3.0 + nrm(6, (DEPTH, DEC_BATCH, PAST_LEN, H_F), 0.5)),
        'state_mlstm_C': nrm(7, (DEPTH, DEC_BATCH, H_M, DH_M, DH_M), 0.1),
        'state_mlstm_n': nrm(8, (DEPTH, DEC_BATCH, H_M, DH_M), 0.1),
        'state_mlstm_m': nrm(9, (DEPTH, DEC_BATCH, H_M), 1.0),
        'norm_mix': 1.0 + nrm(10, (DEPTH, D), 0.02),
        'w_in': nrm(11, (DEPTH, D, IN_COLS), D ** -0.5),
        'mlstm_b_i': nrm(12, (DEPTH, H_M), 0.1),
        'mlstm_b_f': 3.0 + nrm(13, (DEPTH, H_M), 0.5),
        'fox_b_f': 3.0 + nrm(14, (DEPTH, H_F), 0.5),
        'w_br_m': nrm(15, (DEPTH, W_M, D), W_M ** -0.5),
        'w_br_f': nrm(16, (DEPTH, W_F, D), W_F ** -0.5),
        'w_out': nrm(17, (DEPTH, D, D), D ** -0.5),
        'norm_ffn': 1.0 + nrm(18, (DEPTH, D), 0.02),
        'peer_w_q': nrm(19, (DEPTH, D, PEER_HEADS * PEER_DQ), D ** -0.5),
        'peer_keys1': nrm(20, (DEPTH, PEER_HEADS, PEER_KEYS, PEER_DQ // 2), (PEER_DQ // 2) ** -0.5),
        'peer_keys2': nrm(21, (DEPTH, PEER_HEADS, PEER_KEYS, PEER_DQ // 2), (PEER_DQ // 2) ** -0.5),
        'peer_u': nrm(22, (DEPTH, PEER_N, D), D ** -0.5),
        'peer_v': nrm(23, (DEPTH, PEER_N, D), PEER_HEADS ** -0.5),
        'norm_ple': 1.0 + nrm(24, (DEPTH, D), 0.02),
        'w_ple_gate': nrm(25, (DEPTH, D, D), D ** -0.5),
        'w_ple_proj': nrm(26, (DEPTH, D_PLE, D), D_PLE ** -0.5),
        'norm_final': 1.0 + nrm(27, (D,), 0.02),
    }


def reference(x_prompt, x_sample, p_prompt, p_sample, cache_fox_k, cache_fox_v, cache_fox_logf,
              state_mlstm_C, state_mlstm_n, state_mlstm_m, norm_mix, w_in, mlstm_b_i, mlstm_b_f, fox_b_f,
              w_br_m, w_br_f, w_out, norm_ffn, peer_w_q, peer_keys1, peer_keys2, peer_u, peer_v,
              norm_ple, w_ple_gate, w_ple_proj, norm_final):
    hp, hs = x_prompt, x_sample
    Bp = x_prompt.shape[0]
    new_p, new_s = [], []
    for l in range(DEPTH):
        lw = {'norm_mix': norm_mix[l], 'w_in': w_in[l], 'mlstm_b_i': mlstm_b_i[l], 'mlstm_b_f': mlstm_b_f[l],
              'fox_b_f': fox_b_f[l], 'w_br_m': w_br_m[l], 'w_br_f': w_br_f[l], 'w_out': w_out[l],
              'norm_ffn': norm_ffn[l], 'peer_w_q': peer_w_q[l], 'peer_keys1': peer_keys1[l],
              'peer_keys2': peer_keys2[l], 'peer_u': peer_u[l], 'peer_v': peer_v[l], 'norm_ple': norm_ple[l],
              'w_ple_gate': w_ple_gate[l], 'w_ple_proj': w_ple_proj[l]}
        m0 = (jnp.zeros((Bp, H_M, DH_M, DH_M), jnp.float32), jnp.zeros((Bp, H_M, DH_M), jnp.float32),
              jnp.zeros((Bp, H_M), jnp.float32))
        hp, sp = trunk_layer(hp, p_prompt[l], lw, m0, None)
        hs, ss = trunk_layer(hs, p_sample[l], lw, (state_mlstm_C[l], state_mlstm_n[l], state_mlstm_m[l]),
                             (cache_fox_k[l], cache_fox_v[l], cache_fox_logf[l]))
        new_p.append(sp)
        new_s.append(ss)
    y_prompt = rmsnorm(hp, norm_final)
    y_sample = rmsnorm(hs, norm_final)
    fox_k_p = jnp.stack([s[0] for s in new_p])
    fox_v_p = jnp.stack([s[1] for s in new_p])
    fox_logf_p = jnp.stack([s[2] for s in new_p])
    mlstm_C_p = jnp.stack([s[3] for s in new_p])
    mlstm_n_p = jnp.stack([s[4] for s in new_p])
    mlstm_m_p = jnp.stack([s[5] for s in new_p])
    fox_k_s = jnp.stack([s[0] for s in new_s])
    fox_v_s = jnp.stack([s[1] for s in new_s])
    fox_logf_s = jnp.stack([s[2] for s in new_s])
    mlstm_C_s = jnp.stack([s[3] for s in new_s])
    mlstm_n_s = jnp.stack([s[4] for s in new_s])
    mlstm_m_s = jnp.stack([s[5] for s in new_s])
    return (y_prompt, y_sample, fox_k_p, fox_v_p, fox_logf_p, mlstm_C_p, mlstm_n_p, mlstm_m_p,
            fox_k_s, fox_v_s, fox_logf_s, mlstm_C_s, mlstm_n_s, mlstm_m_s)
```

```python
import functools

import jax
import jax.numpy as jnp
from jax import lax
from jax.experimental import pallas as pl
from jax.experimental.pallas import tpu as pltpu
from jax.experimental.pallas import tpu_sc as plsc

D_MODEL = 1024
CHUNK = 64
H_M = 4
DH_M = 256
W_M = H_M * DH_M
H_F = 16
DH_F = 64
W_F = H_F * DH_F
D_PLE = 256
PEER_HEADS = 8
PEER_KEYS = 128
PEER_DQ = 256
PEER_TOPK = 16
PEER_SEL = PEER_HEADS * PEER_TOPK
EPS = 1e-6

LANES = 128
GATE_COLS = LANES
COL_I, COL_F, COL_FF = 0, H_M, 2 * H_M
VMEM_LIMIT = 56 * 1024 * 1024
HIGHEST = lax.Precision.HIGHEST
F32 = jnp.float32
BF16 = jnp.bfloat16
NEG_INF = float("-inf")


def _cparams(*sem):
    return pltpu.CompilerParams(dimension_semantics=sem, vmem_limit_bytes=VMEM_LIMIT)


def _rms(x, g):
    return x * lax.rsqrt(jnp.mean(x * x, axis=-1, keepdims=True) + EPS) * g


def _sigmoid(x):
    return 1.0 / (1.0 + jnp.exp(-x))


def _norm_matmul_kernel(x_ref, g_ref, w_ref, o_ref, a_ref):
    @pl.when(pl.program_id(1) == 0)
    def _():
        a_ref[...] = _rms(x_ref[...], g_ref[...]).astype(a_ref.dtype)

    o_ref[...] = jnp.dot(a_ref[...], w_ref[...], preferred_element_type=F32).astype(o_ref.dtype)


def _norm_matmul(x, g, w, out_dtype, name):
    n, d = x.shape
    cols = w.shape[1]
    tm = min(1024, n)
    tn = 1024
    return pl.pallas_call(
        _norm_matmul_kernel,
        grid=(n // tm, cols // tn),
        in_specs=[pl.BlockSpec((tm, d), lambda i, j: (i, 0)),
                  pl.BlockSpec((1, d), lambda i, j: (0, 0)),
                  pl.BlockSpec((d, tn), lambda i, j: (0, j))],
        out_specs=pl.BlockSpec((tm, tn), lambda i, j: (i, j)),
        out_shape=jax.ShapeDtypeStruct((n, cols), out_dtype),
        scratch_shapes=[pltpu.VMEM((tm, d), BF16)],
        compiler_params=_cparams("parallel", "arbitrary"),
        name=name,
    )(x, g, w)


def _gate_kernel(x_ref, g_ref, w_ref, b_ref, o_ref):
    a = _rms(x_ref[...], g_ref[...])
    z = jnp.dot(a, w_ref[...], precision=HIGHEST, preferred_element_type=F32) + b_ref[...]
    col = lax.broadcasted_iota(jnp.int32, z.shape, 1)
    log_sig = jnp.minimum(z, 0.0) - jnp.log1p(jnp.exp(-jnp.abs(z)))
    o_ref[...] = jnp.where(col < COL_F, z, log_sig)


def _gates(x, g, w, b):
    n, d = x.shape
    tm = min(512, n)
    return pl.pallas_call(
        _gate_kernel,
        grid=(n // tm,),
        in_specs=[pl.BlockSpec((tm, d), lambda i: (i, 0)),
                  pl.BlockSpec((1, d), lambda i: (0, 0)),
                  pl.BlockSpec((d, GATE_COLS), lambda i: (0, 0)),
                  pl.BlockSpec((1, GATE_COLS), lambda i: (0, 0))],
        out_specs=pl.BlockSpec((tm, GATE_COLS), lambda i: (i, 0)),
        out_shape=jax.ShapeDtypeStruct((n, GATE_COLS), F32),
        compiler_params=_cparams("parallel"),
        name="gates",
    )(x, g, w, b)


def _cumsum_kernel(x_ref, o_ref, carry_ref):
    @pl.when(pl.program_id(1) == 0)
    def _():
        carry_ref[...] = jnp.zeros_like(carry_ref)

    x = x_ref[...]
    tb = x.shape[0]
    row = lax.broadcasted_iota(jnp.int32, (tb, tb), 0)
    col = lax.broadcasted_iota(jnp.int32, (tb, tb), 1)
    tril = jnp.where(col <= row, 1.0, 0.0).astype(F32)
    c = jnp.dot(tril, x, precision=HIGHEST, preferred_element_type=F32) + carry_ref[...]
    o_ref[...] = c
    carry_ref[...] = c[tb - 1:tb, :]


def _cumsum_tokens(x):
    b, t, w = x.shape
    tb = 256
    return pl.pallas_call(
        _cumsum_kernel,
        grid=(b, t // tb),
        in_specs=[pl.BlockSpec((None, tb, w), lambda i, j: (i, j, 0))],
        out_specs=pl.BlockSpec((None, tb, w), lambda i, j: (i, j, 0)),
        out_shape=jax.ShapeDtypeStruct((b, t, w), F32),
        scratch_shapes=[pltpu.VMEM((1, w), F32)],
        compiler_params=_cparams("parallel", "arbitrary"),
        name="cumsum",
    )(x)


def _mlstm_kernel(qkv_ref, og_ref, g_ref, c0_ref, n0_ref, m0_ref,
                  y_ref, cn_ref, nn_ref, mn_ref, c_s, n_s, m_s, *, bb_n, blk):
    step = pl.program_id(1)

    @pl.when(step == 0)
    def _():
        c_s[...] = c0_ref[...]
        n_s[...] = n0_ref[...]
        m_s[...] = m0_ref[...]

    row = lax.broadcasted_iota(jnp.int32, (blk, blk), 0)
    col = lax.broadcasted_iota(jnp.int32, (blk, blk), 1)
    tril = col <= row
    triu = row <= col
    eye = col == row

    def to_row(x_col):
        return jnp.sum(jnp.where(eye, x_col, 0.0), axis=0, keepdims=True)

    for bb in range(bb_n):
        g = g_ref[bb]
        for h in range(H_M):
            q = qkv_ref[bb, :, h * DH_M:(h + 1) * DH_M]
            k = qkv_ref[bb, :, (H_M + h) * DH_M:(H_M + h + 1) * DH_M] * (DH_M ** -0.5)
            v = qkv_ref[bb, :, (2 * H_M + h) * DH_M:(2 * H_M + h + 1) * DH_M]
            i_col = g[:, COL_I + h:COL_I + h + 1]
            f_col = g[:, COL_F + h:COL_F + h + 1]
            i_row = to_row(i_col)
            f_row = to_row(f_col)
            b_col = jnp.sum(jnp.where(tril, f_row, 0.0), axis=1, keepdims=True)
            b_row = jnp.sum(jnp.where(triu, f_col, 0.0), axis=0, keepdims=True)
            m_prev = m_s[bb, h][:, 0:1]
            dmat = jnp.where(tril, b_col - b_row + i_row, NEG_INF)
            g_col = b_col + m_prev
            mt = jnp.maximum(g_col, jnp.max(dmat, axis=1, keepdims=True))
            w_d = jnp.exp(dmat - mt)
            w_g = jnp.exp(g_col - mt)
            qk = lax.dot_general(q, k, (((1,), (1,)), ((), ())), preferred_element_type=F32) * w_d
            c_prev = c_s[bb, h]
            n_prev = n_s[bb, h]
            num = (w_g * jnp.dot(q, c_prev.astype(BF16), preferred_element_type=F32)
                   + jnp.dot(qk.astype(BF16), v, preferred_element_type=F32))
            den = (w_g * jnp.sum(q.astype(F32) * n_prev, axis=1, keepdims=True)
                   + jnp.sum(qk, axis=1, keepdims=True))
            hid = num / jnp.maximum(jnp.abs(den), jnp.exp(-mt))
            o_gate = _sigmoid(og_ref[bb, :, h * DH_M:(h + 1) * DH_M])
            y_ref[bb, :, h * DH_M:(h + 1) * DH_M] = (o_gate * hid).astype(y_ref.dtype)
            m_new = mt[blk - 1:blk, :]
            b_last = b_col[blk - 1:blk, :]
            w_c = jnp.exp(b_last + m_prev - m_new)
            w_s = jnp.exp(b_last - b_col + i_col - m_new)
            kw = k.astype(F32) * w_s
            c_s[bb, h] = w_c * c_prev + lax.dot_general(
                kw.astype(BF16), v, (((0,), (0,)), ((), ())), preferred_element_type=F32)
            n_s[bb, h] = w_c * n_prev + jnp.sum(kw, axis=0, keepdims=True)
            m_s[bb, h] = jnp.broadcast_to(m_new, (1, LANES))

    @pl.when(step == pl.num_programs(1) - 1)
    def _():
        cn_ref[...] = c_s[...]
        nn_ref[...] = n_s[...]
        mn_ref[...] = m_s[...]


def _mlstm(qkv, ogate, gates, c0, n0, m0):
    b, t, _ = qkv.shape
    blk = min(CHUNK, t)
    bb_n = 4
    state_spec = lambda shape: pl.BlockSpec((bb_n,) + shape, lambda i, j: (i, 0, 0, 0))
    return pl.pallas_call(
        functools.partial(_mlstm_kernel, bb_n=bb_n, blk=blk),
        grid=(b // bb_n, t // blk),
        in_specs=[pl.BlockSpec((bb_n, blk, 3 * W_M), lambda i, j: (i, j, 0)),
                  pl.BlockSpec((bb_n, blk, W_M), lambda i, j: (i, j, 0)),
                  pl.BlockSpec((bb_n, blk, GATE_COLS), lambda i, j: (i, j, 0)),
                  state_spec((H_M, DH_M, DH_M)),
                  state_spec((H_M, 1, DH_M)),
                  state_spec((H_M, 1, LANES))],
        out_specs=[pl.BlockSpec((bb_n, blk, W_M), lambda i, j: (i, j, 0)),
                   state_spec((H_M, DH_M, DH_M)),
                   state_spec((H_M, 1, DH_M)),
                   state_spec((H_M, 1, LANES))],
        out_shape=[jax.ShapeDtypeStruct((b, t, W_M), BF16),
                   jax.ShapeDtypeStruct((b, H_M, DH_M, DH_M), F32),
                   jax.ShapeDtypeStruct((b, H_M, 1, DH_M), F32),
                   jax.ShapeDtypeStruct((b, H_M, 1, LANES), F32)],
        scratch_shapes=[pltpu.VMEM((bb_n, H_M, DH_M, DH_M), F32),
                        pltpu.VMEM((bb_n, H_M, 1, DH_M), F32),
                        pltpu.VMEM((bb_n, H_M, 1, LANES), F32)],
        compiler_params=_cparams("parallel", "arbitrary"),
        name="mlstm",
    )(qkv, ogate, gates, c0, n0, m0)


def _online_softmax_step(s, v_blk, m_ref, l_ref, acc_ref):
    m_old = m_ref[...]
    m_new = jnp.maximum(m_old, jnp.max(s, axis=1, keepdims=True))
    alpha = jnp.exp(m_old - m_new)
    p = jnp.exp(s - m_new)
    l_ref[...] = alpha * l_ref[...] + jnp.sum(p, axis=1, keepdims=True)
    acc_ref[...] = alpha * acc_ref[...] + jnp.dot(p.astype(BF16), v_blk, preferred_element_type=F32)
    m_ref[...] = m_new


def _fox_prompt_kernel(q_ref, k_ref, v_ref, c_ref, ct_ref, o_ref, m_s, l_s, acc_s, *, tq, tk):
    hp = pl.program_id(1)
    qi = pl.program_id(2)
    lane = lax.broadcasted_iota(jnp.int32, (tq, LANES), 1)
    q2 = q_ref[...] * (DH_F ** -0.5)
    zero = jnp.zeros_like(q2)
    q_heads = (jnp.where(lane < DH_F, q2, zero), jnp.where(lane >= DH_F, q2, zero))
    c_blk = c_ref[...]
    cq = tuple(jnp.sum(jnp.where(lane == COL_FF + 2 * hp + x, c_blk, 0.0), axis=1, keepdims=True)
               for x in range(2))
    m_s[...] = jnp.full_like(m_s, NEG_INF)
    l_s[...] = jnp.zeros_like(l_s)
    acc_s[...] = jnp.zeros_like(acc_s)

    def block(kb, masked):
        ks = pl.multiple_of(kb * tk, tk)
        k_blk = k_ref[pl.ds(ks, tk), :].astype(BF16)
        v_blk = v_ref[pl.ds(ks, tk), :].astype(BF16)
        ck = ct_ref[:, pl.ds(ks, tk)]
        for x in range(2):
            s = lax.dot_general(q_heads[x], k_blk, (((1,), (1,)), ((), ())), preferred_element_type=F32)
            s = s + cq[x] - ck[x:x + 1, :]
            if masked:
                q_pos = lax.broadcasted_iota(jnp.int32, (tq, tk), 0)
                k_pos = lax.broadcasted_iota(jnp.int32, (tq, tk), 1)
                s = jnp.where(k_pos <= q_pos, s, NEG_INF)
            _online_softmax_step(s, v_blk, m_s.at[x], l_s.at[x], acc_s.at[x])

    def body(kb, carry):
        block(kb, False)
        return carry

    lax.fori_loop(0, qi, body, 0)
    block(qi, True)
    out = [acc_s[x] / l_s[x] for x in range(2)]
    o_ref[...] = jnp.where(lane < DH_F, out[0], out[1]).astype(o_ref.dtype)


def _fox_prompt(q, k, v, c, ct):
    b, t, _ = q.shape
    tq = tk = 512
    return pl.pallas_call(
        functools.partial(_fox_prompt_kernel, tq=tq, tk=tk),
        grid=(b, H_F // 2, t // tq),
        in_specs=[pl.BlockSpec((None, tq, LANES), lambda i, h, j: (i, j, h)),
                  pl.BlockSpec((None, t, LANES), lambda i, h, j: (i, 0, h)),
                  pl.BlockSpec((None, t, LANES), lambda i, h, j: (i, 0, h)),
                  pl.BlockSpec((None, tq, LANES), lambda i, h, j: (i, j, 0)),
                  pl.BlockSpec((None, None, 2, t), lambda i, h, j: (i, h, 0, 0))],
        out_specs=pl.BlockSpec((None, tq, LANES), lambda i, h, j: (i, j, h)),
        out_shape=jax.ShapeDtypeStruct((b, t, W_F), BF16),
        scratch_shapes=[pltpu.VMEM((2, tq, 1), F32),
                        pltpu.VMEM((2, tq, 1), F32),
                        pltpu.VMEM((2, tq, LANES), F32)],
        compiler_params=_cparams("parallel", "parallel", "arbitrary"),
        name="fox_prompt",
    )(q, k, v, c, ct)


def _fox_sample_kernel(q_ref, kc_ref, vc_ref, kn_ref, vn_ref, cq_ref, ctc_ref, ctn_ref, o_ref,
                       m_s, l_s, acc_s, *, tn):
    kb = pl.program_id(1)

    @pl.when(kb == 0)
    def _():
        m_s[...] = jnp.full_like(m_s, NEG_INF)
        l_s[...] = jnp.zeros_like(l_s)
        acc_s[...] = jnp.zeros_like(acc_s)

    lane = lax.broadcasted_iota(jnp.int32, (tn, LANES), 1)
    c_blk = cq_ref[...]

    def heads(hp):
        q2 = q_ref[:, hp * LANES:(hp + 1) * LANES] * (DH_F ** -0.5)
        zero = jnp.zeros_like(q2)
        for x in range(2):
            qx = jnp.where((lane < DH_F) == (x == 0), q2, zero)
            cqx = jnp.sum(jnp.where(lane == COL_FF + 2 * hp + x, c_blk, 0.0), axis=1, keepdims=True)
            yield 2 * hp + x, qx, cqx

    def attend(k_ref, v_ref, ct_ref, masked):
        for hp in range(H_F // 2):
            k_blk = k_ref[:, hp * LANES:(hp + 1) * LANES].astype(BF16)
            v_blk = v_ref[:, hp * LANES:(hp + 1) * LANES].astype(BF16)
            for hd, qx, cqx in heads(hp):
                s = lax.dot_general(qx, k_blk, (((1,), (1,)), ((), ())), preferred_element_type=F32)
                s = s + cqx - ct_ref[hp, hd % 2:hd % 2 + 1, :]
                if masked:
                    q_pos = lax.broadcasted_iota(jnp.int32, s.shape, 0)
                    k_pos = lax.broadcasted_iota(jnp.int32, s.shape, 1)
                    s = jnp.where(k_pos <= q_pos, s, NEG_INF)
                _online_softmax_step(s, v_blk, m_s.at[hd], l_s.at[hd], acc_s.at[hd])

    attend(kc_ref, vc_ref, ctc_ref, False)

    @pl.when(kb == pl.num_programs(1) - 1)
    def _():
        attend(kn_ref, vn_ref, ctn_ref, True)
        for hp in range(H_F // 2):
            out = [acc_s[2 * hp + x] / l_s[2 * hp + x] for x in range(2)]
            o_ref[:, hp * LANES:(hp + 1) * LANES] = jnp.where(lane < DH_F, out[0], out[1]).astype(o_ref.dtype)


def _fox_sample(q, k_cache, v_cache, k_new, v_new, cq, ct_cache, ct_new):
    b, tn, _ = q.shape
    p = k_cache.shape[1]
    tk = 1024
    return pl.pallas_call(
        functools.partial(_fox_sample_kernel, tn=tn),
        grid=(b, p // tk),
        in_specs=[pl.BlockSpec((None, tn, W_F), lambda i, j: (i, 0, 0)),
                  pl.BlockSpec((None, tk, W_F), lambda i, j: (i, j, 0)),
                  pl.BlockSpec((None, tk, W_F), lambda i, j: (i, j, 0)),
                  pl.BlockSpec((None, tn, W_F), lambda i, j: (i, 0, 0)),
                  pl.BlockSpec((None, tn, W_F), lambda i, j: (i, 0, 0)),
                  pl.BlockSpec((None, tn, LANES), lambda i, j: (i, 0, 0)),
                  pl.BlockSpec((None, H_F // 2, 2, tk), lambda i, j: (i, 0, 0, j)),
                  pl.BlockSpec((None, H_F // 2, 2, tn), lambda i, j: (i, 0, 0, 0))],
        out_specs=pl.BlockSpec((None, tn, W_F), lambda i, j: (i, 0, 0)),
        out_shape=jax.ShapeDtypeStruct((b, tn, W_F), BF16),
        scratch_shapes=[pltpu.VMEM((H_F, tn, 1), F32),
                        pltpu.VMEM((H_F, tn, 1), F32),
                        pltpu.VMEM((H_F, tn, LANES), F32)],
        compiler_params=_cparams("parallel", "arbitrary"),
        name="fox_sample",
    )(q, k_cache, v_cache, k_new, v_new, cq, ct_cache, ct_new)


def _merge_kernel(h_ref, ym_ref, yf_ref, gm_ref, gf_ref, wm_ref, wf_ref, wo_ref, o_ref):
    a = jnp.dot(ym_ref[...], wm_ref[...], preferred_element_type=F32)
    b = jnp.dot(yf_ref[...], wf_ref[...], preferred_element_type=F32)
    merge = _sigmoid(gm_ref[...]) * a + _sigmoid(gf_ref[...]) * b
    o_ref[...] = h_ref[...] + jnp.dot(merge.astype(BF16), wo_ref[...], preferred_element_type=F32)


def _merge(h, ym, yf, gates, wm, wf, wo):
    n, d = h.shape
    tm = min(512, n)
    tok = lambda c: pl.BlockSpec((tm, d), lambda i: (i, c))
    wspec = pl.BlockSpec((d, d), lambda i: (0, 0))
    return pl.pallas_call(
        _merge_kernel,
        grid=(n // tm,),
        in_specs=[tok(0), tok(0), tok(0), tok(1), tok(2), wspec, wspec, wspec],
        out_specs=tok(0),
        out_shape=jax.ShapeDtypeStruct((n, d), F32),
        compiler_params=_cparams("parallel"),
        name="merge",
    )(h, ym, yf, gates, gates, wm, wf, wo)


def _top_rows(s, count, payload=None):
    rows = lax.broadcasted_iota(jnp.int32, s.shape, 0)
    big = s.shape[0]
    vals, ids, pay = [], [], []
    for _ in range(count):
        m = jnp.max(s, axis=0, keepdims=True)
        am = jnp.min(jnp.where(s == m, rows, big), axis=0, keepdims=True)
        hit = rows == am
        vals.append(m)
        ids.append(am)
        if payload is not None:
            pay.append(jnp.max(jnp.where(hit, payload, -1), axis=0, keepdims=True))
        s = jnp.where(hit, NEG_INF, s)
    cat = lambda xs: jnp.concatenate(xs, axis=0)
    return cat(vals), cat(ids), (cat(pay) if payload is not None else None)


def _peer_score_kernel(h_ref, g_ref, wqt_ref, k1_ref, k2_ref, xn_ref, idx_ref, gate_ref):
    xn = _rms(h_ref[...], g_ref[...])
    xn_ref[...] = xn.astype(xn_ref.dtype)
    qt = lax.dot_general(wqt_ref[...], xn, (((1,), (1,)), ((), ())),
                         precision=HIGHEST, preferred_element_type=F32)
    half = PEER_DQ // 2
    idx_rows, gate_rows = [], []
    for hd in range(PEER_HEADS):
        q1 = qt[hd * PEER_DQ:hd * PEER_DQ + half, :]
        q2 = qt[hd * PEER_DQ + half:(hd + 1) * PEER_DQ, :]
        s1 = jnp.dot(k1_ref[hd], q1, precision=HIGHEST, preferred_element_type=F32)
        s2 = jnp.dot(k2_ref[hd], q2, precision=HIGHEST, preferred_element_type=F32)
        v1, i1, _ = _top_rows(s1, PEER_TOPK)
        v2, i2, _ = _top_rows(s2, PEER_TOPK)
        cand = jnp.concatenate([v1[a:a + 1, :] + v2 for a in range(PEER_TOPK)], axis=0)
        expert = jnp.concatenate([i1[a:a + 1, :] * PEER_KEYS + i2 for a in range(PEER_TOPK)], axis=0)
        sc, _, ex = _top_rows(cand, PEER_TOPK, payload=expert)
        e = jnp.exp(sc - sc[0:1, :])
        gate_rows.append(e / jnp.sum(e, axis=0, keepdims=True))
        idx_rows.append(ex)
    gate_ref[...] = jnp.concatenate(gate_rows, axis=0)
    idx_ref[...] = jnp.concatenate(idx_rows, axis=0).T


def _peer_score(h, g, wqt, k1, k2):
    n, d = h.shape
    tt = min(256, n)
    return pl.pallas_call(
        _peer_score_kernel,
        grid=(n // tt,),
        in_specs=[pl.BlockSpec((tt, d), lambda i: (i, 0)),
                  pl.BlockSpec((1, d), lambda i: (0, 0)),
                  pl.BlockSpec(wqt.shape, lambda i: (0, 0)),
                  pl.BlockSpec(k1.shape, lambda i: (0, 0, 0)),
                  pl.BlockSpec(k2.shape, lambda i: (0, 0, 0))],
        out_specs=[pl.BlockSpec((tt, d), lambda i: (i, 0)),
                   pl.BlockSpec((tt, PEER_SEL), lambda i: (i, 0)),
                   pl.BlockSpec((PEER_SEL, tt), lambda i: (0, i))],
        out_shape=[jax.ShapeDtypeStruct((n, d), BF16),
                   jax.ShapeDtypeStruct((n, PEER_SEL), jnp.int32),
                   jax.ShapeDtypeStruct((PEER_SEL, n), F32)],
        compiler_params=_cparams("parallel"),
        name="peer_score",
    )(h, g, wqt, k1, k2)


SC_CORES = 2
SC_SUBCORES = 16
SC_WORKERS = SC_CORES * SC_SUBCORES
SC_ROWS = 64


def _sc_gather(table, idx):
    r = idx.shape[0]
    w = table.shape[1]
    per_worker = r // SC_WORKERS
    steps = per_worker // SC_ROWS
    mesh = plsc.VectorSubcoreMesh(core_axis_name="c", subcore_axis_name="s")

    @functools.partial(
        pl.kernel, mesh=mesh,
        out_type=jax.ShapeDtypeStruct((r, w), table.dtype),
        scratch_types=[pltpu.VMEM((SC_ROWS,), jnp.int32),
                       pltpu.VMEM((SC_ROWS, w), table.dtype),
                       pltpu.SemaphoreType.DMA],
    )
    def gather(table_hbm, idx_hbm, out_hbm, idx_v, rows_v, sem):
        wid = lax.axis_index("s") * SC_CORES + lax.axis_index("c")
        base = wid * per_worker

        @pl.loop(0, steps)
        def _(i):
            off = pl.multiple_of(base + i * SC_ROWS, SC_ROWS)
            pltpu.sync_copy(idx_hbm.at[pl.ds(off, SC_ROWS)], idx_v)
            pltpu.async_copy(table_hbm.at[idx_v], rows_v, sem).wait()
            pltpu.sync_copy(rows_v, out_hbm.at[pl.ds(off, SC_ROWS)])

    return gather(table, idx)


def _pack_table(tab):
    half = tab.shape[1] // 2
    bits = lax.bitcast_convert_type(tab.astype(BF16), jnp.uint16).astype(jnp.uint32)
    word = bits[:, :half] | (bits[:, half:] << 16)
    return lax.bitcast_convert_type(word, jnp.int32)


def _unpack_lo(wd):
    return lax.bitcast_convert_type(wd << 16, F32)


def _unpack_hi(wd):
    return lax.bitcast_convert_type(wd & jnp.int32(-65536), F32)


def _gelu_exact(x):
    return 0.5 * x * (1.0 + lax.erf(x * (2.0 ** -0.5)))


def _peer_apply_kernel(h_ref, xn_ref, gate_ref, ug_ref, vg_ref, o_ref, act_s, *, tt):
    half = D_MODEL // 2
    for t in range(tt):
        x = xn_ref[t:t + 1, :].astype(F32)
        wd = ug_ref[t]
        prod = _unpack_lo(wd) * x[:, :half] + _unpack_hi(wd) * x[:, half:]
        act_s[:, t:t + 1] = jnp.sum(prod, axis=1, keepdims=True)
    act_s[...] = gate_ref[...] * _gelu_exact(act_s[...])
    for t in range(tt):
        wt = act_s[:, t:t + 1]
        wd = vg_ref[t]
        lo = jnp.sum(wt * _unpack_lo(wd), axis=0, keepdims=True)
        hi = jnp.sum(wt * _unpack_hi(wd), axis=0, keepdims=True)
        o_ref[t:t + 1, :] = h_ref[t:t + 1, :] + jnp.concatenate([lo, hi], axis=1)


PEER_TOKENS = 16


def _peer_apply(h, xn, gate_blocks, ug, vg):
    n, d = h.shape
    tt = PEER_TOKENS
    return pl.pallas_call(
        functools.partial(_peer_apply_kernel, tt=tt),
        grid=(n // tt,),
        in_specs=[pl.BlockSpec((tt, d), lambda i: (i, 0)),
                  pl.BlockSpec((tt, d), lambda i: (i, 0)),
                  pl.BlockSpec((None, PEER_SEL, tt), lambda i: (i, 0, 0)),
                  pl.BlockSpec((tt, PEER_SEL, d // 2), lambda i: (i, 0, 0)),
                  pl.BlockSpec((tt, PEER_SEL, d // 2), lambda i: (i, 0, 0))],
        out_specs=pl.BlockSpec((tt, d), lambda i: (i, 0)),
        out_shape=jax.ShapeDtypeStruct((n, d), F32),
        scratch_shapes=[pltpu.VMEM((PEER_SEL, tt), F32)],
        compiler_params=_cparams("parallel"),
        name="peer_apply",
    )(h, xn, gate_blocks, ug, vg)


def _ple_kernel(h_ref, p_ref, gp_ref, gfin_ref, wg_ref, wp_ref, o_ref):
    h = h_ref[...]
    e = _rms(h, gp_ref[...]).astype(BF16)
    gate = _sigmoid(jnp.dot(e, wg_ref[...], preferred_element_type=F32))
    proj = jnp.dot(p_ref[...].astype(BF16), wp_ref[...], preferred_element_type=F32)
    o_ref[...] = _rms(h + gate * proj, gfin_ref[...])


def _ple_final(h, p, g_ple, g_final, wg, wp):
    n, d = h.shape
    tm = min(512, n)
    return pl.pallas_call(
        _ple_kernel,
        grid=(n // tm,),
        in_specs=[pl.BlockSpec((tm, d), lambda i: (i, 0)),
                  pl.BlockSpec((tm, D_PLE), lambda i: (i, 0)),
                  pl.BlockSpec((1, d), lambda i: (0, 0)),
                  pl.BlockSpec((1, d), lambda i: (0, 0)),
                  pl.BlockSpec((d, d), lambda i: (0, 0)),
                  pl.BlockSpec((D_PLE, d), lambda i: (0, 0))],
        out_specs=pl.BlockSpec((tm, d), lambda i: (i, 0)),
        out_shape=jax.ShapeDtypeStruct((n, d), F32),
        compiler_params=_cparams("parallel"),
        name="ple_final",
    )(h, p, g_ple, g_final, wg, wp)


def _prep_weights(norm_mix, w_in, mlstm_b_i, mlstm_b_f, fox_b_f, w_br_m, w_br_f, w_out, norm_ffn,
                  peer_w_q, peer_keys1, peer_keys2, peer_u, peer_v, norm_ple, w_ple_gate, w_ple_proj,
                  norm_final):
    o = [0]
    for s in (W_M, W_M, W_M, W_M, H_M, H_M, W_F, W_F, W_F, H_F, D_MODEL, D_MODEL):
        o.append(o[-1] + s)
    seg = lambda a, b: w_in[:, o[a]:o[b]]
    w_gate = jnp.concatenate([seg(4, 6), seg(9, 10)], axis=1)
    w_gate = jnp.pad(w_gate, ((0, 0), (0, GATE_COLS - w_gate.shape[1])))
    b_gate = jnp.concatenate([mlstm_b_i, mlstm_b_f, fox_b_f]).astype(F32)
    b_gate = jnp.pad(b_gate, (0, GATE_COLS - b_gate.shape[0]))[None, :]
    row = lambda v: v.astype(F32)[None, :]
    return dict(
        norm_mix=row(norm_mix),
        w_mqkv=seg(0, 3).astype(BF16),
        w_og=jnp.concatenate([seg(3, 4), seg(10, 12)], axis=1).astype(BF16),
        w_fq=seg(6, 7).astype(BF16), w_fk=seg(7, 8).astype(BF16), w_fv=seg(8, 9).astype(BF16),
        w_gate=w_gate.astype(F32), b_gate=b_gate,
        w_br_m=w_br_m.astype(BF16), w_br_f=w_br_f.astype(BF16), w_out=w_out.astype(BF16),
        norm_ffn=row(norm_ffn), wqt=peer_w_q.T.astype(F32),
        k1=peer_keys1.astype(F32), k2=peer_keys2.astype(F32),
        u_pk=_pack_table(peer_u), v_pk=_pack_table(peer_v),
        norm_ple=row(norm_ple), w_ple_gate=w_ple_gate.astype(BF16), w_ple_proj=w_ple_proj.astype(BF16),
        norm_final=row(norm_final),
    )


def _ct_rows(c):
    b, t, _ = c.shape
    return jnp.swapaxes(c[:, :, COL_FF:COL_FF + H_F], 1, 2).reshape(b, H_F // 2, 2, t)


def _peer_block(h, w):
    n = h.shape[0]
    xn, idx, gate_t = _peer_score(h, w["norm_ffn"], w["wqt"], w["k1"], w["k2"])
    gate_blocks = jnp.swapaxes(gate_t.reshape(PEER_SEL, n // PEER_TOKENS, PEER_TOKENS), 0, 1)
    chunk = min(4096, n)
    outs = []
    for s in range(0, n, chunk):
        flat = idx[s:s + chunk].reshape(-1)
        ug = _sc_gather(w["u_pk"], flat).reshape(chunk, PEER_SEL, D_MODEL // 2)
        vg = _sc_gather(w["v_pk"], flat).reshape(chunk, PEER_SEL, D_MODEL // 2)
        outs.append(_peer_apply(h[s:s + chunk], xn[s:s + chunk],
                                gate_blocks[s // PEER_TOKENS:(s + chunk) // PEER_TOKENS], ug, vg))
    return outs[0] if len(outs) == 1 else jnp.concatenate(outs, axis=0)


def _layer(x, p, w, mstate, fox_cache):
    b, t, d = x.shape
    n = b * t
    h = x.reshape(n, d)
    g = w["norm_mix"]
    qkv = _norm_matmul(h, g, w["w_mqkv"], BF16, "proj_mlstm_qkv")
    og = _norm_matmul(h, g, w["w_og"], F32, "proj_gates")
    fq = _norm_matmul(h, g, w["w_fq"], BF16, "proj_fox_q")
    fk = _norm_matmul(h, g, w["w_fk"], F32, "proj_fox_k")
    fv = _norm_matmul(h, g, w["w_fv"], F32, "proj_fox_v")
    gates = _gates(h, g, w["w_gate"], w["b_gate"])

    c0, n0, m0 = mstate
    ym, c_new, n_new, m_new = _mlstm(
        qkv.reshape(b, t, 3 * W_M), og.reshape(b, t, 3 * D_MODEL), gates.reshape(b, t, GATE_COLS),
        c0.astype(F32), n0.astype(F32)[:, :, None, :],
        jnp.broadcast_to(m0.astype(F32)[:, :, None, None], (b, H_M, 1, LANES)))

    gates3 = gates.reshape(b, t, GATE_COLS)
    if fox_cache is None:
        c = _cumsum_tokens(gates3)
        yf = _fox_prompt(fq.reshape(b, t, W_F), fk.reshape(b, t, W_F), fv.reshape(b, t, W_F), c, _ct_rows(c))
    else:
        ck, cv, clf = fox_cache
        past = ck.shape[1]
        lf = jnp.pad(clf.astype(F32), ((0, 0), (0, 0), (COL_FF, GATE_COLS - COL_FF - H_F)))
        lf = jnp.concatenate([lf, gates3], axis=1)
        pad_t = (-lf.shape[1]) % 256
        c = _cumsum_tokens(jnp.pad(lf, ((0, 0), (0, pad_t), (0, 0))))
        ct = _ct_rows(c)
        yf = _fox_sample(fq.reshape(b, t, W_F), ck.reshape(b, past, W_F).astype(F32),
                         cv.reshape(b, past, W_F).astype(F32), fk.reshape(b, t, W_F), fv.reshape(b, t, W_F),
                         c[:, past:past + t, :], ct[..., :past], ct[..., past:past + t])

    h = _merge(h, ym.reshape(n, W_M), yf.reshape(n, W_F), og, w["w_br_m"], w["w_br_f"], w["w_out"])
    h = _peer_block(h, w)
    y = _ple_final(h, p.reshape(n, D_PLE), w["norm_ple"], w["norm_final"], w["w_ple_gate"], w["w_ple_proj"])
    state = (fk.reshape(1, b, t, H_F, DH_F), fv.reshape(1, b, t, H_F, DH_F),
             gates3[None, :, :, COL_FF:COL_FF + H_F],
             c_new[None], n_new[None, :, :, 0, :], m_new[None, :, :, 0, 0])
    return y.reshape(b, t, d), state


def kernel(x_prompt, x_sample, p_prompt, p_sample, cache_fox_k, cache_fox_v, cache_fox_logf, state_mlstm_C, state_mlstm_n, state_mlstm_m, norm_mix, w_in, mlstm_b_i, mlstm_b_f, fox_b_f, w_br_m, w_br_f, w_out, norm_ffn, peer_w_q, peer_keys1, peer_keys2, peer_u, peer_v, norm_ple, w_ple_gate, w_ple_proj, norm_final):
    assert w_in.shape[0] == 1, "single-layer trunk"
    w = _prep_weights(norm_mix[0], w_in[0], mlstm_b_i[0], mlstm_b_f[0], fox_b_f[0], w_br_m[0], w_br_f[0],
                      w_out[0], norm_ffn[0], peer_w_q[0], peer_keys1[0], peer_keys2[0], peer_u[0], peer_v[0],
                      norm_ple[0], w_ple_gate[0], w_ple_proj[0], norm_final)
    bp = x_prompt.shape[0]
    zeros = (jnp.zeros((bp, H_M, DH_M, DH_M), F32), jnp.zeros((bp, H_M, DH_M), F32), jnp.zeros((bp, H_M), F32))
    y_p, sp = _layer(x_prompt, p_prompt[0], w, zeros, None)
    y_s, ss = _layer(x_sample, p_sample[0], w,
                     (state_mlstm_C[0], state_mlstm_n[0], state_mlstm_m[0]),
                     (cache_fox_k[0], cache_fox_v[0], cache_fox_logf[0]))
    return (y_p, y_s) + sp + ss
```

```python
import functools

import jax
import jax.numpy as jnp
from jax import lax
from jax.experimental import pallas as pl
from jax.experimental.pallas import tpu as pltpu
from jax.experimental.pallas import tpu_sc as plsc

D_MODEL = 1024
CHUNK = 64
H_M = 4
DH_M = 256
W_M = H_M * DH_M
H_F = 16
DH_F = 64
W_F = H_F * DH_F
D_PLE = 256
PEER_HEADS = 8
PEER_KEYS = 128
PEER_DQ = 256
PEER_TOPK = 16
PEER_SEL = PEER_HEADS * PEER_TOPK
EPS = 1e-6

LANES = 128
GATE_COLS = LANES
COL_I, COL_F, COL_FF = 0, H_M, 2 * H_M
VMEM_LIMIT = 56 * 1024 * 1024
HIGHEST = lax.Precision.HIGHEST
F32 = jnp.float32
BF16 = jnp.bfloat16
NEG_INF = float("-inf")


def _cparams(*sem):
    return pltpu.CompilerParams(dimension_semantics=sem, vmem_limit_bytes=VMEM_LIMIT)


def _rms(x, g):
    return x * lax.rsqrt(jnp.mean(x * x, axis=-1, keepdims=True) + EPS) * g


def _sigmoid(x):
    return 1.0 / (1.0 + jnp.exp(-x))


def _norm_matmul_kernel(x_ref, g_ref, w_ref, o_ref, a_ref):
    @pl.when(pl.program_id(1) == 0)
    def _():
        a_ref[...] = _rms(x_ref[...], g_ref[...]).astype(a_ref.dtype)

    o_ref[...] = jnp.dot(a_ref[...], w_ref[...], preferred_element_type=F32).astype(o_ref.dtype)


def _norm_matmul(x, g, w, out_dtype, name):
    n, d = x.shape
    cols = w.shape[1]
    tm = min(1024, n)
    tn = 1024
    return pl.pallas_call(
        _norm_matmul_kernel,
        grid=(n // tm, cols // tn),
        in_specs=[pl.BlockSpec((tm, d), lambda i, j: (i, 0)),
                  pl.BlockSpec((1, d), lambda i, j: (0, 0)),
                  pl.BlockSpec((d, tn), lambda i, j: (0, j))],
        out_specs=pl.BlockSpec((tm, tn), lambda i, j: (i, j)),
        out_shape=jax.ShapeDtypeStruct((n, cols), out_dtype),
        scratch_shapes=[pltpu.VMEM((tm, d), BF16)],
        compiler_params=_cparams("parallel", "arbitrary"),
        name=name,
    )(x, g, w)


def _norm_matmul_heads_kernel(x_ref, g_ref, w_ref, o_ref, oh_ref):
    a = _rms(x_ref[...], g_ref[...]).astype(BF16)
    z = jnp.dot(a, w_ref[...], preferred_element_type=F32)
    o_ref[...] = z.astype(o_ref.dtype)
    for h in range(H_F):
        oh_ref[:, h, :] = z[:, h * DH_F:(h + 1) * DH_F]


def _norm_matmul_heads(x, g, w, name):
    n, d = x.shape
    tm = min(512, n)
    return pl.pallas_call(
        _norm_matmul_heads_kernel,
        grid=(n // tm,),
        in_specs=[pl.BlockSpec((tm, d), lambda i: (i, 0)),
                  pl.BlockSpec((1, d), lambda i: (0, 0)),
                  pl.BlockSpec((d, W_F), lambda i: (0, 0))],
        out_specs=[pl.BlockSpec((tm, W_F), lambda i: (i, 0)),
                   pl.BlockSpec((tm, H_F, DH_F), lambda i: (i, 0, 0))],
        out_shape=[jax.ShapeDtypeStruct((n, W_F), BF16),
                   jax.ShapeDtypeStruct((n, H_F, DH_F), F32)],
        compiler_params=_cparams("parallel"),
        name=name,
    )(x, g, w)


def _gate_kernel(x_ref, g_ref, w_ref, b_ref, o_ref):
    a = _rms(x_ref[...], g_ref[...])
    z = jnp.dot(a, w_ref[...], precision=HIGHEST, preferred_element_type=F32) + b_ref[...]
    col = lax.broadcasted_iota(jnp.int32, z.shape, 1)
    log_sig = jnp.minimum(z, 0.0) - jnp.log1p(jnp.exp(-jnp.abs(z)))
    o_ref[...] = jnp.where(col < COL_F, z, log_sig)


def _gates(x, g, w, b):
    n, d = x.shape
    tm = min(512, n)
    return pl.pallas_call(
        _gate_kernel,
        grid=(n // tm,),
        in_specs=[pl.BlockSpec((tm, d), lambda i: (i, 0)),
                  pl.BlockSpec((1, d), lambda i: (0, 0)),
                  pl.BlockSpec((d, GATE_COLS), lambda i: (0, 0)),
                  pl.BlockSpec((1, GATE_COLS), lambda i: (0, 0))],
        out_specs=pl.BlockSpec((tm, GATE_COLS), lambda i: (i, 0)),
        out_shape=jax.ShapeDtypeStruct((n, GATE_COLS), F32),
        compiler_params=_cparams("parallel"),
        name="gates",
    )(x, g, w, b)


def _cumsum_kernel(x_ref, o_ref, carry_ref):
    @pl.when(pl.program_id(1) == 0)
    def _():
        carry_ref[...] = jnp.zeros_like(carry_ref)

    x = x_ref[...]
    tb = x.shape[0]
    row = lax.broadcasted_iota(jnp.int32, (tb, tb), 0)
    col = lax.broadcasted_iota(jnp.int32, (tb, tb), 1)
    tril = jnp.where(col <= row, 1.0, 0.0).astype(F32)
    c = jnp.dot(tril, x, precision=HIGHEST, preferred_element_type=F32) + carry_ref[...]
    o_ref[...] = c
    carry_ref[...] = c[tb - 1:tb, :]


def _cumsum_tokens(x):
    b, t, w = x.shape
    tb = 256
    return pl.pallas_call(
        _cumsum_kernel,
        grid=(b, t // tb),
        in_specs=[pl.BlockSpec((None, tb, w), lambda i, j: (i, j, 0))],
        out_specs=pl.BlockSpec((None, tb, w), lambda i, j: (i, j, 0)),
        out_shape=jax.ShapeDtypeStruct((b, t, w), F32),
        scratch_shapes=[pltpu.VMEM((1, w), F32)],
        compiler_params=_cparams("parallel", "arbitrary"),
        name="cumsum",
    )(x)


def _mlstm_kernel(qkv_ref, og_ref, g_ref, c0_ref, n0_ref, m0_ref,
                  y_ref, cn_ref, nn_ref, mn_ref, c_s, n_s, m_s, *, bb_n, blk):
    step = pl.program_id(1)

    @pl.when(step == 0)
    def _():
        c_s[...] = c0_ref[...]
        n_s[...] = n0_ref[...]
        m_s[...] = m0_ref[...]

    row = lax.broadcasted_iota(jnp.int32, (blk, blk), 0)
    col = lax.broadcasted_iota(jnp.int32, (blk, blk), 1)
    tril = col <= row
    triu = row <= col
    eye = col == row

    def to_row(x_col):
        return jnp.sum(jnp.where(eye, x_col, 0.0), axis=0, keepdims=True)

    for bb in range(bb_n):
        g = g_ref[bb]
        for h in range(H_M):
            q = qkv_ref[bb, :, h * DH_M:(h + 1) * DH_M]
            k = qkv_ref[bb, :, (H_M + h) * DH_M:(H_M + h + 1) * DH_M] * (DH_M ** -0.5)
            v = qkv_ref[bb, :, (2 * H_M + h) * DH_M:(2 * H_M + h + 1) * DH_M]
            i_col = g[:, COL_I + h:COL_I + h + 1]
            f_col = g[:, COL_F + h:COL_F + h + 1]
            i_row = to_row(i_col)
            f_row = to_row(f_col)
            b_col = jnp.sum(jnp.where(tril, f_row, 0.0), axis=1, keepdims=True)
            b_row = jnp.sum(jnp.where(triu, f_col, 0.0), axis=0, keepdims=True)
            m_prev = m_s[bb, h][:, 0:1]
            dmat = jnp.where(tril, b_col - b_row + i_row, NEG_INF)
            g_col = b_col + m_prev
            mt = jnp.maximum(g_col, jnp.max(dmat, axis=1, keepdims=True))
            w_d = jnp.exp(dmat - mt)
            w_g = jnp.exp(g_col - mt)
            qk = lax.dot_general(q, k, (((1,), (1,)), ((), ())), preferred_element_type=F32) * w_d
            c_prev = c_s[bb, h]
            n_prev = n_s[bb, h]
            num = (w_g * jnp.dot(q, c_prev.astype(BF16), preferred_element_type=F32)
                   + jnp.dot(qk.astype(BF16), v, preferred_element_type=F32))
            den = (w_g * jnp.sum(q.astype(F32) * n_prev, axis=1, keepdims=True)
                   + jnp.sum(qk, axis=1, keepdims=True))
            hid = num / jnp.maximum(jnp.abs(den), jnp.exp(-mt))
            o_gate = _sigmoid(og_ref[bb, :, h * DH_M:(h + 1) * DH_M])
            y_ref[bb, :, h * DH_M:(h + 1) * DH_M] = (o_gate * hid).astype(y_ref.dtype)
            m_new = mt[blk - 1:blk, :]
            b_last = b_col[blk - 1:blk, :]
            w_c = jnp.exp(b_last + m_prev - m_new)
            w_s = jnp.exp(b_last - b_col + i_col - m_new)
            kw = k.astype(F32) * w_s
            c_s[bb, h] = w_c * c_prev + lax.dot_general(
                kw.astype(BF16), v, (((0,), (0,)), ((), ())), preferred_element_type=F32)
            n_s[bb, h] = w_c * n_prev + jnp.sum(kw, axis=0, keepdims=True)
            m_s[bb, h] = jnp.broadcast_to(m_new, (1, LANES))

    @pl.when(step == pl.num_programs(1) - 1)
    def _():
        cn_ref[...] = c_s[...]
        nn_ref[...] = n_s[...]
        mn_ref[...] = m_s[...]


def _mlstm(qkv, ogate, gates, c0, n0, m0):
    b, t, _ = qkv.shape
    blk = min(CHUNK, t)
    bb_n = 4
    state_spec = lambda shape: pl.BlockSpec((bb_n,) + shape, lambda i, j: (i, 0, 0, 0))
    return pl.pallas_call(
        functools.partial(_mlstm_kernel, bb_n=bb_n, blk=blk),
        grid=(b // bb_n, t // blk),
        in_specs=[pl.BlockSpec((bb_n, blk, 3 * W_M), lambda i, j: (i, j, 0)),
                  pl.BlockSpec((bb_n, blk, W_M), lambda i, j: (i, j, 0)),
                  pl.BlockSpec((bb_n, blk, GATE_COLS), lambda i, j: (i, j, 0)),
                  state_spec((H_M, DH_M, DH_M)),
                  state_spec((H_M, 1, DH_M)),
                  state_spec((H_M, 1, LANES))],
        out_specs=[pl.BlockSpec((bb_n, blk, W_M), lambda i, j: (i, j, 0)),
                   state_spec((H_M, DH_M, DH_M)),
                   state_spec((H_M, 1, DH_M)),
                   state_spec((H_M, 1, LANES))],
        out_shape=[jax.ShapeDtypeStruct((b, t, W_M), BF16),
                   jax.ShapeDtypeStruct((b, H_M, DH_M, DH_M), F32),
                   jax.ShapeDtypeStruct((b, H_M, 1, DH_M), F32),
                   jax.ShapeDtypeStruct((b, H_M, 1, LANES), F32)],
        scratch_shapes=[pltpu.VMEM((bb_n, H_M, DH_M, DH_M), F32),
                        pltpu.VMEM((bb_n, H_M, 1, DH_M), F32),
                        pltpu.VMEM((bb_n, H_M, 1, LANES), F32)],
        compiler_params=_cparams("parallel", "arbitrary"),
        name="mlstm",
    )(qkv, ogate, gates, c0, n0, m0)


def _online_softmax_step(s, v_blk, m_ref, l_ref, acc_ref):
    m_old = m_ref[...]
    m_new = jnp.maximum(m_old, jnp.max(s, axis=1, keepdims=True))
    alpha = jnp.exp(m_old - m_new)
    p = jnp.exp(s - m_new)
    l_ref[...] = alpha * l_ref[...] + jnp.sum(p, axis=1, keepdims=True)
    acc_ref[...] = alpha * acc_ref[...] + jnp.dot(p.astype(BF16), v_blk, preferred_element_type=F32)
    m_ref[...] = m_new


def _split3(x):
    hi = x.astype(BF16).astype(F32)
    mid = (x - hi).astype(BF16).astype(F32)
    lo = (x - hi - mid).astype(BF16).astype(F32)
    return hi, mid, lo


FOX_V_ROWS = DH_F + 16
AUG_CQ = 6


def _fox_prompt_kernel(q_ref, k_ref, v_ref, c_ref, o_ref, kaug_s, vt_s, m0_s, m1_s, acc0_s, acc1_s,
                       *, tq, n_blocks):
    hp = pl.program_id(1)
    qi = pl.program_id(2)
    lane = lax.broadcasted_iota(jnp.int32, (tq, LANES), 1)
    m_s, acc_s = (m0_s, m1_s), (acc0_s, acc1_s)

    def head_col(c_blk, x):
        return jnp.sum(jnp.where(lane == COL_FF + 2 * hp + x, c_blk, 0.0), axis=1, keepdims=True)

    def aug_tile(entries):
        tile = jnp.zeros((tq, LANES), F32)
        for l, val in entries:
            tile = jnp.where(lane == l, val, tile)
        return tile.astype(BF16)

    @pl.when(qi == 0)
    def _():
        ones_rows = jnp.where(lax.broadcasted_iota(jnp.int32, (FOX_V_ROWS - DH_F, tq), 0) == 0, 1.0, 0.0)

        def chunk(i, carry):
            rs = pl.multiple_of(i * tq, tq)
            c_blk = c_ref[pl.ds(rs, tq), :]
            pieces = _split3(head_col(c_blk, 0)) + _split3(head_col(c_blk, 1))
            entries = list(enumerate(pieces)) + [(AUG_CQ + j, 1.0) for j in range(3)]
            kaug_s[pl.ds(rs, tq), 0:LANES] = k_ref[pl.ds(rs, tq), :]
            kaug_s[pl.ds(rs, tq), LANES:2 * LANES] = aug_tile(entries)
            vt = v_ref[pl.ds(rs, tq), :].astype(F32).T
            for x in range(2):
                vt_s[x, 0:DH_F, pl.ds(rs, tq)] = vt[x * DH_F:(x + 1) * DH_F, :].astype(BF16)
                vt_s[x, DH_F:FOX_V_ROWS, pl.ds(rs, tq)] = ones_rows.astype(BF16)
            return carry

        lax.fori_loop(0, n_blocks, chunk, 0)

    qs = pl.multiple_of(qi * tq, tq)
    c_q = c_ref[pl.ds(qs, tq), :]
    q2 = q_ref[...] * (DH_F ** -0.5)
    q_aug = []
    for x in range(2):
        cq3 = _split3(head_col(c_q, x))
        entries = [(3 * x + j, -1.0) for j in range(3)] + [(AUG_CQ + j, cq3[j]) for j in range(3)]
        q_head = jnp.where((lane < DH_F) == (x == 0), q2, jnp.zeros_like(q2))
        q_aug.append(jnp.concatenate([q_head, aug_tile(entries)], axis=1))
        m_s[x][...] = jnp.full(m_s[x].shape, NEG_INF, F32)
        acc_s[x][...] = jnp.zeros(acc_s[x].shape, F32)

    def block(kb, masked):
        ks = pl.multiple_of(kb * tq, tq)
        k_blk = kaug_s[pl.ds(ks, tq), :]
        scores = [lax.dot_general(k_blk, q_aug[x], (((1,), (1,)), ((), ())), preferred_element_type=F32)
                  for x in range(2)]
        for x in range(2):
            st = scores[x]
            if masked:
                k_pos = lax.broadcasted_iota(jnp.int32, (tq, tq), 0)
                q_pos = lax.broadcasted_iota(jnp.int32, (tq, tq), 1)
                st = jnp.where(k_pos <= q_pos, st, NEG_INF)
            m_old = m_s[x][...]
            m_new = jnp.maximum(m_old, jnp.max(st, axis=0, keepdims=True))
            p = jnp.exp(st - m_new).astype(BF16)
            acc_s[x][...] = (jnp.exp(m_old - m_new) * acc_s[x][...]
                             + jnp.dot(vt_s[x, :, pl.ds(ks, tq)], p, preferred_element_type=F32))
            m_s[x][...] = m_new

    def body(kb, carry):
        block(kb, False)
        return carry

    lax.fori_loop(0, qi, body, 0)
    block(qi, True)
    out_t = jnp.concatenate([acc_s[x][0:DH_F, :] / acc_s[x][DH_F:DH_F + 1, :] for x in range(2)], axis=0)
    o_ref[...] = out_t.T.astype(o_ref.dtype)


def _fox_prompt(q, k, v, c):
    b, t, _ = q.shape
    tq = 512
    head_pair = lambda i, h, j: (i, 0, h)
    return pl.pallas_call(
        functools.partial(_fox_prompt_kernel, tq=tq, n_blocks=t // tq),
        grid=(b, H_F // 2, t // tq),
        in_specs=[pl.BlockSpec((None, tq, LANES), lambda i, h, j: (i, j, h)),
                  pl.BlockSpec((None, t, LANES), head_pair),
                  pl.BlockSpec((None, t, LANES), head_pair),
                  pl.BlockSpec((None, t, LANES), lambda i, h, j: (i, 0, 0))],
        out_specs=pl.BlockSpec((None, tq, LANES), lambda i, h, j: (i, j, h)),
        out_shape=jax.ShapeDtypeStruct((b, t, W_F), BF16),
        scratch_shapes=[pltpu.VMEM((t, 2 * LANES), BF16),
                        pltpu.VMEM((2, FOX_V_ROWS, t), BF16),
                        pltpu.VMEM((1, tq), F32), pltpu.VMEM((1, tq), F32),
                        pltpu.VMEM((FOX_V_ROWS, tq), F32), pltpu.VMEM((FOX_V_ROWS, tq), F32)],
        compiler_params=_cparams("parallel", "parallel", "arbitrary"),
        name="fox_prompt",
    )(q, k, v, c)


def _fox_sample_kernel(q_ref, kc_ref, vc_ref, kn_ref, vn_ref, cq_ref, ctc_ref, ctn_ref, o_ref,
                       m_s, l_s, acc_s, *, tn):
    kb = pl.program_id(1)

    @pl.when(kb == 0)
    def _():
        m_s[...] = jnp.full_like(m_s, NEG_INF)
        l_s[...] = jnp.zeros_like(l_s)
        acc_s[...] = jnp.zeros_like(acc_s)

    lane = lax.broadcasted_iota(jnp.int32, (tn, LANES), 1)
    c_blk = cq_ref[...]

    def heads(hp):
        q2 = q_ref[:, hp * LANES:(hp + 1) * LANES] * (DH_F ** -0.5)
        zero = jnp.zeros_like(q2)
        for x in range(2):
            qx = jnp.where((lane < DH_F) == (x == 0), q2, zero)
            cqx = jnp.sum(jnp.where(lane == COL_FF + 2 * hp + x, c_blk, 0.0), axis=1, keepdims=True)
            yield 2 * hp + x, qx, cqx

    def attend(k_ref, v_ref, ct_ref, masked):
        for hp in range(H_F // 2):
            k_blk = k_ref[:, hp * LANES:(hp + 1) * LANES].astype(BF16)
            v_blk = v_ref[:, hp * LANES:(hp + 1) * LANES].astype(BF16)
            for hd, qx, cqx in heads(hp):
                s = lax.dot_general(qx, k_blk, (((1,), (1,)), ((), ())), preferred_element_type=F32)
                s = s + cqx - ct_ref[hp, hd % 2:hd % 2 + 1, :]
                if masked:
                    q_pos = lax.broadcasted_iota(jnp.int32, s.shape, 0)
                    k_pos = lax.broadcasted_iota(jnp.int32, s.shape, 1)
                    s = jnp.where(k_pos <= q_pos, s, NEG_INF)
                _online_softmax_step(s, v_blk, m_s.at[hd], l_s.at[hd], acc_s.at[hd])

    attend(kc_ref, vc_ref, ctc_ref, False)

    @pl.when(kb == pl.num_programs(1) - 1)
    def _():
        attend(kn_ref, vn_ref, ctn_ref, True)
        for hp in range(H_F // 2):
            out = [acc_s[2 * hp + x] / l_s[2 * hp + x] for x in range(2)]
            o_ref[:, hp * LANES:(hp + 1) * LANES] = jnp.where(lane < DH_F, out[0], out[1]).astype(o_ref.dtype)


def _fox_sample(q, k_cache, v_cache, k_new, v_new, cq, ct_cache, ct_new):
    b, tn, _ = q.shape
    p = k_cache.shape[1]
    tk = 1024
    return pl.pallas_call(
        functools.partial(_fox_sample_kernel, tn=tn),
        grid=(b, p // tk),
        in_specs=[pl.BlockSpec((None, tn, W_F), lambda i, j: (i, 0, 0)),
                  pl.BlockSpec((None, tk, W_F), lambda i, j: (i, j, 0)),
                  pl.BlockSpec((None, tk, W_F), lambda i, j: (i, j, 0)),
                  pl.BlockSpec((None, tn, W_F), lambda i, j: (i, 0, 0)),
                  pl.BlockSpec((None, tn, W_F), lambda i, j: (i, 0, 0)),
                  pl.BlockSpec((None, tn, LANES), lambda i, j: (i, 0, 0)),
                  pl.BlockSpec((None, H_F // 2, 2, tk), lambda i, j: (i, 0, 0, j)),
                  pl.BlockSpec((None, H_F // 2, 2, tn), lambda i, j: (i, 0, 0, 0))],
        out_specs=pl.BlockSpec((None, tn, W_F), lambda i, j: (i, 0, 0)),
        out_shape=jax.ShapeDtypeStruct((b, tn, W_F), BF16),
        scratch_shapes=[pltpu.VMEM((H_F, tn, 1), F32),
                        pltpu.VMEM((H_F, tn, 1), F32),
                        pltpu.VMEM((H_F, tn, LANES), F32)],
        compiler_params=_cparams("parallel", "arbitrary"),
        name="fox_sample",
    )(q, k_cache, v_cache, k_new, v_new, cq, ct_cache, ct_new)


def _merge_kernel(h_ref, ym_ref, yf_ref, gm_ref, gf_ref, wm_ref, wf_ref, wo_ref, o_ref):
    a = jnp.dot(ym_ref[...], wm_ref[...], preferred_element_type=F32)
    b = jnp.dot(yf_ref[...], wf_ref[...], preferred_element_type=F32)
    merge = _sigmoid(gm_ref[...]) * a + _sigmoid(gf_ref[...]) * b
    o_ref[...] = h_ref[...] + jnp.dot(merge.astype(BF16), wo_ref[...], preferred_element_type=F32)


def _merge(h, ym, yf, gates, wm, wf, wo):
    n, d = h.shape
    tm = min(512, n)
    tok = lambda c: pl.BlockSpec((tm, d), lambda i: (i, c))
    wspec = pl.BlockSpec((d, d), lambda i: (0, 0))
    return pl.pallas_call(
        _merge_kernel,
        grid=(n // tm,),
        in_specs=[tok(0), tok(0), tok(0), tok(1), tok(2), wspec, wspec, wspec],
        out_specs=tok(0),
        out_shape=jax.ShapeDtypeStruct((n, d), F32),
        compiler_params=_cparams("parallel"),
        name="merge",
    )(h, ym, yf, gates, gates, wm, wf, wo)


def _top_rows(s, count, ids=None, payload=None):
    if ids is None:
        ids = lax.broadcasted_iota(jnp.int32, s.shape, 0)
    big = jnp.int32(2 ** 30)
    vals, sel, pay = [], [], []
    for _ in range(count):
        m = jnp.max(s, axis=0, keepdims=True)
        am = jnp.min(jnp.where(s == m, ids, big), axis=0, keepdims=True)
        hit = ids == am
        vals.append(m)
        sel.append(am)
        if payload is not None:
            pay.append(jnp.max(jnp.where(hit, payload, -1), axis=0, keepdims=True))
        s = jnp.where(hit, NEG_INF, s)
    cat = lambda xs: jnp.concatenate(xs, axis=0)
    return cat(vals), cat(sel), (cat(pay) if payload is not None else None)


def _pair_candidates(v1, i1, v2, i2):
    t = v1.shape[1]
    half = PEER_TOPK // 2
    r16 = lax.broadcasted_iota(jnp.int32, (PEER_TOPK, t), 0)
    r8 = lax.broadcasted_iota(jnp.int32, (half, t), 0)
    sums = [v1[0:1, :] + v2]
    flat = [r16]
    expert = [i1[0:1, :] * PEER_KEYS + i2]
    for a in range(1, half):
        sums.append(v1[a:a + 1, :] + v2[0:half, :])
        flat.append(r8 + a * PEER_TOPK)
        expert.append(i1[a:a + 1, :] * PEER_KEYS + i2[0:half, :])
    sums.append(v1[half:, :] + v2[0:1, :])
    flat.append((r8 + half) * PEER_TOPK)
    expert.append(i1[half:, :] * PEER_KEYS + i2[0:1, :])
    cat = lambda xs: jnp.concatenate(xs, axis=0)
    return cat(sums), cat(flat), cat(expert)


def _peer_score_kernel(h_ref, g_ref, wqt_ref, k1_ref, k2_ref, xn_ref, idx_ref, gate_ref):
    xn = _rms(h_ref[...], g_ref[...])
    xn_ref[...] = xn.astype(xn_ref.dtype)
    qt = lax.dot_general(wqt_ref[...], xn, (((1,), (1,)), ((), ())),
                         precision=HIGHEST, preferred_element_type=F32)
    half = PEER_DQ // 2
    idx_rows, gate_rows = [], []
    for hd in range(PEER_HEADS):
        q1 = qt[hd * PEER_DQ:hd * PEER_DQ + half, :]
        q2 = qt[hd * PEER_DQ + half:(hd + 1) * PEER_DQ, :]
        s1 = jnp.dot(k1_ref[hd], q1, precision=HIGHEST, preferred_element_type=F32)
        s2 = jnp.dot(k2_ref[hd], q2, precision=HIGHEST, preferred_element_type=F32)
        v1, i1, _ = _top_rows(s1, PEER_TOPK)
        v2, i2, _ = _top_rows(s2, PEER_TOPK)
        cand, flat, expert = _pair_candidates(v1, i1, v2, i2)
        sc, _, ex = _top_rows(cand, PEER_TOPK, ids=flat, payload=expert)
        e = jnp.exp(sc - sc[0:1, :])
        gate_rows.append(e / jnp.sum(e, axis=0, keepdims=True))
        idx_rows.append(ex)
    gate_ref[...] = jnp.concatenate(gate_rows, axis=0)
    idx_ref[...] = jnp.concatenate(idx_rows, axis=0).T


def _peer_score(h, g, wqt, k1, k2):
    n, d = h.shape
    tt = min(256, n)
    return pl.pallas_call(
        _peer_score_kernel,
        grid=(n // tt,),
        in_specs=[pl.BlockSpec((tt, d), lambda i: (i, 0)),
                  pl.BlockSpec((1, d), lambda i: (0, 0)),
                  pl.BlockSpec(wqt.shape, lambda i: (0, 0)),
                  pl.BlockSpec(k1.shape, lambda i: (0, 0, 0)),
                  pl.BlockSpec(k2.shape, lambda i: (0, 0, 0))],
        out_specs=[pl.BlockSpec((tt, d), lambda i: (i, 0)),
                   pl.BlockSpec((tt, PEER_SEL), lambda i: (i, 0)),
                   pl.BlockSpec((PEER_SEL, tt), lambda i: (0, i))],
        out_shape=[jax.ShapeDtypeStruct((n, d), BF16),
                   jax.ShapeDtypeStruct((n, PEER_SEL), jnp.int32),
                   jax.ShapeDtypeStruct((PEER_SEL, n), F32)],
        compiler_params=_cparams("parallel"),
        name="peer_score",
    )(h, g, wqt, k1, k2)


SC_CORES = 2
SC_SUBCORES = 16
SC_WORKERS = SC_CORES * SC_SUBCORES
SC_ROWS = 64


def _sc_gather(table, idx):
    total_steps = idx.shape[0]
    w = table.shape[1]
    steps = total_steps // SC_WORKERS
    assert steps * SC_WORKERS == total_steps and steps % 2 == 0 and idx.shape[1] == SC_ROWS
    mesh = plsc.VectorSubcoreMesh(core_axis_name="c", subcore_axis_name="s")

    @functools.partial(
        pl.kernel, mesh=mesh,
        out_type=jax.ShapeDtypeStruct((total_steps * SC_ROWS, w), table.dtype),
        scratch_types=[pltpu.VMEM((steps, SC_ROWS), jnp.int32),
                       pltpu.VMEM((2, SC_ROWS, w), table.dtype),
                       pltpu.SemaphoreType.DMA((2,))],
    )
    def gather(table_hbm, idx_hbm, out_hbm, idx_v, rows_v, sems):
        wid = lax.axis_index("s") * SC_CORES + lax.axis_index("c")
        first = wid * steps
        pltpu.sync_copy(idx_hbm.at[pl.ds(first, steps)], idx_v)

        def row_gather(j, slot):
            return pltpu.make_async_copy(table_hbm.at[idx_v.at[j]], rows_v.at[slot], sems.at[slot])

        row_gather(0, 0).start()

        @pl.loop(0, steps, step=2)
        def _(i):
            for slot in range(2):
                j = i + slot

                @pl.when(j + 1 < steps)
                def _():
                    row_gather(j + 1, 1 - slot).start()

                row_gather(j, slot).wait()
                off = pl.multiple_of((first + j) * SC_ROWS, SC_ROWS)
                pltpu.sync_copy(rows_v.at[slot], out_hbm.at[pl.ds(off, SC_ROWS)])

    return gather(table, idx)


def _pack_table(tab):
    half = tab.shape[1] // 2
    bits = lax.bitcast_convert_type(tab.astype(BF16), jnp.uint16).astype(jnp.uint32)
    word = bits[:, :half] | (bits[:, half:] << 16)
    return lax.bitcast_convert_type(word, jnp.int32)


def _unpack_lo(wd):
    return lax.bitcast_convert_type(wd << 16, F32)


def _unpack_hi(wd):
    return lax.bitcast_convert_type(wd & jnp.int32(-65536), F32)


def _gelu_exact(x):
    return 0.5 * x * (1.0 + lax.erf(x * (2.0 ** -0.5)))


def _peer_apply_kernel(h_ref, xn_ref, gate_ref, ug_ref, vg_ref, o_ref, act_s, *, tt):
    half = D_MODEL // 2
    for t in range(tt):
        x = xn_ref[t:t + 1, :].astype(F32)
        wd = ug_ref[t]
        prod = _unpack_lo(wd) * x[:, :half] + _unpack_hi(wd) * x[:, half:]
        act_s[:, t:t + 1] = jnp.sum(prod, axis=1, keepdims=True)
    act_s[...] = gate_ref[...] * _gelu_exact(act_s[...])
    for t in range(tt):
        wt = act_s[:, t:t + 1]
        wd = vg_ref[t]
        lo = jnp.sum(wt * _unpack_lo(wd), axis=0, keepdims=True)
        hi = jnp.sum(wt * _unpack_hi(wd), axis=0, keepdims=True)
        o_ref[t:t + 1, :] = h_ref[t:t + 1, :] + jnp.concatenate([lo, hi], axis=1)


PEER_TOKENS = 16


def _peer_apply(h, xn, gate_blocks, ug, vg, first_token):
    n, d = h.shape
    tt = PEER_TOKENS
    first = first_token // tt
    tok = lambda i: (first + i, 0)
    return pl.pallas_call(
        functools.partial(_peer_apply_kernel, tt=tt),
        grid=(ug.shape[0] // tt,),
        in_specs=[pl.BlockSpec((tt, d), tok),
                  pl.BlockSpec((tt, d), tok),
                  pl.BlockSpec((None, PEER_SEL, tt), lambda i: (first + i, 0, 0)),
                  pl.BlockSpec((tt, PEER_SEL, d // 2), lambda i: (i, 0, 0)),
                  pl.BlockSpec((tt, PEER_SEL, d // 2), lambda i: (i, 0, 0))],
        out_specs=pl.BlockSpec((tt, d), tok),
        out_shape=jax.ShapeDtypeStruct((n, d), F32),
        input_output_aliases={0: 0},
        scratch_shapes=[pltpu.VMEM((PEER_SEL, tt), F32)],
        compiler_params=_cparams("parallel"),
        name="peer_apply",
    )(h, xn, gate_blocks, ug, vg)


def _ple_kernel(h_ref, p_ref, gp_ref, gfin_ref, wg_ref, wp_ref, o_ref):
    h = h_ref[...]
    e = _rms(h, gp_ref[...]).astype(BF16)
    gate = _sigmoid(jnp.dot(e, wg_ref[...], preferred_element_type=F32))
    proj = jnp.dot(p_ref[...].astype(BF16), wp_ref[...], preferred_element_type=F32)
    o_ref[...] = _rms(h + gate * proj, gfin_ref[...])


def _ple_final(h, p, g_ple, g_final, wg, wp):
    n, d = h.shape
    tm = min(512, n)
    return pl.pallas_call(
        _ple_kernel,
        grid=(n // tm,),
        in_specs=[pl.BlockSpec((tm, d), lambda i: (i, 0)),
                  pl.BlockSpec((tm, D_PLE), lambda i: (i, 0)),
                  pl.BlockSpec((1, d), lambda i: (0, 0)),
                  pl.BlockSpec((1, d), lambda i: (0, 0)),
                  pl.BlockSpec((d, d), lambda i: (0, 0)),
                  pl.BlockSpec((D_PLE, d), lambda i: (0, 0))],
        out_specs=pl.BlockSpec((tm, d), lambda i: (i, 0)),
        out_shape=jax.ShapeDtypeStruct((n, d), F32),
        compiler_params=_cparams("parallel"),
        name="ple_final",
    )(h, p, g_ple, g_final, wg, wp)


def _prep_weights(norm_mix, w_in, mlstm_b_i, mlstm_b_f, fox_b_f, w_br_m, w_br_f, w_out, norm_ffn,
                  peer_w_q, peer_keys1, peer_keys2, peer_u, peer_v, norm_ple, w_ple_gate, w_ple_proj,
                  norm_final):
    o = [0]
    for s in (W_M, W_M, W_M, W_M, H_M, H_M, W_F, W_F, W_F, H_F, D_MODEL, D_MODEL):
        o.append(o[-1] + s)
    seg = lambda a, b: w_in[:, o[a]:o[b]]
    w_gate = jnp.concatenate([seg(4, 6), seg(9, 10)], axis=1)
    w_gate = jnp.pad(w_gate, ((0, 0), (0, GATE_COLS - w_gate.shape[1])))
    b_gate = jnp.concatenate([mlstm_b_i, mlstm_b_f, fox_b_f]).astype(F32)
    b_gate = jnp.pad(b_gate, (0, GATE_COLS - b_gate.shape[0]))[None, :]
    row = lambda v: v.astype(F32)[None, :]
    return dict(
        norm_mix=row(norm_mix),
        w_mqkv=seg(0, 3).astype(BF16),
        w_og=jnp.concatenate([seg(3, 4), seg(10, 12)], axis=1).astype(BF16),
        w_fq=seg(6, 7).astype(BF16), w_fk=seg(7, 8).astype(BF16), w_fv=seg(8, 9).astype(BF16),
        w_gate=w_gate.astype(F32), b_gate=b_gate,
        w_br_m=w_br_m.astype(BF16), w_br_f=w_br_f.astype(BF16), w_out=w_out.astype(BF16),
        norm_ffn=row(norm_ffn), wqt=peer_w_q.T.astype(F32),
        k1=peer_keys1.astype(F32), k2=peer_keys2.astype(F32),
        u_pk=_pack_table(peer_u), v_pk=_pack_table(peer_v),
        norm_ple=row(norm_ple), w_ple_gate=w_ple_gate.astype(BF16), w_ple_proj=w_ple_proj.astype(BF16),
        norm_final=row(norm_final),
    )


def _ct_rows(c):
    b, t, _ = c.shape
    return jnp.swapaxes(c[:, :, COL_FF:COL_FF + H_F], 1, 2).reshape(b, H_F // 2, 2, t)


def _peer_block(h, w):
    n = h.shape[0]
    xn, idx, gate_t = _peer_score(h, w["norm_ffn"], w["wqt"], w["k1"], w["k2"])
    gate_blocks = jnp.swapaxes(gate_t.reshape(PEER_SEL, n // PEER_TOKENS, PEER_TOKENS), 0, 1)
    steps = idx.reshape(n * PEER_SEL // SC_ROWS, SC_ROWS)
    chunk = min(4096, n)
    chunk_steps = chunk * PEER_SEL // SC_ROWS
    for s in range(0, n, chunk):
        rows = steps[s * PEER_SEL // SC_ROWS:s * PEER_SEL // SC_ROWS + chunk_steps]
        ug = _sc_gather(w["u_pk"], rows).reshape(chunk, PEER_SEL, D_MODEL // 2)
        vg = _sc_gather(w["v_pk"], rows).reshape(chunk, PEER_SEL, D_MODEL // 2)
        h = _peer_apply(h, xn, gate_blocks, ug, vg, s)
    return h


def _layer(x, p, w, mstate, fox_cache):
    b, t, d = x.shape
    n = b * t
    h = x.reshape(n, d)
    g = w["norm_mix"]
    qkv = _norm_matmul(h, g, w["w_mqkv"], BF16, "proj_mlstm_qkv")
    og = _norm_matmul(h, g, w["w_og"], F32, "proj_gates")
    fq = _norm_matmul(h, g, w["w_fq"], BF16, "proj_fox_q")
    fk, fk_heads = _norm_matmul_heads(h, g, w["w_fk"], "proj_fox_k")
    fv, fv_heads = _norm_matmul_heads(h, g, w["w_fv"], "proj_fox_v")
    gates = _gates(h, g, w["w_gate"], w["b_gate"])

    c0, n0, m0 = mstate
    ym, c_new, n_new, m_new = _mlstm(
        qkv.reshape(b, t, 3 * W_M), og.reshape(b, t, 3 * D_MODEL), gates.reshape(b, t, GATE_COLS),
        c0.astype(F32), n0.astype(F32)[:, :, None, :],
        jnp.broadcast_to(m0.astype(F32)[:, :, None, None], (b, H_M, 1, LANES)))

    gates3 = gates.reshape(b, t, GATE_COLS)
    if fox_cache is None:
        c = _cumsum_tokens(gates3)
        yf = _fox_prompt(fq.reshape(b, t, W_F), fk.reshape(b, t, W_F), fv.reshape(b, t, W_F), c)
    else:
        ck, cv, clf = fox_cache
        past = ck.shape[1]
        lf = jnp.pad(clf.astype(F32), ((0, 0), (0, 0), (COL_FF, GATE_COLS - COL_FF - H_F)))
        lf = jnp.concatenate([lf, gates3], axis=1)
        pad_t = (-lf.shape[1]) % 256
        c = _cumsum_tokens(jnp.pad(lf, ((0, 0), (0, pad_t), (0, 0))))
        ct = _ct_rows(c)
        yf = _fox_sample(fq.reshape(b, t, W_F), ck.reshape(b, past, W_F).astype(F32),
                         cv.reshape(b, past, W_F).astype(F32), fk.reshape(b, t, W_F), fv.reshape(b, t, W_F),
                         c[:, past:past + t, :], ct[..., :past], ct[..., past:past + t])

    h = _merge(h, ym.reshape(n, W_M), yf.reshape(n, W_F), og, w["w_br_m"], w["w_br_f"], w["w_out"])
    h = _peer_block(h, w)
    y = _ple_final(h, p.reshape(n, D_PLE), w["norm_ple"], w["norm_final"], w["w_ple_gate"], w["w_ple_proj"])
    state = (fk_heads.reshape(1, b, t, H_F, DH_F), fv_heads.reshape(1, b, t, H_F, DH_F),
             gates3[None, :, :, COL_FF:COL_FF + H_F],
             c_new[None], n_new[None, :, :, 0, :], m_new[None, :, :, 0, 0])
    return y.reshape(b, t, d), state


def kernel(x_prompt, x_sample, p_prompt, p_sample, cache_fox_k, cache_fox_v, cache_fox_logf, state_mlstm_C, state_mlstm_n, state_mlstm_m, norm_mix, w_in, mlstm_b_i, mlstm_b_f, fox_b_f, w_br_m, w_br_f, w_out, norm_ffn, peer_w_q, peer_keys1, peer_keys2, peer_u, peer_v, norm_ple, w_ple_gate, w_ple_proj, norm_final):
    assert w_in.shape[0] == 1, "single-layer trunk"
    w = _prep_weights(norm_mix[0], w_in[0], mlstm_b_i[0], mlstm_b_f[0], fox_b_f[0], w_br_m[0], w_br_f[0],
                      w_out[0], norm_ffn[0], peer_w_q[0], peer_keys1[0], peer_keys2[0], peer_u[0], peer_v[0],
                      norm_ple[0], w_ple_gate[0], w_ple_proj[0], norm_final)
    bp = x_prompt.shape[0]
    zeros = (jnp.zeros((bp, H_M, DH_M, DH_M), F32), jnp.zeros((bp, H_M, DH_M), F32), jnp.zeros((bp, H_M), F32))
    y_p, sp = _layer(x_prompt, p_prompt[0], w, zeros, None)
    y_s, ss = _layer(x_sample, p_sample[0], w,
                     (state_mlstm_C[0], state_mlstm_n[0], state_mlstm_m[0]),
                     (cache_fox_k[0], cache_fox_v[0], cache_fox_logf[0]))
    return (y_p, y_s) + sp + ss
```

```python
import functools

import jax
import jax.numpy as jnp
from jax import lax
from jax.experimental import pallas as pl
from jax.experimental.pallas import tpu as pltpu
from jax.experimental.pallas import tpu_sc as plsc

D_MODEL = 1024
CHUNK = 64
H_M = 4
DH_M = 256
W_M = H_M * DH_M
H_F = 16
DH_F = 64
W_F = H_F * DH_F
D_PLE = 256
PEER_HEADS = 8
PEER_KEYS = 128
PEER_DQ = 256
PEER_TOPK = 16
PEER_SEL = PEER_HEADS * PEER_TOPK
EPS = 1e-6

LANES = 128
GATE_COLS = LANES
COL_I, COL_F, COL_FF = 0, H_M, 2 * H_M
VMEM_LIMIT = 56 * 1024 * 1024
HIGHEST = lax.Precision.HIGHEST
F32 = jnp.float32
BF16 = jnp.bfloat16
NEG_INF = float("-inf")


def _cparams(*sem):
    return pltpu.CompilerParams(dimension_semantics=sem, vmem_limit_bytes=VMEM_LIMIT)


def _rms(x, g):
    return x * lax.rsqrt(jnp.mean(x * x, axis=-1, keepdims=True) + EPS) * g


def _sigmoid(x):
    return 1.0 / (1.0 + jnp.exp(-x))


def _norm_matmul_kernel(x_ref, g_ref, w_ref, o_ref, a_ref):
    @pl.when(pl.program_id(1) == 0)
    def _():
        a_ref[...] = _rms(x_ref[...], g_ref[...]).astype(a_ref.dtype)

    o_ref[...] = jnp.dot(a_ref[...], w_ref[...], preferred_element_type=F32).astype(o_ref.dtype)


def _norm_matmul(x, g, w, out_dtype, name):
    n, d = x.shape
    cols = w.shape[1]
    tm = min(1024, n)
    tn = 1024
    return pl.pallas_call(
        _norm_matmul_kernel,
        grid=(n // tm, cols // tn),
        in_specs=[pl.BlockSpec((tm, d), lambda i, j: (i, 0)),
                  pl.BlockSpec((1, d), lambda i, j: (0, 0)),
                  pl.BlockSpec((d, tn), lambda i, j: (0, j))],
        out_specs=pl.BlockSpec((tm, tn), lambda i, j: (i, j)),
        out_shape=jax.ShapeDtypeStruct((n, cols), out_dtype),
        scratch_shapes=[pltpu.VMEM((tm, d), BF16)],
        compiler_params=_cparams("parallel", "arbitrary"),
        name=name,
    )(x, g, w)


def _norm_matmul_heads_kernel(x_ref, g_ref, w_ref, o_ref, oh_ref):
    a = _rms(x_ref[...], g_ref[...]).astype(BF16)
    z = jnp.dot(a, w_ref[...], preferred_element_type=F32)
    o_ref[...] = z.astype(o_ref.dtype)
    for h in range(H_F):
        oh_ref[:, h, :] = z[:, h * DH_F:(h + 1) * DH_F]


def _norm_matmul_heads(x, g, w, name):
    n, d = x.shape
    tm = min(512, n)
    return pl.pallas_call(
        _norm_matmul_heads_kernel,
        grid=(n // tm,),
        in_specs=[pl.BlockSpec((tm, d), lambda i: (i, 0)),
                  pl.BlockSpec((1, d), lambda i: (0, 0)),
                  pl.BlockSpec((d, W_F), lambda i: (0, 0))],
        out_specs=[pl.BlockSpec((tm, W_F), lambda i: (i, 0)),
                   pl.BlockSpec((tm, H_F, DH_F), lambda i: (i, 0, 0))],
        out_shape=[jax.ShapeDtypeStruct((n, W_F), BF16),
                   jax.ShapeDtypeStruct((n, H_F, DH_F), F32)],
        compiler_params=_cparams("parallel"),
        name=name,
    )(x, g, w)


def _gate_kernel(x_ref, g_ref, w_ref, b_ref, o_ref):
    a = _rms(x_ref[...], g_ref[...])
    z = jnp.dot(a, w_ref[...], precision=HIGHEST, preferred_element_type=F32) + b_ref[...]
    col = lax.broadcasted_iota(jnp.int32, z.shape, 1)
    log_sig = jnp.minimum(z, 0.0) - jnp.log1p(jnp.exp(-jnp.abs(z)))
    o_ref[...] = jnp.where(col < COL_F, z, log_sig)


def _gates(x, g, w, b):
    n, d = x.shape
    tm = min(512, n)
    return pl.pallas_call(
        _gate_kernel,
        grid=(n // tm,),
        in_specs=[pl.BlockSpec((tm, d), lambda i: (i, 0)),
                  pl.BlockSpec((1, d), lambda i: (0, 0)),
                  pl.BlockSpec((d, GATE_COLS), lambda i: (0, 0)),
                  pl.BlockSpec((1, GATE_COLS), lambda i: (0, 0))],
        out_specs=pl.BlockSpec((tm, GATE_COLS), lambda i: (i, 0)),
        out_shape=jax.ShapeDtypeStruct((n, GATE_COLS), F32),
        compiler_params=_cparams("parallel"),
        name="gates",
    )(x, g, w, b)


def _cumsum_kernel(x_ref, o_ref, carry_ref):
    @pl.when(pl.program_id(1) == 0)
    def _():
        carry_ref[...] = jnp.zeros_like(carry_ref)

    x = x_ref[...]
    tb = x.shape[0]
    row = lax.broadcasted_iota(jnp.int32, (tb, tb), 0)
    col = lax.broadcasted_iota(jnp.int32, (tb, tb), 1)
    tril = jnp.where(col <= row, 1.0, 0.0).astype(F32)
    c = jnp.dot(tril, x, precision=HIGHEST, preferred_element_type=F32) + carry_ref[...]
    o_ref[...] = c
    carry_ref[...] = c[tb - 1:tb, :]


def _cumsum_tokens(x):
    b, t, w = x.shape
    tb = 256
    return pl.pallas_call(
        _cumsum_kernel,
        grid=(b, t // tb),
        in_specs=[pl.BlockSpec((None, tb, w), lambda i, j: (i, j, 0))],
        out_specs=pl.BlockSpec((None, tb, w), lambda i, j: (i, j, 0)),
        out_shape=jax.ShapeDtypeStruct((b, t, w), F32),
        scratch_shapes=[pltpu.VMEM((1, w), F32)],
        compiler_params=_cparams("parallel", "arbitrary"),
        name="cumsum",
    )(x)


def _mlstm_kernel(qkv_ref, og_ref, g_ref, c0_ref, n0_ref, m0_ref,
                  y_ref, cn_ref, nn_ref, mn_ref, c_s, n_s, m_s, *, bb_n, blk):
    step = pl.program_id(1)

    @pl.when(step == 0)
    def _():
        c_s[...] = c0_ref[...]
        n_s[...] = n0_ref[...]
        m_s[...] = m0_ref[...]

    row = lax.broadcasted_iota(jnp.int32, (blk, blk), 0)
    col = lax.broadcasted_iota(jnp.int32, (blk, blk), 1)
    tril = col <= row
    triu = row <= col
    eye = col == row

    def to_row(x_col):
        return jnp.sum(jnp.where(eye, x_col, 0.0), axis=0, keepdims=True)

    for bb in range(bb_n):
        g = g_ref[bb]
        for h in range(H_M):
            q = qkv_ref[bb, :, h * DH_M:(h + 1) * DH_M]
            k = qkv_ref[bb, :, (H_M + h) * DH_M:(H_M + h + 1) * DH_M] * (DH_M ** -0.5)
            v = qkv_ref[bb, :, (2 * H_M + h) * DH_M:(2 * H_M + h + 1) * DH_M]
            i_col = g[:, COL_I + h:COL_I + h + 1]
            f_col = g[:, COL_F + h:COL_F + h + 1]
            i_row = to_row(i_col)
            f_row = to_row(f_col)
            b_col = jnp.sum(jnp.where(tril, f_row, 0.0), axis=1, keepdims=True)
            b_row = jnp.sum(jnp.where(triu, f_col, 0.0), axis=0, keepdims=True)
            m_prev = m_s[bb, h][:, 0:1]
            dmat = jnp.where(tril, b_col - b_row + i_row, NEG_INF)
            g_col = b_col + m_prev
            mt = jnp.maximum(g_col, jnp.max(dmat, axis=1, keepdims=True))
            w_d = jnp.exp(dmat - mt)
            w_g = jnp.exp(g_col - mt)
            qk = lax.dot_general(q, k, (((1,), (1,)), ((), ())), preferred_element_type=F32) * w_d
            c_prev = c_s[bb, h]
            n_prev = n_s[bb, h]
            num = (w_g * jnp.dot(q, c_prev.astype(BF16), preferred_element_type=F32)
                   + jnp.dot(qk.astype(BF16), v, preferred_element_type=F32))
            den = (w_g * jnp.sum(q.astype(F32) * n_prev, axis=1, keepdims=True)
                   + jnp.sum(qk, axis=1, keepdims=True))
            hid = num / jnp.maximum(jnp.abs(den), jnp.exp(-mt))
            o_gate = _sigmoid(og_ref[bb, :, h * DH_M:(h + 1) * DH_M])
            y_ref[bb, :, h * DH_M:(h + 1) * DH_M] = (o_gate * hid).astype(y_ref.dtype)
            m_new = mt[blk - 1:blk, :]
            b_last = b_col[blk - 1:blk, :]
            w_c = jnp.exp(b_last + m_prev - m_new)
            w_s = jnp.exp(b_last - b_col + i_col - m_new)
            kw = k.astype(F32) * w_s
            c_s[bb, h] = w_c * c_prev + lax.dot_general(
                kw.astype(BF16), v, (((0,), (0,)), ((), ())), preferred_element_type=F32)
            n_s[bb, h] = w_c * n_prev + jnp.sum(kw, axis=0, keepdims=True)
            m_s[bb, h] = jnp.broadcast_to(m_new, (1, LANES))

    @pl.when(step == pl.num_programs(1) - 1)
    def _():
        cn_ref[...] = c_s[...]
        nn_ref[...] = n_s[...]
        mn_ref[...] = m_s[...]


def _mlstm(qkv, ogate, gates, c0, n0, m0):
    b, t, _ = qkv.shape
    blk = min(CHUNK, t)
    bb_n = 4
    state_spec = lambda shape: pl.BlockSpec((bb_n,) + shape, lambda i, j: (i, 0, 0, 0))
    return pl.pallas_call(
        functools.partial(_mlstm_kernel, bb_n=bb_n, blk=blk),
        grid=(b // bb_n, t // blk),
        in_specs=[pl.BlockSpec((bb_n, blk, 3 * W_M), lambda i, j: (i, j, 0)),
                  pl.BlockSpec((bb_n, blk, W_M), lambda i, j: (i, j, 0)),
                  pl.BlockSpec((bb_n, blk, GATE_COLS), lambda i, j: (i, j, 0)),
                  state_spec((H_M, DH_M, DH_M)),
                  state_spec((H_M, 1, DH_M)),
                  state_spec((H_M, 1, LANES))],
        out_specs=[pl.BlockSpec((bb_n, blk, W_M), lambda i, j: (i, j, 0)),
                   state_spec((H_M, DH_M, DH_M)),
                   state_spec((H_M, 1, DH_M)),
                   state_spec((H_M, 1, LANES))],
        out_shape=[jax.ShapeDtypeStruct((b, t, W_M), BF16),
                   jax.ShapeDtypeStruct((b, H_M, DH_M, DH_M), F32),
                   jax.ShapeDtypeStruct((b, H_M, 1, DH_M), F32),
                   jax.ShapeDtypeStruct((b, H_M, 1, LANES), F32)],
        scratch_shapes=[pltpu.VMEM((bb_n, H_M, DH_M, DH_M), F32),
                        pltpu.VMEM((bb_n, H_M, 1, DH_M), F32),
                        pltpu.VMEM((bb_n, H_M, 1, LANES), F32)],
        compiler_params=_cparams("parallel", "arbitrary"),
        name="mlstm",
    )(qkv, ogate, gates, c0, n0, m0)


def _online_softmax_step(s, v_blk, m_ref, l_ref, acc_ref):
    m_old = m_ref[...]
    m_new = jnp.maximum(m_old, jnp.max(s, axis=1, keepdims=True))
    alpha = jnp.exp(m_old - m_new)
    p = jnp.exp(s - m_new)
    l_ref[...] = alpha * l_ref[...] + jnp.sum(p, axis=1, keepdims=True)
    acc_ref[...] = alpha * acc_ref[...] + jnp.dot(p.astype(BF16), v_blk, preferred_element_type=F32)
    m_ref[...] = m_new


def _split3(x):
    hi = x.astype(BF16).astype(F32)
    mid = (x - hi).astype(BF16).astype(F32)
    lo = (x - hi - mid).astype(BF16).astype(F32)
    return hi, mid, lo


FOX_V_ROWS = DH_F + 16
AUG_CQ = 6


def _fox_prompt_kernel(q_ref, k_ref, v_ref, c_ref, o_ref, kaug_s, vt_s, m0_s, m1_s, acc0_s, acc1_s,
                       *, tq, n_blocks):
    hp = pl.program_id(1)
    qi = pl.program_id(2)
    lane = lax.broadcasted_iota(jnp.int32, (tq, LANES), 1)
    m_s, acc_s = (m0_s, m1_s), (acc0_s, acc1_s)

    def head_col(c_blk, x):
        return jnp.sum(jnp.where(lane == COL_FF + 2 * hp + x, c_blk, 0.0), axis=1, keepdims=True)

    def aug_tile(entries):
        tile = jnp.zeros((tq, LANES), F32)
        for l, val in entries:
            tile = jnp.where(lane == l, val, tile)
        return tile.astype(BF16)

    @pl.when(qi == 0)
    def _():
        ones_rows = jnp.where(lax.broadcasted_iota(jnp.int32, (FOX_V_ROWS - DH_F, tq), 0) == 0, 1.0, 0.0)

        def chunk(i, carry):
            rs = pl.multiple_of(i * tq, tq)
            c_blk = c_ref[pl.ds(rs, tq), :]
            pieces = _split3(head_col(c_blk, 0)) + _split3(head_col(c_blk, 1))
            entries = list(enumerate(pieces)) + [(AUG_CQ + j, 1.0) for j in range(3)]
            kaug_s[pl.ds(rs, tq), 0:LANES] = k_ref[pl.ds(rs, tq), :]
            kaug_s[pl.ds(rs, tq), LANES:2 * LANES] = aug_tile(entries)
            vt = v_ref[pl.ds(rs, tq), :].astype(F32).T
            for x in range(2):
                vt_s[x, 0:DH_F, pl.ds(rs, tq)] = vt[x * DH_F:(x + 1) * DH_F, :].astype(BF16)
                vt_s[x, DH_F:FOX_V_ROWS, pl.ds(rs, tq)] = ones_rows.astype(BF16)
            return carry

        lax.fori_loop(0, n_blocks, chunk, 0)

    qs = pl.multiple_of(qi * tq, tq)
    c_q = c_ref[pl.ds(qs, tq), :]
    q2 = q_ref[...] * (DH_F ** -0.5)
    q_aug = []
    for x in range(2):
        cq3 = _split3(head_col(c_q, x))
        entries = [(3 * x + j, -1.0) for j in range(3)] + [(AUG_CQ + j, cq3[j]) for j in range(3)]
        q_head = jnp.where((lane < DH_F) == (x == 0), q2, jnp.zeros_like(q2))
        q_aug.append(jnp.concatenate([q_head, aug_tile(entries)], axis=1))
        m_s[x][...] = jnp.full(m_s[x].shape, NEG_INF, F32)
        acc_s[x][...] = jnp.zeros(acc_s[x].shape, F32)

    def block(kb, masked):
        ks = pl.multiple_of(kb * tq, tq)
        k_blk = kaug_s[pl.ds(ks, tq), :]
        scores = [lax.dot_general(k_blk, q_aug[x], (((1,), (1,)), ((), ())), preferred_element_type=F32)
                  for x in range(2)]
        for x in range(2):
            st = scores[x]
            if masked:
                k_pos = lax.broadcasted_iota(jnp.int32, (tq, tq), 0)
                q_pos = lax.broadcasted_iota(jnp.int32, (tq, tq), 1)
                st = jnp.where(k_pos <= q_pos, st, NEG_INF)
            m_old = m_s[x][...]
            m_new = jnp.maximum(m_old, jnp.max(st, axis=0, keepdims=True))
            p = jnp.exp(st - m_new).astype(BF16)
            acc_s[x][...] = (jnp.exp(m_old - m_new) * acc_s[x][...]
                             + jnp.dot(vt_s[x, :, pl.ds(ks, tq)], p, preferred_element_type=F32))
            m_s[x][...] = m_new

    def body(kb, carry):
        block(kb, False)
        return carry

    lax.fori_loop(0, qi, body, 0)
    block(qi, True)
    out_t = jnp.concatenate([acc_s[x][0:DH_F, :] / acc_s[x][DH_F:DH_F + 1, :] for x in range(2)], axis=0)
    o_ref[...] = out_t.T.astype(o_ref.dtype)


def _fox_prompt(q, k, v, c):
    b, t, _ = q.shape
    tq = 512
    head_pair = lambda i, h, j: (i, 0, h)
    return pl.pallas_call(
        functools.partial(_fox_prompt_kernel, tq=tq, n_blocks=t // tq),
        grid=(b, H_F // 2, t // tq),
        in_specs=[pl.BlockSpec((None, tq, LANES), lambda i, h, j: (i, j, h)),
                  pl.BlockSpec((None, t, LANES), head_pair),
                  pl.BlockSpec((None, t, LANES), head_pair),
                  pl.BlockSpec((None, t, LANES), lambda i, h, j: (i, 0, 0))],
        out_specs=pl.BlockSpec((None, tq, LANES), lambda i, h, j: (i, j, h)),
        out_shape=jax.ShapeDtypeStruct((b, t, W_F), BF16),
        scratch_shapes=[pltpu.VMEM((t, 2 * LANES), BF16),
                        pltpu.VMEM((2, FOX_V_ROWS, t), BF16),
                        pltpu.VMEM((1, tq), F32), pltpu.VMEM((1, tq), F32),
                        pltpu.VMEM((FOX_V_ROWS, tq), F32), pltpu.VMEM((FOX_V_ROWS, tq), F32)],
        compiler_params=_cparams("parallel", "parallel", "arbitrary"),
        name="fox_prompt",
    )(q, k, v, c)


def _fox_sample_kernel(q_ref, kc_ref, vc_ref, kn_ref, vn_ref, cq_ref, ctc_ref, ctn_ref, o_ref,
                       m_s, l_s, acc_s, *, tn):
    kb = pl.program_id(1)

    @pl.when(kb == 0)
    def _():
        m_s[...] = jnp.full_like(m_s, NEG_INF)
        l_s[...] = jnp.zeros_like(l_s)
        acc_s[...] = jnp.zeros_like(acc_s)

    lane = lax.broadcasted_iota(jnp.int32, (tn, LANES), 1)
    c_blk = cq_ref[...]

    def heads(hp):
        q2 = q_ref[:, hp * LANES:(hp + 1) * LANES] * (DH_F ** -0.5)
        zero = jnp.zeros_like(q2)
        for x in range(2):
            qx = jnp.where((lane < DH_F) == (x == 0), q2, zero)
            cqx = jnp.sum(jnp.where(lane == COL_FF + 2 * hp + x, c_blk, 0.0), axis=1, keepdims=True)
            yield 2 * hp + x, qx, cqx

    def attend(k_ref, v_ref, ct_ref, masked):
        for hp in range(H_F // 2):
            k_blk = k_ref[:, hp * LANES:(hp + 1) * LANES].astype(BF16)
            v_blk = v_ref[:, hp * LANES:(hp + 1) * LANES].astype(BF16)
            for hd, qx, cqx in heads(hp):
                s = lax.dot_general(qx, k_blk, (((1,), (1,)), ((), ())), preferred_element_type=F32)
                s = s + cqx - ct_ref[hp, hd % 2:hd % 2 + 1, :]
                if masked:
                    q_pos = lax.broadcasted_iota(jnp.int32, s.shape, 0)
                    k_pos = lax.broadcasted_iota(jnp.int32, s.shape, 1)
                    s = jnp.where(k_pos <= q_pos, s, NEG_INF)
                _online_softmax_step(s, v_blk, m_s.at[hd], l_s.at[hd], acc_s.at[hd])

    attend(kc_ref, vc_ref, ctc_ref, False)

    @pl.when(kb == pl.num_programs(1) - 1)
    def _():
        attend(kn_ref, vn_ref, ctn_ref, True)
        for hp in range(H_F // 2):
            out = [acc_s[2 * hp + x] / l_s[2 * hp + x] for x in range(2)]
            o_ref[:, hp * LANES:(hp + 1) * LANES] = jnp.where(lane < DH_F, out[0], out[1]).astype(o_ref.dtype)


def _fox_sample(q, k_cache, v_cache, k_new, v_new, cq, ct_cache, ct_new):
    b, tn, _ = q.shape
    p = k_cache.shape[1]
    tk = 1024
    return pl.pallas_call(
        functools.partial(_fox_sample_kernel, tn=tn),
        grid=(b, p // tk),
        in_specs=[pl.BlockSpec((None, tn, W_F), lambda i, j: (i, 0, 0)),
                  pl.BlockSpec((None, tk, W_F), lambda i, j: (i, j, 0)),
                  pl.BlockSpec((None, tk, W_F), lambda i, j: (i, j, 0)),
                  pl.BlockSpec((None, tn, W_F), lambda i, j: (i, 0, 0)),
                  pl.BlockSpec((None, tn, W_F), lambda i, j: (i, 0, 0)),
                  pl.BlockSpec((None, tn, LANES), lambda i, j: (i, 0, 0)),
                  pl.BlockSpec((None, H_F // 2, 2, tk), lambda i, j: (i, 0, 0, j)),
                  pl.BlockSpec((None, H_F // 2, 2, tn), lambda i, j: (i, 0, 0, 0))],
        out_specs=pl.BlockSpec((None, tn, W_F), lambda i, j: (i, 0, 0)),
        out_shape=jax.ShapeDtypeStruct((b, tn, W_F), BF16),
        scratch_shapes=[pltpu.VMEM((H_F, tn, 1), F32),
                        pltpu.VMEM((H_F, tn, 1), F32),
                        pltpu.VMEM((H_F, tn, LANES), F32)],
        compiler_params=_cparams("parallel", "arbitrary"),
        name="fox_sample",
    )(q, k_cache, v_cache, k_new, v_new, cq, ct_cache, ct_new)


def _merge_kernel(h_ref, ym_ref, yf_ref, gm_ref, gf_ref, wm_ref, wf_ref, wo_ref, o_ref):
    a = jnp.dot(ym_ref[...], wm_ref[...], preferred_element_type=F32)
    b = jnp.dot(yf_ref[...], wf_ref[...], preferred_element_type=F32)
    merge = _sigmoid(gm_ref[...]) * a + _sigmoid(gf_ref[...]) * b
    o_ref[...] = h_ref[...] + jnp.dot(merge.astype(BF16), wo_ref[...], preferred_element_type=F32)


def _merge(h, ym, yf, gates, wm, wf, wo):
    n, d = h.shape
    tm = min(512, n)
    tok = lambda c: pl.BlockSpec((tm, d), lambda i: (i, c))
    wspec = pl.BlockSpec((d, d), lambda i: (0, 0))
    return pl.pallas_call(
        _merge_kernel,
        grid=(n // tm,),
        in_specs=[tok(0), tok(0), tok(0), tok(1), tok(2), wspec, wspec, wspec],
        out_specs=tok(0),
        out_shape=jax.ShapeDtypeStruct((n, d), F32),
        compiler_params=_cparams("parallel"),
        name="merge",
    )(h, ym, yf, gates, gates, wm, wf, wo)


def _top_rows(s, count, ids=None, payload=None):
    if ids is None:
        ids = lax.broadcasted_iota(jnp.int32, s.shape, 0)
    big = jnp.int32(2 ** 30)
    vals, sel, pay = [], [], []
    for _ in range(count):
        m = jnp.max(s, axis=0, keepdims=True)
        am = jnp.min(jnp.where(s == m, ids, big), axis=0, keepdims=True)
        hit = ids == am
        vals.append(m)
        sel.append(am)
        if payload is not None:
            pay.append(jnp.max(jnp.where(hit, payload, -1), axis=0, keepdims=True))
        s = jnp.where(hit, NEG_INF, s)
    cat = lambda xs: jnp.concatenate(xs, axis=0)
    return cat(vals), cat(sel), (cat(pay) if payload is not None else None)


def _pair_candidates(v1, i1, v2, i2):
    t = v1.shape[1]
    half = PEER_TOPK // 2
    r16 = lax.broadcasted_iota(jnp.int32, (PEER_TOPK, t), 0)
    r8 = lax.broadcasted_iota(jnp.int32, (half, t), 0)
    sums = [v1[0:1, :] + v2]
    flat = [r16]
    expert = [i1[0:1, :] * PEER_KEYS + i2]
    for a in range(1, half):
        sums.append(v1[a:a + 1, :] + v2[0:half, :])
        flat.append(r8 + a * PEER_TOPK)
        expert.append(i1[a:a + 1, :] * PEER_KEYS + i2[0:half, :])
    sums.append(v1[half:, :] + v2[0:1, :])
    flat.append((r8 + half) * PEER_TOPK)
    expert.append(i1[half:, :] * PEER_KEYS + i2[0:1, :])
    cat = lambda xs: jnp.concatenate(xs, axis=0)
    return cat(sums), cat(flat), cat(expert)


def _peer_score_kernel(h_ref, g_ref, wqt_ref, k1_ref, k2_ref, xn_ref, idx_ref, gate_ref):
    xn = _rms(h_ref[...], g_ref[...])
    xn_ref[...] = xn.astype(xn_ref.dtype)
    qt = lax.dot_general(wqt_ref[...], xn, (((1,), (1,)), ((), ())),
                         precision=HIGHEST, preferred_element_type=F32)
    half = PEER_DQ // 2
    idx_rows, gate_rows = [], []
    for hd in range(PEER_HEADS):
        q1 = qt[hd * PEER_DQ:hd * PEER_DQ + half, :]
        q2 = qt[hd * PEER_DQ + half:(hd + 1) * PEER_DQ, :]
        s1 = jnp.dot(k1_ref[hd], q1, precision=HIGHEST, preferred_element_type=F32)
        s2 = jnp.dot(k2_ref[hd], q2, precision=HIGHEST, preferred_element_type=F32)
        v1, i1, _ = _top_rows(s1, PEER_TOPK)
        v2, i2, _ = _top_rows(s2, PEER_TOPK)
        cand, flat, expert = _pair_candidates(v1, i1, v2, i2)
        sc, _, ex = _top_rows(cand, PEER_TOPK, ids=flat, payload=expert)
        e = jnp.exp(sc - sc[0:1, :])
        gate_rows.append(e / jnp.sum(e, axis=0, keepdims=True))
        idx_rows.append(ex)
    gate_ref[...] = jnp.concatenate(gate_rows, axis=0).T
    idx_ref[...] = jnp.concatenate(idx_rows, axis=0).T


def _peer_score(h, g, wqt, k1, k2):
    n, d = h.shape
    tt = min(256, n)
    return pl.pallas_call(
        _peer_score_kernel,
        grid=(n // tt,),
        in_specs=[pl.BlockSpec((tt, d), lambda i: (i, 0)),
                  pl.BlockSpec((1, d), lambda i: (0, 0)),
                  pl.BlockSpec(wqt.shape, lambda i: (0, 0)),
                  pl.BlockSpec(k1.shape, lambda i: (0, 0, 0)),
                  pl.BlockSpec(k2.shape, lambda i: (0, 0, 0))],
        out_specs=[pl.BlockSpec((tt, d), lambda i: (i, 0)),
                   pl.BlockSpec((tt, PEER_SEL), lambda i: (i, 0)),
                   pl.BlockSpec((tt, PEER_SEL), lambda i: (i, 0))],
        out_shape=[jax.ShapeDtypeStruct((n, d), F32),
                   jax.ShapeDtypeStruct((n, PEER_SEL), jnp.int32),
                   jax.ShapeDtypeStruct((n, PEER_SEL), F32)],
        compiler_params=_cparams("parallel"),
        name="peer_score",
    )(h, g, wqt, k1, k2)


SC_CORES = 2
SC_SUBCORES = 16
SC_WORKERS = SC_CORES * SC_SUBCORES
SC_ROWS = 64


def _pack_table(tab):
    half = tab.shape[1] // 2
    bits = lax.bitcast_convert_type(tab.astype(BF16), jnp.uint16).astype(jnp.uint32)
    word = bits[:, :half] | (bits[:, half:] << 16)
    return lax.bitcast_convert_type(word, jnp.int32)


SC_TOK = 8
SC_LANES = 16
STEPS_PER_TOKEN = PEER_SEL // SC_ROWS
SC_PARAMS = pltpu.CompilerParams(needs_layout_passes=False)


def _row_source(table_hbm, idx_v, local_step, global_step):
    del global_step
    return table_hbm.at[idx_v.at[local_step]]


def _sc_unpack(wd):
    lo = lax.bitcast_convert_type(wd << 16, F32)
    hi = lax.bitcast_convert_type(wd & jnp.int32(-65536), F32)
    return lo, hi


def _sc_token_blocks(n, body_block):
    per_worker = n // SC_WORKERS
    assert per_worker * SC_WORKERS == n and per_worker % SC_TOK == 0
    wid = lax.axis_index("s") * SC_CORES + lax.axis_index("c")

    @pl.loop(0, per_worker // SC_TOK)
    def _(blk):
        body_block(wid * per_worker + blk * SC_TOK)


def _sc_pipelined_steps(table_hbm, idx_v, rows_v, sems, first_step, consume):
    n_steps = SC_TOK * STEPS_PER_TOKEN

    def row_gather(j, slot):
        return pltpu.make_async_copy(_row_source(table_hbm, idx_v, j, first_step + j),
                                     rows_v.at[slot], sems.at[slot])

    row_gather(0, 0).start()

    @pl.loop(0, n_steps, step=2)
    def _(i):
        for slot in range(2):
            j = i + slot

            @pl.when(j + 1 < n_steps)
            def _():
                row_gather(j + 1, 1 - slot).start()

            row_gather(j, slot).wait()
            consume(slot, i // 2, slot)


def _sc_expert_dot(table, idx_steps, x):
    n, d = x.shape
    w = d // 2
    assert STEPS_PER_TOKEN == 2 and table.shape[1] == w
    mesh = plsc.VectorSubcoreMesh(core_axis_name="c", subcore_axis_name="s")
    n_steps = SC_TOK * STEPS_PER_TOKEN
    group = 4

    @functools.partial(
        pl.kernel, mesh=mesh,
        out_type=jax.ShapeDtypeStruct((n, PEER_SEL), F32),
        scratch_types=[pltpu.VMEM((n_steps, SC_ROWS), jnp.int32),
                       pltpu.VMEM((SC_TOK, d), F32),
                       pltpu.VMEM((2, SC_ROWS, w), jnp.int32),
                       pltpu.VMEM((SC_TOK, PEER_SEL), F32),
                       pltpu.SemaphoreType.DMA((2,))],
        compiler_params=SC_PARAMS,
    )
    def expert_dot(table_hbm, idx_hbm, x_hbm, act_hbm, idx_v, x_v, rows_v, act_v, sems):
        lanes = lax.iota(jnp.int32, SC_LANES)
        zero = jnp.zeros((SC_LANES,), F32)

        def consume(slot, tl, half):
            @pl.loop(0, SC_ROWS // SC_LANES)
            def _(g):
                act_vec = zero
                for q in range(SC_LANES // group):
                    r0 = g * SC_LANES + q * group

                    def chunk(j, accs):
                        c = pl.multiple_of(j * SC_LANES, SC_LANES)
                        x_lo = x_v[tl, pl.ds(c, SC_LANES)]
                        x_hi = x_v[tl, pl.ds(w + c, SC_LANES)]
                        out = []
                        for rr in range(group):
                            lo, hi = _sc_unpack(rows_v[slot, r0 + rr, pl.ds(c, SC_LANES)])
                            out.append(accs[rr] + lo * x_lo + hi * x_hi)
                        return tuple(out)

                    accs = lax.fori_loop(0, w // SC_LANES, chunk, (zero,) * group, unroll=2)
                    for rr in range(group):
                        act_vec = jnp.where(lanes == q * group + rr, jnp.sum(accs[rr]), act_vec)
                act_v[tl, pl.ds(half * SC_ROWS + g * SC_LANES, SC_LANES)] = act_vec

        def block(t0):
            s0 = t0 * STEPS_PER_TOKEN
            pltpu.sync_copy(idx_hbm.at[pl.ds(s0, n_steps)], idx_v)
            pltpu.sync_copy(x_hbm.at[pl.ds(t0, SC_TOK)], x_v)
            _sc_pipelined_steps(table_hbm, idx_v, rows_v, sems, s0, consume)
            pltpu.sync_copy(act_v, act_hbm.at[pl.ds(t0, SC_TOK)])

        _sc_token_blocks(n, block)

    return expert_dot(table, idx_steps, x)


def _sc_expert_sum(table, idx_steps, wgt):
    n = wgt.shape[0]
    w = table.shape[1]
    d = 2 * w
    assert STEPS_PER_TOKEN == 2
    mesh = plsc.VectorSubcoreMesh(core_axis_name="c", subcore_axis_name="s")
    n_steps = SC_TOK * STEPS_PER_TOKEN
    cols = 8

    @functools.partial(
        pl.kernel, mesh=mesh,
        out_type=jax.ShapeDtypeStruct((n, d), F32),
        scratch_types=[pltpu.VMEM((n_steps, SC_ROWS), jnp.int32),
                       pltpu.VMEM((SC_TOK, PEER_SEL), F32),
                       pltpu.VMEM((2, SC_ROWS, w), jnp.int32),
                       pltpu.VMEM((SC_TOK, d), F32),
                       pltpu.VMEM((SC_ROWS, SC_LANES), F32),
                       pltpu.SemaphoreType.DMA((2,))],
        compiler_params=SC_PARAMS,
    )
    def expert_sum(table_hbm, idx_hbm, wgt_hbm, out_hbm, idx_v, wgt_v, rows_v, out_v, splat_v, sems):
        zero = jnp.zeros((SC_LANES,), F32)
        lanes = lax.iota(jnp.int32, SC_LANES)

        def consume(slot, tl, half):
            @pl.loop(0, SC_ROWS // SC_LANES)
            def _(g):
                w16 = wgt_v[tl, pl.ds(half * SC_ROWS + g * SC_LANES, SC_LANES)]
                for rr in range(SC_LANES):
                    one = jnp.sum(jnp.where(lanes == rr, w16, 0.0))
                    splat_v[g * SC_LANES + rr, :] = jnp.full((SC_LANES,), one, F32)

            for cb in range(w // (cols * SC_LANES)):
                base = cb * cols * SC_LANES

                def row(r, accs):
                    wv = splat_v[r, :]
                    out = []
                    for jj in range(cols):
                        lo, hi = _sc_unpack(rows_v[slot, r, pl.ds(base + jj * SC_LANES, SC_LANES)])
                        out.append(accs[2 * jj] + wv * lo)
                        out.append(accs[2 * jj + 1] + wv * hi)
                    return tuple(out)

                accs = lax.fori_loop(0, SC_ROWS, row, (zero,) * (2 * cols), unroll=2)
                for jj in range(cols):
                    c = base + jj * SC_LANES
                    if half == 0:
                        out_v[tl, pl.ds(c, SC_LANES)] = accs[2 * jj]
                        out_v[tl, pl.ds(w + c, SC_LANES)] = accs[2 * jj + 1]
                    else:
                        out_v[tl, pl.ds(c, SC_LANES)] = out_v[tl, pl.ds(c, SC_LANES)] + accs[2 * jj]
                        out_v[tl, pl.ds(w + c, SC_LANES)] = out_v[tl, pl.ds(w + c, SC_LANES)] + accs[2 * jj + 1]

        def block(t0):
            s0 = t0 * STEPS_PER_TOKEN
            pltpu.sync_copy(idx_hbm.at[pl.ds(s0, n_steps)], idx_v)
            pltpu.sync_copy(wgt_hbm.at[pl.ds(t0, SC_TOK)], wgt_v)
            _sc_pipelined_steps(table_hbm, idx_v, rows_v, sems, s0, consume)
            pltpu.sync_copy(out_v, out_hbm.at[pl.ds(t0, SC_TOK)])

        _sc_token_blocks(n, block)

    return expert_sum(table, idx_steps, wgt)


def _gelu_exact(x):
    return 0.5 * x * (1.0 + lax.erf(x * (2.0 ** -0.5)))


def _peer_weight_kernel(act_ref, gate_ref, o_ref):
    o_ref[...] = gate_ref[...] * _gelu_exact(act_ref[...])


def _peer_weight(act, gate):
    n = act.shape[0]
    tt = min(2048, n)
    spec = pl.BlockSpec((tt, PEER_SEL), lambda i: (i, 0))
    return pl.pallas_call(
        _peer_weight_kernel,
        grid=(n // tt,),
        in_specs=[spec, spec],
        out_specs=spec,
        out_shape=jax.ShapeDtypeStruct((n, PEER_SEL), F32),
        compiler_params=_cparams("parallel"),
        name="peer_weight",
    )(act, gate)


def _ple_kernel(h_ref, peer_ref, p_ref, gp_ref, gfin_ref, wg_ref, wp_ref, o_ref):
    h = h_ref[...] + peer_ref[...]
    e = _rms(h, gp_ref[...]).astype(BF16)
    gate = _sigmoid(jnp.dot(e, wg_ref[...], preferred_element_type=F32))
    proj = jnp.dot(p_ref[...].astype(BF16), wp_ref[...], preferred_element_type=F32)
    o_ref[...] = _rms(h + gate * proj, gfin_ref[...])


def _ple_final(h, peer, p, g_ple, g_final, wg, wp):
    n, d = h.shape
    tm = min(512, n)
    return pl.pallas_call(
        _ple_kernel,
        grid=(n // tm,),
        in_specs=[pl.BlockSpec((tm, d), lambda i: (i, 0)),
                  pl.BlockSpec((tm, d), lambda i: (i, 0)),
                  pl.BlockSpec((tm, D_PLE), lambda i: (i, 0)),
                  pl.BlockSpec((1, d), lambda i: (0, 0)),
                  pl.BlockSpec((1, d), lambda i: (0, 0)),
                  pl.BlockSpec((d, d), lambda i: (0, 0)),
                  pl.BlockSpec((D_PLE, d), lambda i: (0, 0))],
        out_specs=pl.BlockSpec((tm, d), lambda i: (i, 0)),
        out_shape=jax.ShapeDtypeStruct((n, d), F32),
        compiler_params=_cparams("parallel"),
        name="ple_final",
    )(h, peer, p, g_ple, g_final, wg, wp)


def _prep_weights(norm_mix, w_in, mlstm_b_i, mlstm_b_f, fox_b_f, w_br_m, w_br_f, w_out, norm_ffn,
                  peer_w_q, peer_keys1, peer_keys2, peer_u, peer_v, norm_ple, w_ple_gate, w_ple_proj,
                  norm_final):
    o = [0]
    for s in (W_M, W_M, W_M, W_M, H_M, H_M, W_F, W_F, W_F, H_F, D_MODEL, D_MODEL):
        o.append(o[-1] + s)
    seg = lambda a, b: w_in[:, o[a]:o[b]]
    w_gate = jnp.concatenate([seg(4, 6), seg(9, 10)], axis=1)
    w_gate = jnp.pad(w_gate, ((0, 0), (0, GATE_COLS - w_gate.shape[1])))
    b_gate = jnp.concatenate([mlstm_b_i, mlstm_b_f, fox_b_f]).astype(F32)
    b_gate = jnp.pad(b_gate, (0, GATE_COLS - b_gate.shape[0]))[None, :]
    row = lambda v: v.astype(F32)[None, :]
    return dict(
        norm_mix=row(norm_mix),
        w_mqkv=seg(0, 3).astype(BF16),
        w_og=jnp.concatenate([seg(3, 4), seg(10, 12)], axis=1).astype(BF16),
        w_fq=seg(6, 7).astype(BF16), w_fk=seg(7, 8).astype(BF16), w_fv=seg(8, 9).astype(BF16),
        w_gate=w_gate.astype(F32), b_gate=b_gate,
        w_br_m=w_br_m.astype(BF16), w_br_f=w_br_f.astype(BF16), w_out=w_out.astype(BF16),
        norm_ffn=row(norm_ffn), wqt=peer_w_q.T.astype(F32),
        k1=peer_keys1.astype(F32), k2=peer_keys2.astype(F32),
        u_pk=_pack_table(peer_u), v_pk=_pack_table(peer_v),
        norm_ple=row(norm_ple), w_ple_gate=w_ple_gate.astype(BF16), w_ple_proj=w_ple_proj.astype(BF16),
        norm_final=row(norm_final),
    )


def _ct_rows(c):
    b, t, _ = c.shape
    return jnp.swapaxes(c[:, :, COL_FF:COL_FF + H_F], 1, 2).reshape(b, H_F // 2, 2, t)


def _peer_block(h, w):
    n = h.shape[0]
    xn, idx, gate = _peer_score(h, w["norm_ffn"], w["wqt"], w["k1"], w["k2"])
    steps = idx.reshape(n * STEPS_PER_TOKEN, SC_ROWS)
    act = _sc_expert_dot(w["u_pk"], steps, xn)
    return _sc_expert_sum(w["v_pk"], steps, _peer_weight(act, gate))


def _layer(x, p, w, mstate, fox_cache):
    b, t, d = x.shape
    n = b * t
    h = x.reshape(n, d)
    g = w["norm_mix"]
    qkv = _norm_matmul(h, g, w["w_mqkv"], BF16, "proj_mlstm_qkv")
    og = _norm_matmul(h, g, w["w_og"], F32, "proj_gates")
    fq = _norm_matmul(h, g, w["w_fq"], BF16, "proj_fox_q")
    fk, fk_heads = _norm_matmul_heads(h, g, w["w_fk"], "proj_fox_k")
    fv, fv_heads = _norm_matmul_heads(h, g, w["w_fv"], "proj_fox_v")
    gates = _gates(h, g, w["w_gate"], w["b_gate"])

    c0, n0, m0 = mstate
    ym, c_new, n_new, m_new = _mlstm(
        qkv.reshape(b, t, 3 * W_M), og.reshape(b, t, 3 * D_MODEL), gates.reshape(b, t, GATE_COLS),
        c0.astype(F32), n0.astype(F32)[:, :, None, :],
        jnp.broadcast_to(m0.astype(F32)[:, :, None, None], (b, H_M, 1, LANES)))

    gates3 = gates.reshape(b, t, GATE_COLS)
    if fox_cache is None:
        c = _cumsum_tokens(gates3)
        yf = _fox_prompt(fq.reshape(b, t, W_F), fk.reshape(b, t, W_F), fv.reshape(b, t, W_F), c)
    else:
        ck, cv, clf = fox_cache
        past = ck.shape[1]
        lf = jnp.pad(clf.astype(F32), ((0, 0), (0, 0), (COL_FF, GATE_COLS - COL_FF - H_F)))
        lf = jnp.concatenate([lf, gates3], axis=1)
        pad_t = (-lf.shape[1]) % 256
        c = _cumsum_tokens(jnp.pad(lf, ((0, 0), (0, pad_t), (0, 0))))
        ct = _ct_rows(c)
        yf = _fox_sample(fq.reshape(b, t, W_F), ck.reshape(b, past, W_F).astype(F32),
                         cv.reshape(b, past, W_F).astype(F32), fk.reshape(b, t, W_F), fv.reshape(b, t, W_F),
                         c[:, past:past + t, :], ct[..., :past], ct[..., past:past + t])

    h = _merge(h, ym.reshape(n, W_M), yf.reshape(n, W_F), og, w["w_br_m"], w["w_br_f"], w["w_out"])
    y = _ple_final(h, _peer_block(h, w), p.reshape(n, D_PLE), w["norm_ple"], w["norm_final"],
                   w["w_ple_gate"], w["w_ple_proj"])
    state = (fk_heads.reshape(1, b, t, H_F, DH_F), fv_heads.reshape(1, b, t, H_F, DH_F),
             gates3[None, :, :, COL_FF:COL_FF + H_F],
             c_new[None], n_new[None, :, :, 0, :], m_new[None, :, :, 0, 0])
    return y.reshape(b, t, d), state


def kernel(x_prompt, x_sample, p_prompt, p_sample, cache_fox_k, cache_fox_v, cache_fox_logf, state_mlstm_C, state_mlstm_n, state_mlstm_m, norm_mix, w_in, mlstm_b_i, mlstm_b_f, fox_b_f, w_br_m, w_br_f, w_out, norm_ffn, peer_w_q, peer_keys1, peer_keys2, peer_u, peer_v, norm_ple, w_ple_gate, w_ple_proj, norm_final):
    assert w_in.shape[0] == 1, "single-layer trunk"
    w = _prep_weights(norm_mix[0], w_in[0], mlstm_b_i[0], mlstm_b_f[0], fox_b_f[0], w_br_m[0], w_br_f[0],
                      w_out[0], norm_ffn[0], peer_w_q[0], peer_keys1[0], peer_keys2[0], peer_u[0], peer_v[0],
                      norm_ple[0], w_ple_gate[0], w_ple_proj[0], norm_final)
    bp = x_prompt.shape[0]
    zeros = (jnp.zeros((bp, H_M, DH_M, DH_M), F32), jnp.zeros((bp, H_M, DH_M), F32), jnp.zeros((bp, H_M), F32))
    y_p, sp = _layer(x_prompt, p_prompt[0], w, zeros, None)
    y_s, ss = _layer(x_sample, p_sample[0], w,
                     (state_mlstm_C[0], state_mlstm_n[0], state_mlstm_m[0]),
                     (cache_fox_k[0], cache_fox_v[0], cache_fox_logf[0]))
    return (y_p, y_s) + sp + ss
```

```python
import functools

import jax
import jax.numpy as jnp
from jax import lax
from jax.experimental import pallas as pl
from jax.experimental.pallas import tpu as pltpu
from jax.experimental.pallas import tpu_sc as plsc

D_MODEL = 1024
CHUNK = 64
H_M = 4
DH_M = 256
W_M = H_M * DH_M
H_F = 16
DH_F = 64
W_F = H_F * DH_F
D_PLE = 256
PEER_HEADS = 8
PEER_KEYS = 128
PEER_DQ = 256
PEER_TOPK = 16
PEER_SEL = PEER_HEADS * PEER_TOPK
EPS = 1e-6

LANES = 128
GATE_COLS = LANES
COL_I, COL_F, COL_FF = 0, H_M, 2 * H_M
VMEM_LIMIT = 56 * 1024 * 1024
HIGHEST = lax.Precision.HIGHEST
F32 = jnp.float32
BF16 = jnp.bfloat16
NEG_INF = float("-inf")


def _cparams(*sem):
    return pltpu.CompilerParams(dimension_semantics=sem, vmem_limit_bytes=VMEM_LIMIT)


def _rms(x, g):
    return x * lax.rsqrt(jnp.mean(x * x, axis=-1, keepdims=True) + EPS) * g


def _sigmoid(x):
    return 1.0 / (1.0 + jnp.exp(-x))


def _norm_matmul_kernel(x_ref, g_ref, w_ref, o_ref, a_ref):
    @pl.when(pl.program_id(1) == 0)
    def _():
        a_ref[...] = _rms(x_ref[...], g_ref[...]).astype(a_ref.dtype)

    o_ref[...] = jnp.dot(a_ref[...], w_ref[...], preferred_element_type=F32).astype(o_ref.dtype)


def _norm_matmul(x, g, w, out_dtype, name):
    n, d = x.shape
    cols = w.shape[1]
    tm = min(1024, n)
    tn = 1024
    return pl.pallas_call(
        _norm_matmul_kernel,
        grid=(n // tm, cols // tn),
        in_specs=[pl.BlockSpec((tm, d), lambda i, j: (i, 0)),
                  pl.BlockSpec((1, d), lambda i, j: (0, 0)),
                  pl.BlockSpec((d, tn), lambda i, j: (0, j))],
        out_specs=pl.BlockSpec((tm, tn), lambda i, j: (i, j)),
        out_shape=jax.ShapeDtypeStruct((n, cols), out_dtype),
        scratch_shapes=[pltpu.VMEM((tm, d), BF16)],
        compiler_params=_cparams("parallel", "arbitrary"),
        name=name,
    )(x, g, w)


def _norm_matmul_heads_kernel(x_ref, g_ref, w_ref, o_ref, oh_ref):
    a = _rms(x_ref[...], g_ref[...]).astype(BF16)
    z = jnp.dot(a, w_ref[...], preferred_element_type=F32)
    o_ref[...] = z.astype(o_ref.dtype)
    for h in range(H_F):
        oh_ref[:, h, :] = z[:, h * DH_F:(h + 1) * DH_F]


def _norm_matmul_heads(x, g, w, name):
    n, d = x.shape
    tm = min(512, n)
    return pl.pallas_call(
        _norm_matmul_heads_kernel,
        grid=(n // tm,),
        in_specs=[pl.BlockSpec((tm, d), lambda i: (i, 0)),
                  pl.BlockSpec((1, d), lambda i: (0, 0)),
                  pl.BlockSpec((d, W_F), lambda i: (0, 0))],
        out_specs=[pl.BlockSpec((tm, W_F), lambda i: (i, 0)),
                   pl.BlockSpec((tm, H_F, DH_F), lambda i: (i, 0, 0))],
        out_shape=[jax.ShapeDtypeStruct((n, W_F), BF16),
                   jax.ShapeDtypeStruct((n, H_F, DH_F), F32)],
        compiler_params=_cparams("parallel"),
        name=name,
    )(x, g, w)


def _gate_kernel(x_ref, g_ref, w_ref, b_ref, o_ref):
    a = _rms(x_ref[...], g_ref[...])
    z = jnp.dot(a, w_ref[...], precision=HIGHEST, preferred_element_type=F32) + b_ref[...]
    col = lax.broadcasted_iota(jnp.int32, z.shape, 1)
    log_sig = jnp.minimum(z, 0.0) - jnp.log1p(jnp.exp(-jnp.abs(z)))
    o_ref[...] = jnp.where(col < COL_F, z, log_sig)


def _gates(x, g, w, b):
    n, d = x.shape
    tm = min(512, n)
    return pl.pallas_call(
        _gate_kernel,
        grid=(n // tm,),
        in_specs=[pl.BlockSpec((tm, d), lambda i: (i, 0)),
                  pl.BlockSpec((1, d), lambda i: (0, 0)),
                  pl.BlockSpec((d, GATE_COLS), lambda i: (0, 0)),
                  pl.BlockSpec((1, GATE_COLS), lambda i: (0, 0))],
        out_specs=pl.BlockSpec((tm, GATE_COLS), lambda i: (i, 0)),
        out_shape=jax.ShapeDtypeStruct((n, GATE_COLS), F32),
        compiler_params=_cparams("parallel"),
        name="gates",
    )(x, g, w, b)


def _cumsum_kernel(x_ref, o_ref, carry_ref):
    @pl.when(pl.program_id(1) == 0)
    def _():
        carry_ref[...] = jnp.zeros_like(carry_ref)

    x = x_ref[...]
    tb = x.shape[0]
    row = lax.broadcasted_iota(jnp.int32, (tb, tb), 0)
    col = lax.broadcasted_iota(jnp.int32, (tb, tb), 1)
    tril = jnp.where(col <= row, 1.0, 0.0).astype(F32)
    c = jnp.dot(tril, x, precision=HIGHEST, preferred_element_type=F32) + carry_ref[...]
    o_ref[...] = c
    carry_ref[...] = c[tb - 1:tb, :]


def _cumsum_tokens(x):
    b, t, w = x.shape
    tb = 256
    return pl.pallas_call(
        _cumsum_kernel,
        grid=(b, t // tb),
        in_specs=[pl.BlockSpec((None, tb, w), lambda i, j: (i, j, 0))],
        out_specs=pl.BlockSpec((None, tb, w), lambda i, j: (i, j, 0)),
        out_shape=jax.ShapeDtypeStruct((b, t, w), F32),
        scratch_shapes=[pltpu.VMEM((1, w), F32)],
        compiler_params=_cparams("parallel", "arbitrary"),
        name="cumsum",
    )(x)


def _mlstm_kernel(qkv_ref, og_ref, g_ref, c0_ref, n0_ref, m0_ref,
                  y_ref, cn_ref, nn_ref, mn_ref, c_s, n_s, m_s, *, bb_n, blk):
    step = pl.program_id(1)

    @pl.when(step == 0)
    def _():
        c_s[...] = c0_ref[...]
        n_s[...] = n0_ref[...]
        m_s[...] = m0_ref[...]

    row = lax.broadcasted_iota(jnp.int32, (blk, blk), 0)
    col = lax.broadcasted_iota(jnp.int32, (blk, blk), 1)
    tril = col <= row
    triu = row <= col
    eye = col == row

    def to_row(x_col):
        return jnp.sum(jnp.where(eye, x_col, 0.0), axis=0, keepdims=True)

    for bb in range(bb_n):
        g = g_ref[bb]
        for h in range(H_M):
            q = qkv_ref[bb, :, h * DH_M:(h + 1) * DH_M]
            k = qkv_ref[bb, :, (H_M + h) * DH_M:(H_M + h + 1) * DH_M] * (DH_M ** -0.5)
            v = qkv_ref[bb, :, (2 * H_M + h) * DH_M:(2 * H_M + h + 1) * DH_M]
            i_col = g[:, COL_I + h:COL_I + h + 1]
            f_col = g[:, COL_F + h:COL_F + h + 1]
            i_row = to_row(i_col)
            f_row = to_row(f_col)
            b_col = jnp.sum(jnp.where(tril, f_row, 0.0), axis=1, keepdims=True)
            b_row = jnp.sum(jnp.where(triu, f_col, 0.0), axis=0, keepdims=True)
            m_prev = m_s[bb, h][:, 0:1]
            dmat = jnp.where(tril, b_col - b_row + i_row, NEG_INF)
            g_col = b_col + m_prev
            mt = jnp.maximum(g_col, jnp.max(dmat, axis=1, keepdims=True))
            w_d = jnp.exp(dmat - mt)
            w_g = jnp.exp(g_col - mt)
            qk = lax.dot_general(q, k, (((1,), (1,)), ((), ())), preferred_element_type=F32) * w_d
            c_prev = c_s[bb, h]
            n_prev = n_s[bb, h]
            num = (w_g * jnp.dot(q, c_prev.astype(BF16), preferred_element_type=F32)
                   + jnp.dot(qk.astype(BF16), v, preferred_element_type=F32))
            den = (w_g * jnp.sum(q.astype(F32) * n_prev, axis=1, keepdims=True)
                   + jnp.sum(qk, axis=1, keepdims=True))
            hid = num / jnp.maximum(jnp.abs(den), jnp.exp(-mt))
            o_gate = _sigmoid(og_ref[bb, :, h * DH_M:(h + 1) * DH_M])
            y_ref[bb, :, h * DH_M:(h + 1) * DH_M] = (o_gate * hid).astype(y_ref.dtype)
            m_new = mt[blk - 1:blk, :]
            b_last = b_col[blk - 1:blk, :]
            w_c = jnp.exp(b_last + m_prev - m_new)
            w_s = jnp.exp(b_last - b_col + i_col - m_new)
            kw = k.astype(F32) * w_s
            c_s[bb, h] = w_c * c_prev + lax.dot_general(
                kw.astype(BF16), v, (((0,), (0,)), ((), ())), preferred_element_type=F32)
            n_s[bb, h] = w_c * n_prev + jnp.sum(kw, axis=0, keepdims=True)
            m_s[bb, h] = jnp.broadcast_to(m_new, (1, LANES))

    @pl.when(step == pl.num_programs(1) - 1)
    def _():
        cn_ref[...] = c_s[...]
        nn_ref[...] = n_s[...]
        mn_ref[...] = m_s[...]


def _mlstm(qkv, ogate, gates, c0, n0, m0):
    b, t, _ = qkv.shape
    blk = min(CHUNK, t)
    bb_n = 4
    state_spec = lambda shape: pl.BlockSpec((bb_n,) + shape, lambda i, j: (i, 0, 0, 0))
    return pl.pallas_call(
        functools.partial(_mlstm_kernel, bb_n=bb_n, blk=blk),
        grid=(b // bb_n, t // blk),
        in_specs=[pl.BlockSpec((bb_n, blk, 3 * W_M), lambda i, j: (i, j, 0)),
                  pl.BlockSpec((bb_n, blk, W_M), lambda i, j: (i, j, 0)),
                  pl.BlockSpec((bb_n, blk, GATE_COLS), lambda i, j: (i, j, 0)),
                  state_spec((H_M, DH_M, DH_M)),
                  state_spec((H_M, 1, DH_M)),
                  state_spec((H_M, 1, LANES))],
        out_specs=[pl.BlockSpec((bb_n, blk, W_M), lambda i, j: (i, j, 0)),
                   state_spec((H_M, DH_M, DH_M)),
                   state_spec((H_M, 1, DH_M)),
                   state_spec((H_M, 1, LANES))],
        out_shape=[jax.ShapeDtypeStruct((b, t, W_M), BF16),
                   jax.ShapeDtypeStruct((b, H_M, DH_M, DH_M), F32),
                   jax.ShapeDtypeStruct((b, H_M, 1, DH_M), F32),
                   jax.ShapeDtypeStruct((b, H_M, 1, LANES), F32)],
        scratch_shapes=[pltpu.VMEM((bb_n, H_M, DH_M, DH_M), F32),
                        pltpu.VMEM((bb_n, H_M, 1, DH_M), F32),
                        pltpu.VMEM((bb_n, H_M, 1, LANES), F32)],
        compiler_params=_cparams("parallel", "arbitrary"),
        name="mlstm",
    )(qkv, ogate, gates, c0, n0, m0)


def _online_softmax_step(s, v_blk, m_ref, l_ref, acc_ref):
    m_old = m_ref[...]
    m_new = jnp.maximum(m_old, jnp.max(s, axis=1, keepdims=True))
    alpha = jnp.exp(m_old - m_new)
    p = jnp.exp(s - m_new)
    l_ref[...] = alpha * l_ref[...] + jnp.sum(p, axis=1, keepdims=True)
    acc_ref[...] = alpha * acc_ref[...] + jnp.dot(p.astype(BF16), v_blk, preferred_element_type=F32)
    m_ref[...] = m_new


def _split3(x):
    hi = x.astype(BF16).astype(F32)
    mid = (x - hi).astype(BF16).astype(F32)
    lo = (x - hi - mid).astype(BF16).astype(F32)
    return hi, mid, lo


FOX_V_ROWS = DH_F + 16
AUG_CQ = 6


def _fox_prompt_kernel(q_ref, k_ref, v_ref, c_ref, o_ref, kaug_s, vt_s, m0_s, m1_s, acc0_s, acc1_s,
                       *, tq, n_blocks):
    hp = pl.program_id(1)
    qi = pl.program_id(2)
    lane = lax.broadcasted_iota(jnp.int32, (tq, LANES), 1)
    m_s, acc_s = (m0_s, m1_s), (acc0_s, acc1_s)

    def head_col(c_blk, x):
        return jnp.sum(jnp.where(lane == COL_FF + 2 * hp + x, c_blk, 0.0), axis=1, keepdims=True)

    def aug_tile(entries):
        tile = jnp.zeros((tq, LANES), F32)
        for l, val in entries:
            tile = jnp.where(lane == l, val, tile)
        return tile.astype(BF16)

    @pl.when(qi == 0)
    def _():
        ones_rows = jnp.where(lax.broadcasted_iota(jnp.int32, (FOX_V_ROWS - DH_F, tq), 0) == 0, 1.0, 0.0)

        def chunk(i, carry):
            rs = pl.multiple_of(i * tq, tq)
            c_blk = c_ref[pl.ds(rs, tq), :]
            pieces = _split3(head_col(c_blk, 0)) + _split3(head_col(c_blk, 1))
            entries = list(enumerate(pieces)) + [(AUG_CQ + j, 1.0) for j in range(3)]
            kaug_s[pl.ds(rs, tq), 0:LANES] = k_ref[pl.ds(rs, tq), :]
            kaug_s[pl.ds(rs, tq), LANES:2 * LANES] = aug_tile(entries)
            vt = v_ref[pl.ds(rs, tq), :].astype(F32).T
            for x in range(2):
                vt_s[x, 0:DH_F, pl.ds(rs, tq)] = vt[x * DH_F:(x + 1) * DH_F, :].astype(BF16)
                vt_s[x, DH_F:FOX_V_ROWS, pl.ds(rs, tq)] = ones_rows.astype(BF16)
            return carry

        lax.fori_loop(0, n_blocks, chunk, 0)

    qs = pl.multiple_of(qi * tq, tq)
    c_q = c_ref[pl.ds(qs, tq), :]
    q2 = q_ref[...] * (DH_F ** -0.5)
    q_aug = []
    for x in range(2):
        cq3 = _split3(head_col(c_q, x))
        entries = [(3 * x + j, -1.0) for j in range(3)] + [(AUG_CQ + j, cq3[j]) for j in range(3)]
        q_head = jnp.where((lane < DH_F) == (x == 0), q2, jnp.zeros_like(q2))
        q_aug.append(jnp.concatenate([q_head, aug_tile(entries)], axis=1))
        m_s[x][...] = jnp.full(m_s[x].shape, NEG_INF, F32)
        acc_s[x][...] = jnp.zeros(acc_s[x].shape, F32)

    def block(kb, masked):
        ks = pl.multiple_of(kb * tq, tq)
        k_blk = kaug_s[pl.ds(ks, tq), :]
        scores = [lax.dot_general(k_blk, q_aug[x], (((1,), (1,)), ((), ())), preferred_element_type=F32)
                  for x in range(2)]
        for x in range(2):
            st = scores[x]
            if masked:
                k_pos = lax.broadcasted_iota(jnp.int32, (tq, tq), 0)
                q_pos = lax.broadcasted_iota(jnp.int32, (tq, tq), 1)
                st = jnp.where(k_pos <= q_pos, st, NEG_INF)
            m_old = m_s[x][...]
            m_new = jnp.maximum(m_old, jnp.max(st, axis=0, keepdims=True))
            p = jnp.exp(st - m_new).astype(BF16)
            acc_s[x][...] = (jnp.exp(m_old - m_new) * acc_s[x][...]
                             + jnp.dot(vt_s[x, :, pl.ds(ks, tq)], p, preferred_element_type=F32))
            m_s[x][...] = m_new

    def body(kb, carry):
        block(kb, False)
        return carry

    lax.fori_loop(0, qi, body, 0)
    block(qi, True)
    out_t = jnp.concatenate([acc_s[x][0:DH_F, :] / acc_s[x][DH_F:DH_F + 1, :] for x in range(2)], axis=0)
    o_ref[...] = out_t.T.astype(o_ref.dtype)


def _fox_prompt(q, k, v, c, batch):
    _, t, _ = q.shape
    tq = 512
    head_pair = lambda i, h, j: (batch, 0, h)
    return pl.pallas_call(
        functools.partial(_fox_prompt_kernel, tq=tq, n_blocks=t // tq),
        grid=(1, H_F // 2, t // tq),
        in_specs=[pl.BlockSpec((None, tq, LANES), lambda i, h, j: (batch, j, h)),
                  pl.BlockSpec((None, t, LANES), head_pair),
                  pl.BlockSpec((None, t, LANES), head_pair),
                  pl.BlockSpec((None, t, LANES), lambda i, h, j: (batch, 0, 0))],
        out_specs=pl.BlockSpec((tq, LANES), lambda i, h, j: (j, h)),
        out_shape=jax.ShapeDtypeStruct((t, W_F), BF16),
        scratch_shapes=[pltpu.VMEM((t, 2 * LANES), BF16),
                        pltpu.VMEM((2, FOX_V_ROWS, t), BF16),
                        pltpu.VMEM((1, tq), F32), pltpu.VMEM((1, tq), F32),
                        pltpu.VMEM((FOX_V_ROWS, tq), F32), pltpu.VMEM((FOX_V_ROWS, tq), F32)],
        compiler_params=_cparams("parallel", "parallel", "arbitrary"),
        name="fox_prompt",
    )(q, k, v, c)


def _fox_sample_kernel(q_ref, kc_ref, vc_ref, kn_ref, vn_ref, cq_ref, ctc_ref, ctn_ref, o_ref,
                       m_s, l_s, acc_s, *, tn):
    kb = pl.program_id(1)

    @pl.when(kb == 0)
    def _():
        m_s[...] = jnp.full_like(m_s, NEG_INF)
        l_s[...] = jnp.zeros_like(l_s)
        acc_s[...] = jnp.zeros_like(acc_s)

    lane = lax.broadcasted_iota(jnp.int32, (tn, LANES), 1)
    c_blk = cq_ref[...]

    def heads(hp):
        q2 = q_ref[:, hp * LANES:(hp + 1) * LANES] * (DH_F ** -0.5)
        zero = jnp.zeros_like(q2)
        for x in range(2):
            qx = jnp.where((lane < DH_F) == (x == 0), q2, zero)
            cqx = jnp.sum(jnp.where(lane == COL_FF + 2 * hp + x, c_blk, 0.0), axis=1, keepdims=True)
            yield 2 * hp + x, qx, cqx

    def dense_pair(ref, hp):
        return ref[:, hp * LANES:(hp + 1) * LANES].astype(BF16)

    def cached_pair(ref, hp):
        return jnp.concatenate([ref[:, 2 * hp, :], ref[:, 2 * hp + 1, :]], axis=1).astype(BF16)

    def attend(k_ref, v_ref, ct_ref, pair, masked):
        for hp in range(H_F // 2):
            k_blk = pair(k_ref, hp)
            v_blk = pair(v_ref, hp)
            for hd, qx, cqx in heads(hp):
                s = lax.dot_general(qx, k_blk, (((1,), (1,)), ((), ())), preferred_element_type=F32)
                s = s + cqx - ct_ref[hp, hd % 2:hd % 2 + 1, :]
                if masked:
                    q_pos = lax.broadcasted_iota(jnp.int32, s.shape, 0)
                    k_pos = lax.broadcasted_iota(jnp.int32, s.shape, 1)
                    s = jnp.where(k_pos <= q_pos, s, NEG_INF)
                _online_softmax_step(s, v_blk, m_s.at[hd], l_s.at[hd], acc_s.at[hd])

    attend(kc_ref, vc_ref, ctc_ref, cached_pair, False)

    @pl.when(kb == pl.num_programs(1) - 1)
    def _():
        attend(kn_ref, vn_ref, ctn_ref, dense_pair, True)
        for hp in range(H_F // 2):
            out = [acc_s[2 * hp + x] / l_s[2 * hp + x] for x in range(2)]
            o_ref[:, hp * LANES:(hp + 1) * LANES] = jnp.where(lane < DH_F, out[0], out[1]).astype(o_ref.dtype)


def _fox_sample(q, k_cache, v_cache, k_new, v_new, cq, ct_cache, ct_new):
    b, tn, _ = q.shape
    p = k_cache.shape[1]
    tk = 1024
    return pl.pallas_call(
        functools.partial(_fox_sample_kernel, tn=tn),
        grid=(b, p // tk),
        in_specs=[pl.BlockSpec((None, tn, W_F), lambda i, j: (i, 0, 0)),
                  pl.BlockSpec((None, tk, H_F, DH_F), lambda i, j: (i, j, 0, 0)),
                  pl.BlockSpec((None, tk, H_F, DH_F), lambda i, j: (i, j, 0, 0)),
                  pl.BlockSpec((None, tn, W_F), lambda i, j: (i, 0, 0)),
                  pl.BlockSpec((None, tn, W_F), lambda i, j: (i, 0, 0)),
                  pl.BlockSpec((None, tn, LANES), lambda i, j: (i, 0, 0)),
                  pl.BlockSpec((None, H_F // 2, 2, tk), lambda i, j: (i, 0, 0, j)),
                  pl.BlockSpec((None, H_F // 2, 2, tn), lambda i, j: (i, 0, 0, 0))],
        out_specs=pl.BlockSpec((None, tn, W_F), lambda i, j: (i, 0, 0)),
        out_shape=jax.ShapeDtypeStruct((b, tn, W_F), BF16),
        scratch_shapes=[pltpu.VMEM((H_F, tn, 1), F32),
                        pltpu.VMEM((H_F, tn, 1), F32),
                        pltpu.VMEM((H_F, tn, LANES), F32)],
        compiler_params=_cparams("parallel", "arbitrary"),
        name="fox_sample",
    )(q, k_cache, v_cache, k_new, v_new, cq, ct_cache, ct_new)


def _merge_kernel(h_ref, ym_ref, yf_ref, gm_ref, gf_ref, wm_ref, wf_ref, wo_ref, o_ref):
    a = jnp.dot(ym_ref[...], wm_ref[...], preferred_element_type=F32)
    b = jnp.dot(yf_ref[...], wf_ref[...], preferred_element_type=F32)
    merge = _sigmoid(gm_ref[...]) * a + _sigmoid(gf_ref[...]) * b
    o_ref[...] = h_ref[...] + jnp.dot(merge.astype(BF16), wo_ref[...], preferred_element_type=F32)


def _merge(h, ym, yf, gates, wm, wf, wo, first_token):
    n, d = yf.shape
    tm = min(512, n)
    first = first_token // tm
    tok = lambda c: pl.BlockSpec((tm, d), lambda i: (first + i, c))
    local = pl.BlockSpec((tm, d), lambda i: (i, 0))
    wspec = pl.BlockSpec((d, d), lambda i: (0, 0))
    return pl.pallas_call(
        _merge_kernel,
        grid=(n // tm,),
        in_specs=[tok(0), tok(0), local, tok(1), tok(2), wspec, wspec, wspec],
        out_specs=local,
        out_shape=jax.ShapeDtypeStruct((n, d), F32),
        compiler_params=_cparams("parallel"),
        name="merge",
    )(h, ym, yf, gates, gates, wm, wf, wo)


def _top_rows(s, count, ids=None, payload=None):
    if ids is None:
        ids = lax.broadcasted_iota(jnp.int32, s.shape, 0)
    big = jnp.int32(2 ** 30)
    vals, sel, pay = [], [], []
    for _ in range(count):
        m = jnp.max(s, axis=0, keepdims=True)
        am = jnp.min(jnp.where(s == m, ids, big), axis=0, keepdims=True)
        hit = ids == am
        vals.append(m)
        sel.append(am)
        if payload is not None:
            pay.append(jnp.max(jnp.where(hit, payload, -1), axis=0, keepdims=True))
        s = jnp.where(hit, NEG_INF, s)
    cat = lambda xs: jnp.concatenate(xs, axis=0)
    return cat(vals), cat(sel), (cat(pay) if payload is not None else None)


def _pair_candidates(v1, i1, v2, i2):
    t = v1.shape[1]
    half = PEER_TOPK // 2
    r16 = lax.broadcasted_iota(jnp.int32, (PEER_TOPK, t), 0)
    r8 = lax.broadcasted_iota(jnp.int32, (half, t), 0)
    sums = [v1[0:1, :] + v2]
    flat = [r16]
    expert = [i1[0:1, :] * PEER_KEYS + i2]
    for a in range(1, half):
        sums.append(v1[a:a + 1, :] + v2[0:half, :])
        flat.append(r8 + a * PEER_TOPK)
        expert.append(i1[a:a + 1, :] * PEER_KEYS + i2[0:half, :])
    sums.append(v1[half:, :] + v2[0:1, :])
    flat.append((r8 + half) * PEER_TOPK)
    expert.append(i1[half:, :] * PEER_KEYS + i2[0:1, :])
    cat = lambda xs: jnp.concatenate(xs, axis=0)
    return cat(sums), cat(flat), cat(expert)


def _peer_score_kernel(h_ref, g_ref, wqt_ref, k1_ref, k2_ref, xn_ref, idx_ref, gate_ref):
    xn = _rms(h_ref[...], g_ref[...])
    xn_ref[...] = xn.astype(xn_ref.dtype)
    qt = lax.dot_general(wqt_ref[...], xn, (((1,), (1,)), ((), ())),
                         precision=HIGHEST, preferred_element_type=F32)
    half = PEER_DQ // 2
    idx_rows, gate_rows = [], []
    for hd in range(PEER_HEADS):
        q1 = qt[hd * PEER_DQ:hd * PEER_DQ + half, :]
        q2 = qt[hd * PEER_DQ + half:(hd + 1) * PEER_DQ, :]
        s1 = jnp.dot(k1_ref[hd], q1, precision=HIGHEST, preferred_element_type=F32)
        s2 = jnp.dot(k2_ref[hd], q2, precision=HIGHEST, preferred_element_type=F32)
        v1, i1, _ = _top_rows(s1, PEER_TOPK)
        v2, i2, _ = _top_rows(s2, PEER_TOPK)
        cand, flat, expert = _pair_candidates(v1, i1, v2, i2)
        sc, _, ex = _top_rows(cand, PEER_TOPK, ids=flat, payload=expert)
        e = jnp.exp(sc - sc[0:1, :])
        gate_rows.append(e / jnp.sum(e, axis=0, keepdims=True))
        idx_rows.append(ex)
    gate_ref[...] = jnp.concatenate(gate_rows, axis=0).T
    idx_ref[...] = jnp.concatenate(idx_rows, axis=0).T


def _peer_score(h, g, wqt, k1, k2):
    n, d = h.shape
    tt = min(256, n)
    return pl.pallas_call(
        _peer_score_kernel,
        grid=(n // tt,),
        in_specs=[pl.BlockSpec((tt, d), lambda i: (i, 0)),
                  pl.BlockSpec((1, d), lambda i: (0, 0)),
                  pl.BlockSpec(wqt.shape, lambda i: (0, 0)),
                  pl.BlockSpec(k1.shape, lambda i: (0, 0, 0)),
                  pl.BlockSpec(k2.shape, lambda i: (0, 0, 0))],
        out_specs=[pl.BlockSpec((tt, d), lambda i: (i, 0)),
                   pl.BlockSpec((tt, PEER_SEL), lambda i: (i, 0)),
                   pl.BlockSpec((tt, PEER_SEL), lambda i: (i, 0))],
        out_shape=[jax.ShapeDtypeStruct((n, d), F32),
                   jax.ShapeDtypeStruct((n, PEER_SEL), jnp.int32),
                   jax.ShapeDtypeStruct((n, PEER_SEL), F32)],
        compiler_params=_cparams("parallel"),
        name="peer_score",
    )(h, g, wqt, k1, k2)


SC_CORES = 2
SC_SUBCORES = 16
SC_WORKERS = SC_CORES * SC_SUBCORES
SC_ROWS = 64


def _pack_table(tab):
    half = tab.shape[1] // 2
    bits = lax.bitcast_convert_type(tab.astype(BF16), jnp.uint16).astype(jnp.uint32)
    word = bits[:, :half] | (bits[:, half:] << 16)
    return lax.bitcast_convert_type(word, jnp.int32)


SC_TOK = 8
SC_LANES = 16
STEPS_PER_TOKEN = PEER_SEL // SC_ROWS
SC_PARAMS = pltpu.CompilerParams(needs_layout_passes=False)


def _row_source(table_hbm, idx_v, local_step, global_step):
    del global_step
    return table_hbm.at[idx_v.at[local_step]]


def _sc_unpack(wd):
    lo = lax.bitcast_convert_type(wd << 16, F32)
    hi = lax.bitcast_convert_type(wd & jnp.int32(-65536), F32)
    return lo, hi


def _sc_token_blocks(n, body_block):
    per_worker = n // SC_WORKERS
    assert per_worker * SC_WORKERS == n and per_worker % SC_TOK == 0
    wid = lax.axis_index("s") * SC_CORES + lax.axis_index("c")

    @pl.loop(0, per_worker // SC_TOK)
    def _(blk):
        body_block(wid * per_worker + blk * SC_TOK)


def _sc_pipelined_steps(table_hbm, idx_v, rows_v, sems, first_step, consume):
    n_steps = SC_TOK * STEPS_PER_TOKEN

    def row_gather(j, slot):
        return pltpu.make_async_copy(_row_source(table_hbm, idx_v, j, first_step + j),
                                     rows_v.at[slot], sems.at[slot])

    row_gather(0, 0).start()

    @pl.loop(0, n_steps, step=2)
    def _(i):
        for slot in range(2):
            j = i + slot

            @pl.when(j + 1 < n_steps)
            def _():
                row_gather(j + 1, 1 - slot).start()

            row_gather(j, slot).wait()
            consume(slot, i // 2, slot)


def _sc_expert_dot(table, idx_steps, x):
    n, d = x.shape
    w = d // 2
    assert STEPS_PER_TOKEN == 2 and table.shape[1] == w
    mesh = plsc.VectorSubcoreMesh(core_axis_name="c", subcore_axis_name="s")
    n_steps = SC_TOK * STEPS_PER_TOKEN
    group = 4

    @functools.partial(
        pl.kernel, mesh=mesh,
        out_type=jax.ShapeDtypeStruct((n, PEER_SEL), F32),
        scratch_types=[pltpu.VMEM((n_steps, SC_ROWS), jnp.int32),
                       pltpu.VMEM((SC_TOK, d), F32),
                       pltpu.VMEM((2, SC_ROWS, w), jnp.int32),
                       pltpu.VMEM((SC_TOK, PEER_SEL), F32),
                       pltpu.SemaphoreType.DMA((2,))],
        compiler_params=SC_PARAMS,
    )
    def expert_dot(table_hbm, idx_hbm, x_hbm, act_hbm, idx_v, x_v, rows_v, act_v, sems):
        lanes = lax.iota(jnp.int32, SC_LANES)
        zero = jnp.zeros((SC_LANES,), F32)

        def consume(slot, tl, half):
            @pl.loop(0, SC_ROWS // SC_LANES)
            def _(g):
                act_vec = zero
                for q in range(SC_LANES // group):
                    r0 = g * SC_LANES + q * group

                    def chunk(j, accs):
                        c = pl.multiple_of(j * SC_LANES, SC_LANES)
                        x_lo = x_v[tl, pl.ds(c, SC_LANES)]
                        x_hi = x_v[tl, pl.ds(w + c, SC_LANES)]
                        out = []
                        for rr in range(group):
                            lo, hi = _sc_unpack(rows_v[slot, r0 + rr, pl.ds(c, SC_LANES)])
                            out.append(accs[rr] + lo * x_lo + hi * x_hi)
                        return tuple(out)

                    accs = lax.fori_loop(0, w // SC_LANES, chunk, (zero,) * group, unroll=2)
                    for rr in range(group):
                        act_vec = jnp.where(lanes == q * group + rr, jnp.sum(accs[rr]), act_vec)
                act_v[tl, pl.ds(half * SC_ROWS + g * SC_LANES, SC_LANES)] = act_vec

        def block(t0):
            s0 = t0 * STEPS_PER_TOKEN
            pltpu.sync_copy(idx_hbm.at[pl.ds(s0, n_steps)], idx_v)
            pltpu.sync_copy(x_hbm.at[pl.ds(t0, SC_TOK)], x_v)
            _sc_pipelined_steps(table_hbm, idx_v, rows_v, sems, s0, consume)
            pltpu.sync_copy(act_v, act_hbm.at[pl.ds(t0, SC_TOK)])

        _sc_token_blocks(n, block)

    return expert_dot(table, idx_steps, x)


def _sc_expert_sum(table, idx_steps, wgt):
    n = wgt.shape[0]
    w = table.shape[1]
    d = 2 * w
    assert STEPS_PER_TOKEN == 2
    mesh = plsc.VectorSubcoreMesh(core_axis_name="c", subcore_axis_name="s")
    n_steps = SC_TOK * STEPS_PER_TOKEN
    cols = 8

    @functools.partial(
        pl.kernel, mesh=mesh,
        out_type=jax.ShapeDtypeStruct((n, d), F32),
        scratch_types=[pltpu.VMEM((n_steps, SC_ROWS), jnp.int32),
                       pltpu.VMEM((SC_TOK, PEER_SEL), F32),
                       pltpu.VMEM((2, SC_ROWS, w), jnp.int32),
                       pltpu.VMEM((SC_TOK, d), F32),
                       pltpu.VMEM((SC_ROWS, SC_LANES), F32),
                       pltpu.SemaphoreType.DMA((2,))],
        compiler_params=SC_PARAMS,
    )
    def expert_sum(table_hbm, idx_hbm, wgt_hbm, out_hbm, idx_v, wgt_v, rows_v, out_v, splat_v, sems):
        zero = jnp.zeros((SC_LANES,), F32)
        lanes = lax.iota(jnp.int32, SC_LANES)

        def consume(slot, tl, half):
            @pl.loop(0, SC_ROWS // SC_LANES)
            def _(g):
                w16 = wgt_v[tl, pl.ds(half * SC_ROWS + g * SC_LANES, SC_LANES)]
                for rr in range(SC_LANES):
                    one = jnp.sum(jnp.where(lanes == rr, w16, 0.0))
                    splat_v[g * SC_LANES + rr, :] = jnp.full((SC_LANES,), one, F32)

            for cb in range(w // (cols * SC_LANES)):
                base = cb * cols * SC_LANES

                def row(r, accs):
                    wv = splat_v[r, :]
                    out = []
                    for jj in range(cols):
                        lo, hi = _sc_unpack(rows_v[slot, r, pl.ds(base + jj * SC_LANES, SC_LANES)])
                        out.append(accs[2 * jj] + wv * lo)
                        out.append(accs[2 * jj + 1] + wv * hi)
                    return tuple(out)

                accs = lax.fori_loop(0, SC_ROWS, row, (zero,) * (2 * cols), unroll=2)
                for jj in range(cols):
                    c = base + jj * SC_LANES
                    if half == 0:
                        out_v[tl, pl.ds(c, SC_LANES)] = accs[2 * jj]
                        out_v[tl, pl.ds(w + c, SC_LANES)] = accs[2 * jj + 1]
                    else:
                        out_v[tl, pl.ds(c, SC_LANES)] = out_v[tl, pl.ds(c, SC_LANES)] + accs[2 * jj]
                        out_v[tl, pl.ds(w + c, SC_LANES)] = out_v[tl, pl.ds(w + c, SC_LANES)] + accs[2 * jj + 1]

        def block(t0):
            s0 = t0 * STEPS_PER_TOKEN
            pltpu.sync_copy(idx_hbm.at[pl.ds(s0, n_steps)], idx_v)
            pltpu.sync_copy(wgt_hbm.at[pl.ds(t0, SC_TOK)], wgt_v)
            _sc_pipelined_steps(table_hbm, idx_v, rows_v, sems, s0, consume)
            pltpu.sync_copy(out_v, out_hbm.at[pl.ds(t0, SC_TOK)])

        _sc_token_blocks(n, block)

    return expert_sum(table, idx_steps, wgt)


def _gelu_exact(x):
    return 0.5 * x * (1.0 + lax.erf(x * (2.0 ** -0.5)))


def _peer_weight_kernel(act_ref, gate_ref, o_ref):
    o_ref[...] = gate_ref[...] * _gelu_exact(act_ref[...])


def _peer_weight(act, gate):
    n = act.shape[0]
    tt = min(2048, n)
    spec = pl.BlockSpec((tt, PEER_SEL), lambda i: (i, 0))
    return pl.pallas_call(
        _peer_weight_kernel,
        grid=(n // tt,),
        in_specs=[spec, spec],
        out_specs=spec,
        out_shape=jax.ShapeDtypeStruct((n, PEER_SEL), F32),
        compiler_params=_cparams("parallel"),
        name="peer_weight",
    )(act, gate)


def _ple_kernel(h_ref, peer_ref, p_ref, gp_ref, gfin_ref, wg_ref, wp_ref, o_ref):
    h = h_ref[...] + peer_ref[...]
    e = _rms(h, gp_ref[...]).astype(BF16)
    gate = _sigmoid(jnp.dot(e, wg_ref[...], preferred_element_type=F32))
    proj = jnp.dot(p_ref[...].astype(BF16), wp_ref[...], preferred_element_type=F32)
    o_ref[...] = _rms(h + gate * proj, gfin_ref[...])


def _ple_final(h, peer, p, g_ple, g_final, wg, wp, first_token):
    n, d = h.shape
    tm = min(512, n)
    first = first_token // tm
    return pl.pallas_call(
        _ple_kernel,
        grid=(n // tm,),
        in_specs=[pl.BlockSpec((tm, d), lambda i: (i, 0)),
                  pl.BlockSpec((tm, d), lambda i: (i, 0)),
                  pl.BlockSpec((tm, D_PLE), lambda i: (first + i, 0)),
                  pl.BlockSpec((1, d), lambda i: (0, 0)),
                  pl.BlockSpec((1, d), lambda i: (0, 0)),
                  pl.BlockSpec((d, d), lambda i: (0, 0)),
                  pl.BlockSpec((D_PLE, d), lambda i: (0, 0))],
        out_specs=pl.BlockSpec((tm, d), lambda i: (i, 0)),
        out_shape=jax.ShapeDtypeStruct((n, d), F32),
        compiler_params=_cparams("parallel"),
        name="ple_final",
    )(h, peer, p, g_ple, g_final, wg, wp)


def _prep_weights(norm_mix, w_in, mlstm_b_i, mlstm_b_f, fox_b_f, w_br_m, w_br_f, w_out, norm_ffn,
                  peer_w_q, peer_keys1, peer_keys2, peer_u, peer_v, norm_ple, w_ple_gate, w_ple_proj,
                  norm_final):
    o = [0]
    for s in (W_M, W_M, W_M, W_M, H_M, H_M, W_F, W_F, W_F, H_F, D_MODEL, D_MODEL):
        o.append(o[-1] + s)
    seg = lambda a, b: w_in[:, o[a]:o[b]]
    w_gate = jnp.concatenate([seg(4, 6), seg(9, 10)], axis=1)
    w_gate = jnp.pad(w_gate, ((0, 0), (0, GATE_COLS - w_gate.shape[1])))
    b_gate = jnp.concatenate([mlstm_b_i, mlstm_b_f, fox_b_f]).astype(F32)
    b_gate = jnp.pad(b_gate, (0, GATE_COLS - b_gate.shape[0]))[None, :]
    row = lambda v: v.astype(F32)[None, :]
    return dict(
        norm_mix=row(norm_mix),
        w_mqkv=seg(0, 3).astype(BF16),
        w_og=jnp.concatenate([seg(3, 4), seg(10, 12)], axis=1).astype(BF16),
        w_fq=seg(6, 7).astype(BF16), w_fk=seg(7, 8).astype(BF16), w_fv=seg(8, 9).astype(BF16),
        w_gate=w_gate.astype(F32), b_gate=b_gate,
        w_br_m=w_br_m.astype(BF16), w_br_f=w_br_f.astype(BF16), w_out=w_out.astype(BF16),
        norm_ffn=row(norm_ffn), wqt=peer_w_q.T.astype(F32),
        k1=peer_keys1.astype(F32), k2=peer_keys2.astype(F32),
        u_pk=_pack_table(peer_u), v_pk=_pack_table(peer_v),
        norm_ple=row(norm_ple), w_ple_gate=w_ple_gate.astype(BF16), w_ple_proj=w_ple_proj.astype(BF16),
        norm_final=row(norm_final),
    )


def _ct_rows(c):
    b, t, _ = c.shape
    return jnp.swapaxes(c[:, :, COL_FF:COL_FF + H_F], 1, 2).reshape(b, H_F // 2, 2, t)


def _peer_block(h, w):
    n = h.shape[0]
    xn, idx, gate = _peer_score(h, w["norm_ffn"], w["wqt"], w["k1"], w["k2"])
    steps = idx.reshape(n * STEPS_PER_TOKEN, SC_ROWS)
    act = _sc_expert_dot(w["u_pk"], steps, xn)
    return _sc_expert_sum(w["v_pk"], steps, _peer_weight(act, gate))


def _layer(x, p, w, mstate, fox_cache):
    b, t, d = x.shape
    n = b * t
    h = x.reshape(n, d)
    g = w["norm_mix"]
    qkv = _norm_matmul(h, g, w["w_mqkv"], BF16, "proj_mlstm_qkv")
    og = _norm_matmul(h, g, w["w_og"], F32, "proj_gates")
    fq = _norm_matmul(h, g, w["w_fq"], BF16, "proj_fox_q")
    fk, fk_heads = _norm_matmul_heads(h, g, w["w_fk"], "proj_fox_k")
    fv, fv_heads = _norm_matmul_heads(h, g, w["w_fv"], "proj_fox_v")
    gates = _gates(h, g, w["w_gate"], w["b_gate"])

    c0, n0, m0 = mstate
    ym, c_new, n_new, m_new = _mlstm(
        qkv.reshape(b, t, 3 * W_M), og.reshape(b, t, 3 * D_MODEL), gates.reshape(b, t, GATE_COLS),
        c0.astype(F32), n0.astype(F32)[:, :, None, :],
        jnp.broadcast_to(m0.astype(F32)[:, :, None, None], (b, H_M, 1, LANES)))

    def finish(yf, first_token):
        hm = _merge(h, ym.reshape(n, W_M), yf, og, w["w_br_m"], w["w_br_f"], w["w_out"], first_token)
        return _ple_final(hm, _peer_block(hm, w), p.reshape(n, D_PLE), w["norm_ple"], w["norm_final"],
                          w["w_ple_gate"], w["w_ple_proj"], first_token)

    gates3 = gates.reshape(b, t, GATE_COLS)
    if fox_cache is None:
        c = _cumsum_tokens(gates3)
        seqs = (fq.reshape(b, t, W_F), fk.reshape(b, t, W_F), fv.reshape(b, t, W_F))
        y = jnp.concatenate([finish(_fox_prompt(*seqs, c, bi), bi * t) for bi in range(b)], axis=0)
    else:
        ck, cv, clf = fox_cache
        past = ck.shape[1]
        lf = jnp.pad(clf.astype(F32), ((0, 0), (0, 0), (COL_FF, GATE_COLS - COL_FF - H_F)))
        lf = jnp.concatenate([lf, gates3], axis=1)
        pad_t = (-lf.shape[1]) % 256
        c = _cumsum_tokens(jnp.pad(lf, ((0, 0), (0, pad_t), (0, 0))))
        ct = _ct_rows(c)
        yf = _fox_sample(fq.reshape(b, t, W_F), ck.astype(F32), cv.astype(F32),
                         fk.reshape(b, t, W_F), fv.reshape(b, t, W_F),
                         c[:, past:past + t, :], ct[..., :past], ct[..., past:past + t])
        y = finish(yf.reshape(n, W_F), 0)

    state = (fk_heads.reshape(1, b, t, H_F, DH_F), fv_heads.reshape(1, b, t, H_F, DH_F),
             gates3[None, :, :, COL_FF:COL_FF + H_F],
             c_new[None], n_new[None, :, :, 0, :], m_new[None, :, :, 0, 0])
    return y.reshape(b, t, d), state


def kernel(x_prompt, x_sample, p_prompt, p_sample, cache_fox_k, cache_fox_v, cache_fox_logf, state_mlstm_C, state_mlstm_n, state_mlstm_m, norm_mix, w_in, mlstm_b_i, mlstm_b_f, fox_b_f, w_br_m, w_br_f, w_out, norm_ffn, peer_w_q, peer_keys1, peer_keys2, peer_u, peer_v, norm_ple, w_ple_gate, w_ple_proj, norm_final):
    assert w_in.shape[0] == 1, "single-layer trunk"
    w = _prep_weights(norm_mix[0], w_in[0], mlstm_b_i[0], mlstm_b_f[0], fox_b_f[0], w_br_m[0], w_br_f[0],
                      w_out[0], norm_ffn[0], peer_w_q[0], peer_keys1[0], peer_keys2[0], peer_u[0], peer_v[0],
                      norm_ple[0], w_ple_gate[0], w_ple_proj[0], norm_final)
    bp = x_prompt.shape[0]
    zeros = (jnp.zeros((bp, H_M, DH_M, DH_M), F32), jnp.zeros((bp, H_M, DH_M), F32), jnp.zeros((bp, H_M), F32))
    y_p, sp = _layer(x_prompt, p_prompt[0], w, zeros, None)
    y_s, ss = _layer(x_sample, p_sample[0], w,
                     (state_mlstm_C[0], state_mlstm_n[0], state_mlstm_m[0]),
                     (cache_fox_k[0], cache_fox_v[0], cache_fox_logf[0]))
    return (y_p, y_s) + sp + ss
```

```python
import functools

import jax
import jax.numpy as jnp
from jax import lax
from jax.experimental import pallas as pl
from jax.experimental.pallas import tpu as pltpu
from jax.experimental.pallas import tpu_sc as plsc

D_MODEL = 1024
CHUNK = 64
H_M = 4
DH_M = 256
W_M = H_M * DH_M
H_F = 16
DH_F = 64
W_F = H_F * DH_F
D_PLE = 256
PEER_HEADS = 8
PEER_KEYS = 128
PEER_DQ = 256
PEER_TOPK = 16
PEER_SEL = PEER_HEADS * PEER_TOPK
EPS = 1e-6

LANES = 128
GATE_COLS = LANES
COL_I, COL_F, COL_FF = 0, H_M, 2 * H_M
VMEM_LIMIT = 56 * 1024 * 1024
HIGHEST = lax.Precision.HIGHEST
F32 = jnp.float32
BF16 = jnp.bfloat16
NEG_INF = float("-inf")


def _cparams(*sem):
    return pltpu.CompilerParams(dimension_semantics=sem, vmem_limit_bytes=VMEM_LIMIT)


def _rms(x, g):
    return x * lax.rsqrt(jnp.mean(x * x, axis=-1, keepdims=True) + EPS) * g


def _sigmoid(x):
    return 1.0 / (1.0 + jnp.exp(-x))


def _norm_matmul_kernel(x_ref, g_ref, w_ref, o_ref, a_ref):
    @pl.when(pl.program_id(1) == 0)
    def _():
        a_ref[...] = _rms(x_ref[...], g_ref[...]).astype(a_ref.dtype)

    o_ref[...] = jnp.dot(a_ref[...], w_ref[...], preferred_element_type=F32).astype(o_ref.dtype)


def _norm_matmul(x, g, w, out_dtype, name):
    n, d = x.shape
    cols = w.shape[1]
    tm = min(1024, n)
    tn = 1024
    return pl.pallas_call(
        _norm_matmul_kernel,
        grid=(n // tm, cols // tn),
        in_specs=[pl.BlockSpec((tm, d), lambda i, j: (i, 0)),
                  pl.BlockSpec((1, d), lambda i, j: (0, 0)),
                  pl.BlockSpec((d, tn), lambda i, j: (0, j))],
        out_specs=pl.BlockSpec((tm, tn), lambda i, j: (i, j)),
        out_shape=jax.ShapeDtypeStruct((n, cols), out_dtype),
        scratch_shapes=[pltpu.VMEM((tm, d), BF16)],
        compiler_params=_cparams("parallel", "arbitrary"),
        name=name,
    )(x, g, w)


def _norm_matmul_heads_kernel(x_ref, g_ref, w_ref, o_ref, oh_ref):
    a = _rms(x_ref[...], g_ref[...]).astype(BF16)
    z = jnp.dot(a, w_ref[...], preferred_element_type=F32)
    o_ref[...] = z.astype(o_ref.dtype)
    for h in range(H_F):
        oh_ref[:, h, :] = z[:, h * DH_F:(h + 1) * DH_F]


def _norm_matmul_heads(x, g, w, name):
    n, d = x.shape
    tm = min(512, n)
    return pl.pallas_call(
        _norm_matmul_heads_kernel,
        grid=(n // tm,),
        in_specs=[pl.BlockSpec((tm, d), lambda i: (i, 0)),
                  pl.BlockSpec((1, d), lambda i: (0, 0)),
                  pl.BlockSpec((d, W_F), lambda i: (0, 0))],
        out_specs=[pl.BlockSpec((tm, W_F), lambda i: (i, 0)),
                   pl.BlockSpec((tm, H_F, DH_F), lambda i: (i, 0, 0))],
        out_shape=[jax.ShapeDtypeStruct((n, W_F), BF16),
                   jax.ShapeDtypeStruct((n, H_F, DH_F), F32)],
        compiler_params=_cparams("parallel"),
        name=name,
    )(x, g, w)


def _gate_kernel(x_ref, g_ref, w_ref, b_ref, o_ref):
    a = _rms(x_ref[...], g_ref[...])
    z = jnp.dot(a, w_ref[...], precision=HIGHEST, preferred_element_type=F32) + b_ref[...]
    col = lax.broadcasted_iota(jnp.int32, z.shape, 1)
    log_sig = jnp.minimum(z, 0.0) - jnp.log1p(jnp.exp(-jnp.abs(z)))
    o_ref[...] = jnp.where(col < COL_F, z, log_sig)


def _gates(x, g, w, b):
    n, d = x.shape
    tm = min(512, n)
    return pl.pallas_call(
        _gate_kernel,
        grid=(n // tm,),
        in_specs=[pl.BlockSpec((tm, d), lambda i: (i, 0)),
                  pl.BlockSpec((1, d), lambda i: (0, 0)),
                  pl.BlockSpec((d, GATE_COLS), lambda i: (0, 0)),
                  pl.BlockSpec((1, GATE_COLS), lambda i: (0, 0))],
        out_specs=pl.BlockSpec((tm, GATE_COLS), lambda i: (i, 0)),
        out_shape=jax.ShapeDtypeStruct((n, GATE_COLS), F32),
        compiler_params=_cparams("parallel"),
        name="gates",
    )(x, g, w, b)


def _cumsum_kernel(x_ref, o_ref, carry_ref):
    @pl.when(pl.program_id(1) == 0)
    def _():
        carry_ref[...] = jnp.zeros_like(carry_ref)

    x = x_ref[...]
    tb = x.shape[0]
    row = lax.broadcasted_iota(jnp.int32, (tb, tb), 0)
    col = lax.broadcasted_iota(jnp.int32, (tb, tb), 1)
    tril = jnp.where(col <= row, 1.0, 0.0).astype(F32)
    c = jnp.dot(tril, x, precision=HIGHEST, preferred_element_type=F32) + carry_ref[...]
    o_ref[...] = c
    carry_ref[...] = c[tb - 1:tb, :]


def _cumsum_tokens(x):
    b, t, w = x.shape
    tb = 256
    return pl.pallas_call(
        _cumsum_kernel,
        grid=(b, t // tb),
        in_specs=[pl.BlockSpec((None, tb, w), lambda i, j: (i, j, 0))],
        out_specs=pl.BlockSpec((None, tb, w), lambda i, j: (i, j, 0)),
        out_shape=jax.ShapeDtypeStruct((b, t, w), F32),
        scratch_shapes=[pltpu.VMEM((1, w), F32)],
        compiler_params=_cparams("parallel", "arbitrary"),
        name="cumsum",
    )(x)


def _mlstm_kernel(qkv_ref, og_ref, g_ref, c0_ref, n0_ref, m0_ref,
                  y_ref, cn_ref, nn_ref, mn_ref, c_s, n_s, m_s, *, bb_n, blk):
    step = pl.program_id(1)

    @pl.when(step == 0)
    def _():
        c_s[...] = c0_ref[...]
        n_s[...] = n0_ref[...]
        m_s[...] = m0_ref[...]

    row = lax.broadcasted_iota(jnp.int32, (blk, blk), 0)
    col = lax.broadcasted_iota(jnp.int32, (blk, blk), 1)
    tril = col <= row
    triu = row <= col
    eye = col == row

    def to_row(x_col):
        return jnp.sum(jnp.where(eye, x_col, 0.0), axis=0, keepdims=True)

    for bb in range(bb_n):
        g = g_ref[bb]
        for h in range(H_M):
            q = qkv_ref[bb, :, h * DH_M:(h + 1) * DH_M]
            k = qkv_ref[bb, :, (H_M + h) * DH_M:(H_M + h + 1) * DH_M] * (DH_M ** -0.5)
            v = qkv_ref[bb, :, (2 * H_M + h) * DH_M:(2 * H_M + h + 1) * DH_M]
            i_col = g[:, COL_I + h:COL_I + h + 1]
            f_col = g[:, COL_F + h:COL_F + h + 1]
            i_row = to_row(i_col)
            f_row = to_row(f_col)
            b_col = jnp.sum(jnp.where(tril, f_row, 0.0), axis=1, keepdims=True)
            b_row = jnp.sum(jnp.where(triu, f_col, 0.0), axis=0, keepdims=True)
            m_prev = m_s[bb, h][:, 0:1]
            dmat = jnp.where(tril, b_col - b_row + i_row, NEG_INF)
            g_col = b_col + m_prev
            mt = jnp.maximum(g_col, jnp.max(dmat, axis=1, keepdims=True))
            w_d = jnp.exp(dmat - mt)
            w_g = jnp.exp(g_col - mt)
            qk = lax.dot_general(q, k, (((1,), (1,)), ((), ())), preferred_element_type=F32) * w_d
            c_prev = c_s[bb, h]
            n_prev = n_s[bb, h]
            num = (w_g * jnp.dot(q, c_prev.astype(BF16), preferred_element_type=F32)
                   + jnp.dot(qk.astype(BF16), v, preferred_element_type=F32))
            den = (w_g * jnp.sum(q.astype(F32) * n_prev, axis=1, keepdims=True)
                   + jnp.sum(qk, axis=1, keepdims=True))
            hid = num / jnp.maximum(jnp.abs(den), jnp.exp(-mt))
            o_gate = _sigmoid(og_ref[bb, :, h * DH_M:(h + 1) * DH_M])
            y_ref[bb, :, h * DH_M:(h + 1) * DH_M] = (o_gate * hid).astype(y_ref.dtype)
            m_new = mt[blk - 1:blk, :]
            b_last = b_col[blk - 1:blk, :]
            w_c = jnp.exp(b_last + m_prev - m_new)
            w_s = jnp.exp(b_last - b_col + i_col - m_new)
            kw = k.astype(F32) * w_s
            c_s[bb, h] = w_c * c_prev + lax.dot_general(
                kw.astype(BF16), v, (((0,), (0,)), ((), ())), preferred_element_type=F32)
            n_s[bb, h] = w_c * n_prev + jnp.sum(kw, axis=0, keepdims=True)
            m_s[bb, h] = jnp.broadcast_to(m_new, (1, LANES))

    @pl.when(step == pl.num_programs(1) - 1)
    def _():
        cn_ref[...] = c_s[...]
        nn_ref[...] = n_s[...]
        mn_ref[...] = m_s[...]


def _mlstm(qkv, ogate, gates, c0, n0, m0):
    b, t, _ = qkv.shape
    blk = min(CHUNK, t)
    bb_n = 4
    state_spec = lambda shape: pl.BlockSpec((bb_n,) + shape, lambda i, j: (i, 0, 0, 0))
    return pl.pallas_call(
        functools.partial(_mlstm_kernel, bb_n=bb_n, blk=blk),
        grid=(b // bb_n, t // blk),
        in_specs=[pl.BlockSpec((bb_n, blk, 3 * W_M), lambda i, j: (i, j, 0)),
                  pl.BlockSpec((bb_n, blk, W_M), lambda i, j: (i, j, 0)),
                  pl.BlockSpec((bb_n, blk, GATE_COLS), lambda i, j: (i, j, 0)),
                  state_spec((H_M, DH_M, DH_M)),
                  state_spec((H_M, 1, DH_M)),
                  state_spec((H_M, 1, LANES))],
        out_specs=[pl.BlockSpec((bb_n, blk, W_M), lambda i, j: (i, j, 0)),
                   state_spec((H_M, DH_M, DH_M)),
                   state_spec((H_M, 1, DH_M)),
                   state_spec((H_M, 1, LANES))],
        out_shape=[jax.ShapeDtypeStruct((b, t, W_M), BF16),
                   jax.ShapeDtypeStruct((b, H_M, DH_M, DH_M), F32),
                   jax.ShapeDtypeStruct((b, H_M, 1, DH_M), F32),
                   jax.ShapeDtypeStruct((b, H_M, 1, LANES), F32)],
        scratch_shapes=[pltpu.VMEM((bb_n, H_M, DH_M, DH_M), F32),
                        pltpu.VMEM((bb_n, H_M, 1, DH_M), F32),
                        pltpu.VMEM((bb_n, H_M, 1, LANES), F32)],
        compiler_params=_cparams("parallel", "arbitrary"),
        name="mlstm",
    )(qkv, ogate, gates, c0, n0, m0)


def _online_softmax_step(s, pv, m_ref, l_ref, acc_ref):
    m_old = m_ref[...]
    m_new = jnp.maximum(m_old, jnp.max(s, axis=1, keepdims=True))
    alpha = jnp.exp(m_old - m_new)
    p = jnp.exp(s - m_new)
    l_ref[...] = alpha * l_ref[...] + jnp.sum(p, axis=1, keepdims=True)
    acc_ref[...] = alpha * acc_ref[...] + pv(p.astype(BF16))
    m_ref[...] = m_new


def _split3(x):
    hi = x.astype(BF16).astype(F32)
    mid = (x - hi).astype(BF16).astype(F32)
    lo = (x - hi - mid).astype(BF16).astype(F32)
    return hi, mid, lo


FOX_V_ROWS = DH_F + 16
AUG_CQ = 6


def _fox_prompt_kernel(q_ref, k_ref, v_ref, c_ref, o_ref, kaug_s, vt_s, m0_s, m1_s, acc0_s, acc1_s,
                       *, tq, n_blocks):
    hp = pl.program_id(1)
    qi = pl.program_id(2)
    lane = lax.broadcasted_iota(jnp.int32, (tq, LANES), 1)
    m_s, acc_s = (m0_s, m1_s), (acc0_s, acc1_s)

    def head_col(c_blk, x):
        return jnp.sum(jnp.where(lane == COL_FF + 2 * hp + x, c_blk, 0.0), axis=1, keepdims=True)

    def aug_tile(entries):
        tile = jnp.zeros((tq, LANES), F32)
        for l, val in entries:
            tile = jnp.where(lane == l, val, tile)
        return tile.astype(BF16)

    @pl.when(qi == 0)
    def _():
        ones_rows = jnp.where(lax.broadcasted_iota(jnp.int32, (FOX_V_ROWS - DH_F, tq), 0) == 0, 1.0, 0.0)

        def chunk(i, carry):
            rs = pl.multiple_of(i * tq, tq)
            c_blk = c_ref[pl.ds(rs, tq), :]
            pieces = _split3(head_col(c_blk, 0)) + _split3(head_col(c_blk, 1))
            entries = list(enumerate(pieces)) + [(AUG_CQ + j, 1.0) for j in range(3)]
            kaug_s[pl.ds(rs, tq), 0:LANES] = k_ref[pl.ds(rs, tq), :]
            kaug_s[pl.ds(rs, tq), LANES:2 * LANES] = aug_tile(entries)
            vt = v_ref[pl.ds(rs, tq), :].astype(F32).T
            for x in range(2):
                vt_s[x, 0:DH_F, pl.ds(rs, tq)] = vt[x * DH_F:(x + 1) * DH_F, :].astype(BF16)
                vt_s[x, DH_F:FOX_V_ROWS, pl.ds(rs, tq)] = ones_rows.astype(BF16)
            return carry

        lax.fori_loop(0, n_blocks, chunk, 0)

    qs = pl.multiple_of(qi * tq, tq)
    c_q = c_ref[pl.ds(qs, tq), :]
    q2 = q_ref[...] * (DH_F ** -0.5)
    q_aug = []
    for x in range(2):
        cq3 = _split3(head_col(c_q, x))
        entries = [(3 * x + j, -1.0) for j in range(3)] + [(AUG_CQ + j, cq3[j]) for j in range(3)]
        q_head = jnp.where((lane < DH_F) == (x == 0), q2, jnp.zeros_like(q2))
        q_aug.append(jnp.concatenate([q_head, aug_tile(entries)], axis=1))
        m_s[x][...] = jnp.full(m_s[x].shape, NEG_INF, F32)
        acc_s[x][...] = jnp.zeros(acc_s[x].shape, F32)

    def block(kb, masked):
        ks = pl.multiple_of(kb * tq, tq)
        k_blk = kaug_s[pl.ds(ks, tq), :]
        scores = [lax.dot_general(k_blk, q_aug[x], (((1,), (1,)), ((), ())), preferred_element_type=F32)
                  for x in range(2)]
        for x in range(2):
            st = scores[x]
            if masked:
                k_pos = lax.broadcasted_iota(jnp.int32, (tq, tq), 0)
                q_pos = lax.broadcasted_iota(jnp.int32, (tq, tq), 1)
                st = jnp.where(k_pos <= q_pos, st, NEG_INF)
            m_old = m_s[x][...]
            m_new = jnp.maximum(m_old, jnp.max(st, axis=0, keepdims=True))
            p = jnp.exp(st - m_new).astype(BF16)
            acc_s[x][...] = (jnp.exp(m_old - m_new) * acc_s[x][...]
                             + jnp.dot(vt_s[x, :, pl.ds(ks, tq)], p, preferred_element_type=F32))
            m_s[x][...] = m_new

    def body(kb, carry):
        block(kb, False)
        return carry

    lax.fori_loop(0, qi, body, 0)
    block(qi, True)
    out_t = jnp.concatenate([acc_s[x][0:DH_F, :] / acc_s[x][DH_F:DH_F + 1, :] for x in range(2)], axis=0)
    o_ref[...] = out_t.T.astype(o_ref.dtype)


def _fox_prompt(q, k, v, c, batch):
    _, t, _ = q.shape
    tq = 512
    head_pair = lambda i, h, j: (batch, 0, h)
    return pl.pallas_call(
        functools.partial(_fox_prompt_kernel, tq=tq, n_blocks=t // tq),
        grid=(1, H_F // 2, t // tq),
        in_specs=[pl.BlockSpec((None, tq, LANES), lambda i, h, j: (batch, j, h)),
                  pl.BlockSpec((None, t, LANES), head_pair),
                  pl.BlockSpec((None, t, LANES), head_pair),
                  pl.BlockSpec((None, t, LANES), lambda i, h, j: (batch, 0, 0))],
        out_specs=pl.BlockSpec((tq, LANES), lambda i, h, j: (j, h)),
        out_shape=jax.ShapeDtypeStruct((t, W_F), BF16),
        scratch_shapes=[pltpu.VMEM((t, 2 * LANES), BF16),
                        pltpu.VMEM((2, FOX_V_ROWS, t), BF16),
                        pltpu.VMEM((1, tq), F32), pltpu.VMEM((1, tq), F32),
                        pltpu.VMEM((FOX_V_ROWS, tq), F32), pltpu.VMEM((FOX_V_ROWS, tq), F32)],
        compiler_params=_cparams("parallel", "parallel", "arbitrary"),
        name="fox_prompt",
    )(q, k, v, c)


def _fox_sample_kernel(q_ref, kc_ref, vc_ref, kn_ref, vn_ref, cq_ref, ctc_ref, ctn_ref, o_ref,
                       m_s, l_s, acc_s, *, tn):
    kb = pl.program_id(1)

    @pl.when(kb == 0)
    def _():
        m_s[...] = jnp.full_like(m_s, NEG_INF)
        l_s[...] = jnp.zeros_like(l_s)
        acc_s[...] = jnp.zeros_like(acc_s)

    lane = lax.broadcasted_iota(jnp.int32, (tn, LANES), 1)
    c_blk = cq_ref[...]
    nt = (((1,), (1,)), ((), ()))

    def head_query(hd):
        q = q_ref[:, hd * DH_F:(hd + 1) * DH_F] * (DH_F ** -0.5)
        cq = jnp.sum(jnp.where(lane == COL_FF + hd, c_blk, 0.0), axis=1, keepdims=True)
        return q, cq

    for hd in range(H_F):
        q, cq = head_query(hd)
        k_t = kc_ref[hd].astype(BF16)
        v_t = vc_ref[hd].astype(BF16)
        s = jnp.dot(q, k_t, preferred_element_type=F32) + cq - ctc_ref[hd // 2, hd % 2:hd % 2 + 1, :]
        _online_softmax_step(s, lambda p: lax.dot_general(p, v_t, nt, preferred_element_type=F32),
                             m_s.at[hd], l_s.at[hd], acc_s.at[hd])

    @pl.when(kb == pl.num_programs(1) - 1)
    def _():
        q_pos = lax.broadcasted_iota(jnp.int32, (tn, tn), 0)
        k_pos = lax.broadcasted_iota(jnp.int32, (tn, tn), 1)
        for hd in range(H_F):
            q, cq = head_query(hd)
            k_new = kn_ref[:, hd * DH_F:(hd + 1) * DH_F]
            v_new = vn_ref[:, hd * DH_F:(hd + 1) * DH_F]
            s = (lax.dot_general(q, k_new, nt, preferred_element_type=F32)
                 + cq - ctn_ref[hd // 2, hd % 2:hd % 2 + 1, :])
            s = jnp.where(k_pos <= q_pos, s, NEG_INF)
            _online_softmax_step(s, lambda p: jnp.dot(p, v_new, preferred_element_type=F32),
                                 m_s.at[hd], l_s.at[hd], acc_s.at[hd])
            o_ref[:, hd * DH_F:(hd + 1) * DH_F] = (acc_s[hd] / l_s[hd]).astype(o_ref.dtype)


def _fox_sample(q, k_cache_t, v_cache_t, k_new, v_new, cq, ct_cache, ct_new):
    b, tn, _ = q.shape
    p = k_cache_t.shape[3]
    tk = 1024
    new = pl.BlockSpec((None, tn, W_F), lambda i, j: (i, 0, 0))
    cache = pl.BlockSpec((None, H_F, DH_F, tk), lambda i, j: (i, 0, 0, j))
    return pl.pallas_call(
        functools.partial(_fox_sample_kernel, tn=tn),
        grid=(b, p // tk),
        in_specs=[new, cache, cache, new, new,
                  pl.BlockSpec((None, tn, LANES), lambda i, j: (i, 0, 0)),
                  pl.BlockSpec((None, H_F // 2, 2, tk), lambda i, j: (i, 0, 0, j)),
                  pl.BlockSpec((None, H_F // 2, 2, tn), lambda i, j: (i, 0, 0, 0))],
        out_specs=new,
        out_shape=jax.ShapeDtypeStruct((b, tn, W_F), BF16),
        scratch_shapes=[pltpu.VMEM((H_F, tn, 1), F32),
                        pltpu.VMEM((H_F, tn, 1), F32),
                        pltpu.VMEM((H_F, tn, DH_F), F32)],
        compiler_params=_cparams("parallel", "arbitrary"),
        name="fox_sample",
    )(q, k_cache_t, v_cache_t, k_new, v_new, cq, ct_cache, ct_new)


def _merge_kernel(h_ref, ym_ref, yf_ref, gm_ref, gf_ref, wm_ref, wf_ref, wo_ref, after_a, after_b, o_ref):
    del after_a, after_b
    a = jnp.dot(ym_ref[...], wm_ref[...], preferred_element_type=F32)
    b = jnp.dot(yf_ref[...], wf_ref[...], preferred_element_type=F32)
    merge = _sigmoid(gm_ref[...]) * a + _sigmoid(gf_ref[...]) * b
    o_ref[...] = h_ref[...] + jnp.dot(merge.astype(BF16), wo_ref[...], preferred_element_type=F32)


def _merge(h, ym, yf, gates, wm, wf, wo, first_token, after):
    n, d = yf.shape
    tm = min(512, n)
    first = first_token // tm
    tok = lambda c: pl.BlockSpec((tm, d), lambda i: (first + i, c))
    local = pl.BlockSpec((tm, d), lambda i: (i, 0))
    wspec = pl.BlockSpec((d, d), lambda i: (0, 0))
    return pl.pallas_call(
        _merge_kernel,
        grid=(n // tm,),
        in_specs=[tok(0), tok(0), local, tok(1), tok(2), wspec, wspec, wspec,
                  pl.BlockSpec(memory_space=pl.ANY), pl.BlockSpec(memory_space=pl.ANY)],
        out_specs=local,
        out_shape=jax.ShapeDtypeStruct((n, d), F32),
        compiler_params=_cparams("parallel"),
        name="merge",
    )(h, ym, yf, gates, gates, wm, wf, wo, *after)


def _top_rows(s, count, ids=None, payload=None):
    if ids is None:
        ids = lax.broadcasted_iota(jnp.int32, s.shape, 0)
    big = jnp.int32(2 ** 30)
    vals, sel, pay = [], [], []
    for _ in range(count):
        m = jnp.max(s, axis=0, keepdims=True)
        am = jnp.min(jnp.where(s == m, ids, big), axis=0, keepdims=True)
        hit = ids == am
        vals.append(m)
        sel.append(am)
        if payload is not None:
            pay.append(jnp.max(jnp.where(hit, payload, -1), axis=0, keepdims=True))
        s = jnp.where(hit, NEG_INF, s)
    cat = lambda xs: jnp.concatenate(xs, axis=0)
    return cat(vals), cat(sel), (cat(pay) if payload is not None else None)


def _pair_candidates(v1, i1, v2, i2):
    t = v1.shape[1]
    half = PEER_TOPK // 2
    r16 = lax.broadcasted_iota(jnp.int32, (PEER_TOPK, t), 0)
    r8 = lax.broadcasted_iota(jnp.int32, (half, t), 0)
    sums = [v1[0:1, :] + v2]
    flat = [r16]
    expert = [i1[0:1, :] * PEER_KEYS + i2]
    for a in range(1, half):
        sums.append(v1[a:a + 1, :] + v2[0:half, :])
        flat.append(r8 + a * PEER_TOPK)
        expert.append(i1[a:a + 1, :] * PEER_KEYS + i2[0:half, :])
    sums.append(v1[half:, :] + v2[0:1, :])
    flat.append((r8 + half) * PEER_TOPK)
    expert.append(i1[half:, :] * PEER_KEYS + i2[0:1, :])
    cat = lambda xs: jnp.concatenate(xs, axis=0)
    return cat(sums), cat(flat), cat(expert)


def _peer_score_kernel(h_ref, g_ref, wqt_ref, k1_ref, k2_ref, xn_ref, idx_ref, gate_ref):
    xn = _rms(h_ref[...], g_ref[...])
    xn_ref[...] = xn.astype(xn_ref.dtype)
    qt = lax.dot_general(wqt_ref[...], xn, (((1,), (1,)), ((), ())),
                         precision=HIGHEST, preferred_element_type=F32)
    half = PEER_DQ // 2
    idx_rows, gate_rows = [], []
    for hd in range(PEER_HEADS):
        q1 = qt[hd * PEER_DQ:hd * PEER_DQ + half, :]
        q2 = qt[hd * PEER_DQ + half:(hd + 1) * PEER_DQ, :]
        s1 = jnp.dot(k1_ref[hd], q1, precision=HIGHEST, preferred_element_type=F32)
        s2 = jnp.dot(k2_ref[hd], q2, precision=HIGHEST, preferred_element_type=F32)
        v1, i1, _ = _top_rows(s1, PEER_TOPK)
        v2, i2, _ = _top_rows(s2, PEER_TOPK)
        cand, flat, expert = _pair_candidates(v1, i1, v2, i2)
        sc, _, ex = _top_rows(cand, PEER_TOPK, ids=flat, payload=expert)
        e = jnp.exp(sc - sc[0:1, :])
        gate_rows.append(e / jnp.sum(e, axis=0, keepdims=True))
        idx_rows.append(ex)
    gate_ref[...] = jnp.concatenate(gate_rows, axis=0).T
    idx_ref[...] = jnp.concatenate(idx_rows, axis=0).T


def _peer_score(h, g, wqt, k1, k2):
    n, d = h.shape
    tt = min(256, n)
    return pl.pallas_call(
        _peer_score_kernel,
        grid=(n // tt,),
        in_specs=[pl.BlockSpec((tt, d), lambda i: (i, 0)),
                  pl.BlockSpec((1, d), lambda i: (0, 0)),
                  pl.BlockSpec(wqt.shape, lambda i: (0, 0)),
                  pl.BlockSpec(k1.shape, lambda i: (0, 0, 0)),
                  pl.BlockSpec(k2.shape, lambda i: (0, 0, 0))],
        out_specs=[pl.BlockSpec((tt, d), lambda i: (i, 0)),
                   pl.BlockSpec((tt, PEER_SEL), lambda i: (i, 0)),
                   pl.BlockSpec((tt, PEER_SEL), lambda i: (i, 0))],
        out_shape=[jax.ShapeDtypeStruct((n, d), F32),
                   jax.ShapeDtypeStruct((n, PEER_SEL), jnp.int32),
                   jax.ShapeDtypeStruct((n, PEER_SEL), F32)],
        compiler_params=_cparams("parallel"),
        name="peer_score",
    )(h, g, wqt, k1, k2)


SC_CORES = 2
SC_SUBCORES = 16
SC_WORKERS = SC_CORES * SC_SUBCORES
SC_ROWS = 64


def _pack_table(tab):
    half = tab.shape[1] // 2
    bits = lax.bitcast_convert_type(tab.astype(BF16), jnp.uint16).astype(jnp.uint32)
    word = bits[:, :half] | (bits[:, half:] << 16)
    return lax.bitcast_convert_type(word, jnp.int32)


SC_TOK = 8
SC_LANES = 16
STEPS_PER_TOKEN = PEER_SEL // SC_ROWS
SC_PARAMS = pltpu.CompilerParams(needs_layout_passes=False)


def _row_source(table_hbm, idx_v, local_step, global_step):
    del global_step
    return table_hbm.at[idx_v.at[local_step]]


def _sc_unpack(wd):
    lo = lax.bitcast_convert_type(wd << 16, F32)
    hi = lax.bitcast_convert_type(wd & jnp.int32(-65536), F32)
    return lo, hi


def _sc_token_blocks(n, body_block):
    per_worker = n // SC_WORKERS
    assert per_worker * SC_WORKERS == n and per_worker % SC_TOK == 0
    wid = lax.axis_index("s") * SC_CORES + lax.axis_index("c")

    @pl.loop(0, per_worker // SC_TOK)
    def _(blk):
        body_block(wid * per_worker + blk * SC_TOK)


def _sc_pipelined_steps(table_hbm, idx_v, rows_v, sems, first_step, consume):
    n_steps = SC_TOK * STEPS_PER_TOKEN

    def row_gather(j, slot):
        return pltpu.make_async_copy(_row_source(table_hbm, idx_v, j, first_step + j),
                                     rows_v.at[slot], sems.at[slot])

    row_gather(0, 0).start()

    @pl.loop(0, n_steps, step=2)
    def _(i):
        for slot in range(2):
            j = i + slot

            @pl.when(j + 1 < n_steps)
            def _():
                row_gather(j + 1, 1 - slot).start()

            row_gather(j, slot).wait()
            consume(slot, i // 2, slot)


def _sc_expert_dot(table, idx_steps, x):
    n, d = x.shape
    w = d // 2
    assert STEPS_PER_TOKEN == 2 and table.shape[1] == w
    mesh = plsc.VectorSubcoreMesh(core_axis_name="c", subcore_axis_name="s")
    n_steps = SC_TOK * STEPS_PER_TOKEN
    group = 4

    @functools.partial(
        pl.kernel, mesh=mesh,
        out_type=jax.ShapeDtypeStruct((n, PEER_SEL), F32),
        scratch_types=[pltpu.VMEM((n_steps, SC_ROWS), jnp.int32),
                       pltpu.VMEM((SC_TOK, d), F32),
                       pltpu.VMEM((2, SC_ROWS, w), jnp.int32),
                       pltpu.VMEM((SC_TOK, PEER_SEL), F32),
                       pltpu.SemaphoreType.DMA((2,))],
        compiler_params=SC_PARAMS,
    )
    def expert_dot(table_hbm, idx_hbm, x_hbm, act_hbm, idx_v, x_v, rows_v, act_v, sems):
        lanes = lax.iota(jnp.int32, SC_LANES)
        zero = jnp.zeros((SC_LANES,), F32)

        def consume(slot, tl, half):
            @pl.loop(0, SC_ROWS // SC_LANES)
            def _(g):
                act_vec = zero
                for q in range(SC_LANES // group):
                    r0 = g * SC_LANES + q * group

                    def chunk(j, accs):
                        c = pl.multiple_of(j * SC_LANES, SC_LANES)
                        x_lo = x_v[tl, pl.ds(c, SC_LANES)]
                        x_hi = x_v[tl, pl.ds(w + c, SC_LANES)]
                        out = []
                        for rr in range(group):
                            lo, hi = _sc_unpack(rows_v[slot, r0 + rr, pl.ds(c, SC_LANES)])
                            out.append(accs[rr] + lo * x_lo + hi * x_hi)
                        return tuple(out)

                    accs = lax.fori_loop(0, w // SC_LANES, chunk, (zero,) * group, unroll=2)
                    for rr in range(group):
                        act_vec = jnp.where(lanes == q * group + rr, jnp.sum(accs[rr]), act_vec)
                act_v[tl, pl.ds(half * SC_ROWS + g * SC_LANES, SC_LANES)] = act_vec

        def block(t0):
            s0 = t0 * STEPS_PER_TOKEN
            pltpu.sync_copy(idx_hbm.at[pl.ds(s0, n_steps)], idx_v)
            pltpu.sync_copy(x_hbm.at[pl.ds(t0, SC_TOK)], x_v)
            _sc_pipelined_steps(table_hbm, idx_v, rows_v, sems, s0, consume)
            pltpu.sync_copy(act_v, act_hbm.at[pl.ds(t0, SC_TOK)])

        _sc_token_blocks(n, block)

    return expert_dot(table, idx_steps, x)


def _sc_expert_sum(table, idx_steps, wgt):
    n = wgt.shape[0]
    w = table.shape[1]
    d = 2 * w
    assert STEPS_PER_TOKEN == 2
    mesh = plsc.VectorSubcoreMesh(core_axis_name="c", subcore_axis_name="s")
    n_steps = SC_TOK * STEPS_PER_TOKEN
    cols = 8

    @functools.partial(
        pl.kernel, mesh=mesh,
        out_type=jax.ShapeDtypeStruct((n, d), F32),
        scratch_types=[pltpu.VMEM((n_steps, SC_ROWS), jnp.int32),
                       pltpu.VMEM((SC_TOK, PEER_SEL), F32),
                       pltpu.VMEM((2, SC_ROWS, w), jnp.int32),
                       pltpu.VMEM((SC_TOK, d), F32),
                       pltpu.VMEM((SC_ROWS, SC_LANES), F32),
                       pltpu.SemaphoreType.DMA((2,))],
        compiler_params=SC_PARAMS,
    )
    def expert_sum(table_hbm, idx_hbm, wgt_hbm, out_hbm, idx_v, wgt_v, rows_v, out_v, splat_v, sems):
        zero = jnp.zeros((SC_LANES,), F32)
        lanes = lax.iota(jnp.int32, SC_LANES)

        def consume(slot, tl, half):
            @pl.loop(0, SC_ROWS // SC_LANES)
            def _(g):
                w16 = wgt_v[tl, pl.ds(half * SC_ROWS + g * SC_LANES, SC_LANES)]
                for rr in range(SC_LANES):
                    one = jnp.sum(jnp.where(lanes == rr, w16, 0.0))
                    splat_v[g * SC_LANES + rr, :] = jnp.full((SC_LANES,), one, F32)

            for cb in range(w // (cols * SC_LANES)):
                base = cb * cols * SC_LANES

                def row(r, accs):
                    wv = splat_v[r, :]
                    out = []
                    for jj in range(cols):
                        lo, hi = _sc_unpack(rows_v[slot, r, pl.ds(base + jj * SC_LANES, SC_LANES)])
                        out.append(accs[2 * jj] + wv * lo)
                        out.append(accs[2 * jj + 1] + wv * hi)
                    return tuple(out)

                accs = lax.fori_loop(0, SC_ROWS, row, (zero,) * (2 * cols), unroll=2)
                for jj in range(cols):
                    c = base + jj * SC_LANES
                    if half == 0:
                        out_v[tl, pl.ds(c, SC_LANES)] = accs[2 * jj]
                        out_v[tl, pl.ds(w + c, SC_LANES)] = accs[2 * jj + 1]
                    else:
                        out_v[tl, pl.ds(c, SC_LANES)] = out_v[tl, pl.ds(c, SC_LANES)] + accs[2 * jj]
                        out_v[tl, pl.ds(w + c, SC_LANES)] = out_v[tl, pl.ds(w + c, SC_LANES)] + accs[2 * jj + 1]

        def block(t0):
            s0 = t0 * STEPS_PER_TOKEN
            pltpu.sync_copy(idx_hbm.at[pl.ds(s0, n_steps)], idx_v)
            pltpu.sync_copy(wgt_hbm.at[pl.ds(t0, SC_TOK)], wgt_v)
            _sc_pipelined_steps(table_hbm, idx_v, rows_v, sems, s0, consume)
            pltpu.sync_copy(out_v, out_hbm.at[pl.ds(t0, SC_TOK)])

        _sc_token_blocks(n, block)

    return expert_sum(table, idx_steps, wgt)


def _gelu_exact(x):
    return 0.5 * x * (1.0 + lax.erf(x * (2.0 ** -0.5)))


def _peer_weight_kernel(act_ref, gate_ref, o_ref):
    o_ref[...] = gate_ref[...] * _gelu_exact(act_ref[...])


def _peer_weight(act, gate):
    n = act.shape[0]
    tt = min(2048, n)
    spec = pl.BlockSpec((tt, PEER_SEL), lambda i: (i, 0))
    return pl.pallas_call(
        _peer_weight_kernel,
        grid=(n // tt,),
        in_specs=[spec, spec],
        out_specs=spec,
        out_shape=jax.ShapeDtypeStruct((n, PEER_SEL), F32),
        compiler_params=_cparams("parallel"),
        name="peer_weight",
    )(act, gate)


def _ple_kernel(h_ref, peer_ref, p_ref, gp_ref, gfin_ref, wg_ref, wp_ref, o_ref):
    h = h_ref[...] + peer_ref[...]
    e = _rms(h, gp_ref[...]).astype(BF16)
    gate = _sigmoid(jnp.dot(e, wg_ref[...], preferred_element_type=F32))
    proj = jnp.dot(p_ref[...].astype(BF16), wp_ref[...], preferred_element_type=F32)
    o_ref[...] = _rms(h + gate * proj, gfin_ref[...])


def _ple_final(h, peer, p, g_ple, g_final, wg, wp, first_token):
    n, d = h.shape
    tm = min(512, n)
    first = first_token // tm
    return pl.pallas_call(
        _ple_kernel,
        grid=(n // tm,),
        in_specs=[pl.BlockSpec((tm, d), lambda i: (i, 0)),
                  pl.BlockSpec((tm, d), lambda i: (i, 0)),
                  pl.BlockSpec((tm, D_PLE), lambda i: (first + i, 0)),
                  pl.BlockSpec((1, d), lambda i: (0, 0)),
                  pl.BlockSpec((1, d), lambda i: (0, 0)),
                  pl.BlockSpec((d, d), lambda i: (0, 0)),
                  pl.BlockSpec((D_PLE, d), lambda i: (0, 0))],
        out_specs=pl.BlockSpec((tm, d), lambda i: (i, 0)),
        out_shape=jax.ShapeDtypeStruct((n, d), F32),
        compiler_params=_cparams("parallel"),
        name="ple_final",
    )(h, peer, p, g_ple, g_final, wg, wp)


def _prep_weights(norm_mix, w_in, mlstm_b_i, mlstm_b_f, fox_b_f, w_br_m, w_br_f, w_out, norm_ffn,
                  peer_w_q, peer_keys1, peer_keys2, peer_u, peer_v, norm_ple, w_ple_gate, w_ple_proj,
                  norm_final):
    o = [0]
    for s in (W_M, W_M, W_M, W_M, H_M, H_M, W_F, W_F, W_F, H_F, D_MODEL, D_MODEL):
        o.append(o[-1] + s)
    seg = lambda a, b: w_in[:, o[a]:o[b]]
    w_gate = jnp.concatenate([seg(4, 6), seg(9, 10)], axis=1)
    w_gate = jnp.pad(w_gate, ((0, 0), (0, GATE_COLS - w_gate.shape[1])))
    b_gate = jnp.concatenate([mlstm_b_i, mlstm_b_f, fox_b_f]).astype(F32)
    b_gate = jnp.pad(b_gate, (0, GATE_COLS - b_gate.shape[0]))[None, :]
    row = lambda v: v.astype(F32)[None, :]
    return dict(
        norm_mix=row(norm_mix),
        w_mqkv=seg(0, 3).astype(BF16),
        w_og=jnp.concatenate([seg(3, 4), seg(10, 12)], axis=1).astype(BF16),
        w_fq=seg(6, 7).astype(BF16), w_fk=seg(7, 8).astype(BF16), w_fv=seg(8, 9).astype(BF16),
        w_gate=w_gate.astype(F32), b_gate=b_gate,
        w_br_m=w_br_m.astype(BF16), w_br_f=w_br_f.astype(BF16), w_out=w_out.astype(BF16),
        norm_ffn=row(norm_ffn), wqt=peer_w_q.T.astype(F32),
        k1=peer_keys1.astype(F32), k2=peer_keys2.astype(F32),
        u_pk=_pack_table(peer_u), v_pk=_pack_table(peer_v),
        norm_ple=row(norm_ple), w_ple_gate=w_ple_gate.astype(BF16), w_ple_proj=w_ple_proj.astype(BF16),
        norm_final=row(norm_final),
    )


def _ct_rows(c):
    b, t, _ = c.shape
    return jnp.swapaxes(c[:, :, COL_FF:COL_FF + H_F], 1, 2).reshape(b, H_F // 2, 2, t)


def _peer_block(h, w):
    n = h.shape[0]
    xn, idx, gate = _peer_score(h, w["norm_ffn"], w["wqt"], w["k1"], w["k2"])
    steps = idx.reshape(n * STEPS_PER_TOKEN, SC_ROWS)
    wgt = _peer_weight(_sc_expert_dot(w["u_pk"], steps, xn), gate)
    return _sc_expert_sum(w["v_pk"], steps, wgt), wgt


def _layer(x, p, w, mstate, fox_cache):
    b, t, d = x.shape
    n = b * t
    h = x.reshape(n, d)
    g = w["norm_mix"]
    qkv = _norm_matmul(h, g, w["w_mqkv"], BF16, "proj_mlstm_qkv")
    og = _norm_matmul(h, g, w["w_og"], F32, "proj_gates")
    fq = _norm_matmul(h, g, w["w_fq"], BF16, "proj_fox_q")
    fk, fk_heads = _norm_matmul_heads(h, g, w["w_fk"], "proj_fox_k")
    fv, fv_heads = _norm_matmul_heads(h, g, w["w_fv"], "proj_fox_v")
    gates = _gates(h, g, w["w_gate"], w["b_gate"])

    c0, n0, m0 = mstate
    ym, c_new, n_new, m_new = _mlstm(
        qkv.reshape(b, t, 3 * W_M), og.reshape(b, t, 3 * D_MODEL), gates.reshape(b, t, GATE_COLS),
        c0.astype(F32), n0.astype(F32)[:, :, None, :],
        jnp.broadcast_to(m0.astype(F32)[:, :, None, None], (b, H_M, 1, LANES)))

    def finish(yf, first_token, after):
        hm = _merge(h, ym.reshape(n, W_M), yf, og, w["w_br_m"], w["w_br_f"], w["w_out"], first_token, after)
        peer, wgt = _peer_block(hm, w)
        y_seq = _ple_final(hm, peer, p.reshape(n, D_PLE), w["norm_ple"], w["norm_final"],
                           w["w_ple_gate"], w["w_ple_proj"], first_token)
        return y_seq, wgt

    gates3 = gates.reshape(b, t, GATE_COLS)
    if fox_cache is None:
        c = _cumsum_tokens(gates3)
        seqs = (fq.reshape(b, t, W_F), fk.reshape(b, t, W_F), fv.reshape(b, t, W_F))
        ys, wgts = [], []
        for bi in range(b):
            after = (wgts[-1] if bi >= 1 else w["norm_ffn"], ys[-2] if bi >= 2 else w["norm_ffn"])
            y_seq, wgt = finish(_fox_prompt(*seqs, c, bi), bi * t, after)
            ys.append(y_seq)
            wgts.append(wgt)
        y = jnp.concatenate(ys, axis=0)
    else:
        ck, cv, clf = fox_cache
        past = ck.shape[1]
        lf = jnp.pad(clf.astype(F32), ((0, 0), (0, 0), (COL_FF, GATE_COLS - COL_FF - H_F)))
        lf = jnp.concatenate([lf, gates3], axis=1)
        pad_t = (-lf.shape[1]) % 256
        c = _cumsum_tokens(jnp.pad(lf, ((0, 0), (0, pad_t), (0, 0))))
        ct = _ct_rows(c)
        keys_minor = lambda cache: jnp.transpose(cache.astype(F32), (0, 2, 3, 1))
        yf = _fox_sample(fq.reshape(b, t, W_F), keys_minor(ck), keys_minor(cv),
                         fk.reshape(b, t, W_F), fv.reshape(b, t, W_F),
                         c[:, past:past + t, :], ct[..., :past], ct[..., past:past + t])
        y, _ = finish(yf.reshape(n, W_F), 0, (w["norm_ffn"], w["norm_ffn"]))

    state = (fk_heads.reshape(1, b, t, H_F, DH_F), fv_heads.reshape(1, b, t, H_F, DH_F),
             gates3[None, :, :, COL_FF:COL_FF + H_F],
             c_new[None], n_new[None, :, :, 0, :], m_new[None, :, :, 0, 0])
    return y.reshape(b, t, d), state


def kernel(x_prompt, x_sample, p_prompt, p_sample, cache_fox_k, cache_fox_v, cache_fox_logf, state_mlstm_C, state_mlstm_n, state_mlstm_m, norm_mix, w_in, mlstm_b_i, mlstm_b_f, fox_b_f, w_br_m, w_br_f, w_out, norm_ffn, peer_w_q, peer_keys1, peer_keys2, peer_u, peer_v, norm_ple, w_ple_gate, w_ple_proj, norm_final):
    assert w_in.shape[0] == 1, "single-layer trunk"
    w = _prep_weights(norm_mix[0], w_in[0], mlstm_b_i[0], mlstm_b_f[0], fox_b_f[0], w_br_m[0], w_br_f[0],
                      w_out[0], norm_ffn[0], peer_w_q[0], peer_keys1[0], peer_keys2[0], peer_u[0], peer_v[0],
                      norm_ple[0], w_ple_gate[0], w_ple_proj[0], norm_final)
    bp = x_prompt.shape[0]
    zeros = (jnp.zeros((bp, H_M, DH_M, DH_M), F32), jnp.zeros((bp, H_M, DH_M), F32), jnp.zeros((bp, H_M), F32))
    y_p, sp = _layer(x_prompt, p_prompt[0], w, zeros, None)
    y_s, ss = _layer(x_sample, p_sample[0], w,
                     (state_mlstm_C[0], state_mlstm_n[0], state_mlstm_m[0]),
                     (cache_fox_k[0], cache_fox_v[0], cache_fox_logf[0]))
    return (y_p, y_s) + sp + ss
```

```python
import functools

import jax
import jax.numpy as jnp
from jax import lax
from jax.experimental import pallas as pl
from jax.experimental.pallas import tpu as pltpu
from jax.experimental.pallas import tpu_sc as plsc

D_MODEL = 1024
CHUNK = 64
H_M = 4
DH_M = 256
W_M = H_M * DH_M
H_F = 16
DH_F = 64
W_F = H_F * DH_F
D_PLE = 256
PEER_HEADS = 8
PEER_KEYS = 128
PEER_DQ = 256
PEER_TOPK = 16
PEER_SEL = PEER_HEADS * PEER_TOPK
EPS = 1e-6

LANES = 128
GATE_COLS = LANES
COL_I, COL_F, COL_FF = 0, H_M, 2 * H_M
VMEM_LIMIT = 56 * 1024 * 1024
HIGHEST = lax.Precision.HIGHEST
F32 = jnp.float32
BF16 = jnp.bfloat16
NEG_INF = float("-inf")


def _cparams(*sem):
    return pltpu.CompilerParams(dimension_semantics=sem, vmem_limit_bytes=VMEM_LIMIT)


ORDERED = pl.BlockSpec(memory_space=pl.ANY)


def _ordered_kernel(x_ref, after_a, after_b, o_ref):
    del after_a, after_b
    o_ref[...] = x_ref[...]


def _ordered(x, after_a, after_b):
    n, d = x.shape
    spec = pl.BlockSpec((n, d), lambda i: (0, 0))
    return pl.pallas_call(
        _ordered_kernel, grid=(1,), in_specs=[spec, ORDERED, ORDERED], out_specs=spec,
        out_shape=jax.ShapeDtypeStruct((n, d), x.dtype),
        compiler_params=_cparams("arbitrary"), name="ordered_copy",
    )(x, after_a, after_b)


def _rms(x, g):
    return x * lax.rsqrt(jnp.mean(x * x, axis=-1, keepdims=True) + EPS) * g


def _sigmoid(x):
    return 1.0 / (1.0 + jnp.exp(-x))


def _norm_matmul_kernel(x_ref, g_ref, w_ref, o_ref, a_ref):
    @pl.when(pl.program_id(1) == 0)
    def _():
        a_ref[...] = _rms(x_ref[...], g_ref[...]).astype(a_ref.dtype)

    o_ref[...] = jnp.dot(a_ref[...], w_ref[...], preferred_element_type=F32).astype(o_ref.dtype)


def _norm_matmul(x, g, w, out_dtype, name):
    n, d = x.shape
    cols = w.shape[1]
    tm = min(1024, n)
    tn = 1024
    return pl.pallas_call(
        _norm_matmul_kernel,
        grid=(n // tm, cols // tn),
        in_specs=[pl.BlockSpec((tm, d), lambda i, j: (i, 0)),
                  pl.BlockSpec((1, d), lambda i, j: (0, 0)),
                  pl.BlockSpec((d, tn), lambda i, j: (0, j))],
        out_specs=pl.BlockSpec((tm, tn), lambda i, j: (i, j)),
        out_shape=jax.ShapeDtypeStruct((n, cols), out_dtype),
        scratch_shapes=[pltpu.VMEM((tm, d), BF16)],
        compiler_params=_cparams("parallel", "arbitrary"),
        name=name,
    )(x, g, w)


def _norm_matmul_heads_kernel(x_ref, g_ref, w_ref, o_ref, oh_ref):
    a = _rms(x_ref[...], g_ref[...]).astype(BF16)
    z = jnp.dot(a, w_ref[...], preferred_element_type=F32)
    o_ref[...] = z.astype(o_ref.dtype)
    for h in range(H_F):
        oh_ref[:, h, :] = z[:, h * DH_F:(h + 1) * DH_F]


def _norm_matmul_heads(x, g, w, name):
    n, d = x.shape
    tm = min(512, n)
    return pl.pallas_call(
        _norm_matmul_heads_kernel,
        grid=(n // tm,),
        in_specs=[pl.BlockSpec((tm, d), lambda i: (i, 0)),
                  pl.BlockSpec((1, d), lambda i: (0, 0)),
                  pl.BlockSpec((d, W_F), lambda i: (0, 0))],
        out_specs=[pl.BlockSpec((tm, W_F), lambda i: (i, 0)),
                   pl.BlockSpec((tm, H_F, DH_F), lambda i: (i, 0, 0))],
        out_shape=[jax.ShapeDtypeStruct((n, W_F), BF16),
                   jax.ShapeDtypeStruct((n, H_F, DH_F), F32)],
        compiler_params=_cparams("parallel"),
        name=name,
    )(x, g, w)


def _gate_kernel(x_ref, g_ref, w_ref, b_ref, o_ref):
    a = _rms(x_ref[...], g_ref[...])
    z = jnp.dot(a, w_ref[...], precision=HIGHEST, preferred_element_type=F32) + b_ref[...]
    col = lax.broadcasted_iota(jnp.int32, z.shape, 1)
    log_sig = jnp.minimum(z, 0.0) - jnp.log1p(jnp.exp(-jnp.abs(z)))
    o_ref[...] = jnp.where(col < COL_F, z, log_sig)


def _gates(x, g, w, b):
    n, d = x.shape
    tm = min(512, n)
    return pl.pallas_call(
        _gate_kernel,
        grid=(n // tm,),
        in_specs=[pl.BlockSpec((tm, d), lambda i: (i, 0)),
                  pl.BlockSpec((1, d), lambda i: (0, 0)),
                  pl.BlockSpec((d, GATE_COLS), lambda i: (0, 0)),
                  pl.BlockSpec((1, GATE_COLS), lambda i: (0, 0))],
        out_specs=pl.BlockSpec((tm, GATE_COLS), lambda i: (i, 0)),
        out_shape=jax.ShapeDtypeStruct((n, GATE_COLS), F32),
        compiler_params=_cparams("parallel"),
        name="gates",
    )(x, g, w, b)


def _cumsum_kernel(x_ref, o_ref, carry_ref):
    @pl.when(pl.program_id(1) == 0)
    def _():
        carry_ref[...] = jnp.zeros_like(carry_ref)

    x = x_ref[...]
    tb = x.shape[0]
    row = lax.broadcasted_iota(jnp.int32, (tb, tb), 0)
    col = lax.broadcasted_iota(jnp.int32, (tb, tb), 1)
    tril = jnp.where(col <= row, 1.0, 0.0).astype(F32)
    c = jnp.dot(tril, x, precision=HIGHEST, preferred_element_type=F32) + carry_ref[...]
    o_ref[...] = c
    carry_ref[...] = c[tb - 1:tb, :]


def _cumsum_tokens(x):
    b, t, w = x.shape
    tb = 256
    return pl.pallas_call(
        _cumsum_kernel,
        grid=(b, t // tb),
        in_specs=[pl.BlockSpec((None, tb, w), lambda i, j: (i, j, 0))],
        out_specs=pl.BlockSpec((None, tb, w), lambda i, j: (i, j, 0)),
        out_shape=jax.ShapeDtypeStruct((b, t, w), F32),
        scratch_shapes=[pltpu.VMEM((1, w), F32)],
        compiler_params=_cparams("parallel", "arbitrary"),
        name="cumsum",
    )(x)


def _mlstm_kernel(qkv_ref, og_ref, g_ref, c0_ref, n0_ref, m0_ref,
                  y_ref, cn_ref, nn_ref, mn_ref, c_s, n_s, m_s, *, bb_n, blk):
    step = pl.program_id(1)

    @pl.when(step == 0)
    def _():
        c_s[...] = c0_ref[...]
        n_s[...] = n0_ref[...]
        m_s[...] = m0_ref[...]

    row = lax.broadcasted_iota(jnp.int32, (blk, blk), 0)
    col = lax.broadcasted_iota(jnp.int32, (blk, blk), 1)
    tril = col <= row
    triu = row <= col
    eye = col == row

    def to_row(x_col):
        return jnp.sum(jnp.where(eye, x_col, 0.0), axis=0, keepdims=True)

    for bb in range(bb_n):
        g = g_ref[bb]
        for h in range(H_M):
            q = qkv_ref[bb, :, h * DH_M:(h + 1) * DH_M]
            k = qkv_ref[bb, :, (H_M + h) * DH_M:(H_M + h + 1) * DH_M] * (DH_M ** -0.5)
            v = qkv_ref[bb, :, (2 * H_M + h) * DH_M:(2 * H_M + h + 1) * DH_M]
            i_col = g[:, COL_I + h:COL_I + h + 1]
            f_col = g[:, COL_F + h:COL_F + h + 1]
            i_row = to_row(i_col)
            f_row = to_row(f_col)
            b_col = jnp.sum(jnp.where(tril, f_row, 0.0), axis=1, keepdims=True)
            b_row = jnp.sum(jnp.where(triu, f_col, 0.0), axis=0, keepdims=True)
            m_prev = m_s[bb, h][:, 0:1]
            dmat = jnp.where(tril, b_col - b_row + i_row, NEG_INF)
            g_col = b_col + m_prev
            mt = jnp.maximum(g_col, jnp.max(dmat, axis=1, keepdims=True))
            w_d = jnp.exp(dmat - mt)
            w_g = jnp.exp(g_col - mt)
            qk = lax.dot_general(q, k, (((1,), (1,)), ((), ())), preferred_element_type=F32) * w_d
            c_prev = c_s[bb, h]
            n_prev = n_s[bb, h]
            num = (w_g * jnp.dot(q, c_prev.astype(BF16), preferred_element_type=F32)
                   + jnp.dot(qk.astype(BF16), v, preferred_element_type=F32))
            den = (w_g * jnp.sum(q.astype(F32) * n_prev, axis=1, keepdims=True)
                   + jnp.sum(qk, axis=1, keepdims=True))
            hid = num / jnp.maximum(jnp.abs(den), jnp.exp(-mt))
            o_gate = _sigmoid(og_ref[bb, :, h * DH_M:(h + 1) * DH_M])
            y_ref[bb, :, h * DH_M:(h + 1) * DH_M] = (o_gate * hid).astype(y_ref.dtype)
            m_new = mt[blk - 1:blk, :]
            b_last = b_col[blk - 1:blk, :]
            w_c = jnp.exp(b_last + m_prev - m_new)
            w_s = jnp.exp(b_last - b_col + i_col - m_new)
            kw = k.astype(F32) * w_s
            c_s[bb, h] = w_c * c_prev + lax.dot_general(
                kw.astype(BF16), v, (((0,), (0,)), ((), ())), preferred_element_type=F32)
            n_s[bb, h] = w_c * n_prev + jnp.sum(kw, axis=0, keepdims=True)
            m_s[bb, h] = jnp.broadcast_to(m_new, (1, LANES))

    @pl.when(step == pl.num_programs(1) - 1)
    def _():
        cn_ref[...] = c_s[...]
        nn_ref[...] = n_s[...]
        mn_ref[...] = m_s[...]


def _mlstm(qkv, ogate, gates, c0, n0, m0):
    b, t, _ = qkv.shape
    blk = min(CHUNK, t)
    bb_n = 4
    state_spec = lambda shape: pl.BlockSpec((bb_n,) + shape, lambda i, j: (i, 0, 0, 0))
    return pl.pallas_call(
        functools.partial(_mlstm_kernel, bb_n=bb_n, blk=blk),
        grid=(b // bb_n, t // blk),
        in_specs=[pl.BlockSpec((bb_n, blk, 3 * W_M), lambda i, j: (i, j, 0)),
                  pl.BlockSpec((bb_n, blk, W_M), lambda i, j: (i, j, 0)),
                  pl.BlockSpec((bb_n, blk, GATE_COLS), lambda i, j: (i, j, 0)),
                  state_spec((H_M, DH_M, DH_M)),
                  state_spec((H_M, 1, DH_M)),
                  state_spec((H_M, 1, LANES))],
        out_specs=[pl.BlockSpec((bb_n, blk, W_M), lambda i, j: (i, j, 0)),
                   state_spec((H_M, DH_M, DH_M)),
                   state_spec((H_M, 1, DH_M)),
                   state_spec((H_M, 1, LANES))],
        out_shape=[jax.ShapeDtypeStruct((b, t, W_M), BF16),
                   jax.ShapeDtypeStruct((b, H_M, DH_M, DH_M), F32),
                   jax.ShapeDtypeStruct((b, H_M, 1, DH_M), F32),
                   jax.ShapeDtypeStruct((b, H_M, 1, LANES), F32)],
        scratch_shapes=[pltpu.VMEM((bb_n, H_M, DH_M, DH_M), F32),
                        pltpu.VMEM((bb_n, H_M, 1, DH_M), F32),
                        pltpu.VMEM((bb_n, H_M, 1, LANES), F32)],
        compiler_params=_cparams("parallel", "arbitrary"),
        name="mlstm",
    )(qkv, ogate, gates, c0, n0, m0)


def _online_softmax_step(s, pv, m_ref, l_ref, acc_ref):
    m_old = m_ref[...]
    m_new = jnp.maximum(m_old, jnp.max(s, axis=1, keepdims=True))
    alpha = jnp.exp(m_old - m_new)
    p = jnp.exp(s - m_new)
    l_ref[...] = alpha * l_ref[...] + jnp.sum(p, axis=1, keepdims=True)
    acc_ref[...] = alpha * acc_ref[...] + pv(p.astype(BF16))
    m_ref[...] = m_new


def _split3(x):
    hi = x.astype(BF16).astype(F32)
    mid = (x - hi).astype(BF16).astype(F32)
    lo = (x - hi - mid).astype(BF16).astype(F32)
    return hi, mid, lo


FOX_V_ROWS = DH_F + 16
AUG_CQ = 6


def _fox_prompt_kernel(q_ref, k_ref, v_ref, c_ref, after_ref, o_ref, kaug_s, vt_s, m0_s, m1_s, acc0_s,
                       acc1_s, *, tq, n_blocks):
    del after_ref
    hp = pl.program_id(1)
    qi = pl.program_id(2)
    lane = lax.broadcasted_iota(jnp.int32, (tq, LANES), 1)
    m_s, acc_s = (m0_s, m1_s), (acc0_s, acc1_s)

    def head_col(c_blk, x):
        return jnp.sum(jnp.where(lane == COL_FF + 2 * hp + x, c_blk, 0.0), axis=1, keepdims=True)

    def aug_tile(entries):
        tile = jnp.zeros((tq, LANES), F32)
        for l, val in entries:
            tile = jnp.where(lane == l, val, tile)
        return tile.astype(BF16)

    @pl.when(qi == 0)
    def _():
        ones_rows = jnp.where(lax.broadcasted_iota(jnp.int32, (FOX_V_ROWS - DH_F, tq), 0) == 0, 1.0, 0.0)

        def chunk(i, carry):
            rs = pl.multiple_of(i * tq, tq)
            c_blk = c_ref[pl.ds(rs, tq), :]
            pieces = _split3(head_col(c_blk, 0)) + _split3(head_col(c_blk, 1))
            entries = list(enumerate(pieces)) + [(AUG_CQ + j, 1.0) for j in range(3)]
            kaug_s[pl.ds(rs, tq), 0:LANES] = k_ref[pl.ds(rs, tq), :]
            kaug_s[pl.ds(rs, tq), LANES:2 * LANES] = aug_tile(entries)
            vt = v_ref[pl.ds(rs, tq), :].astype(F32).T
            for x in range(2):
                vt_s[x, 0:DH_F, pl.ds(rs, tq)] = vt[x * DH_F:(x + 1) * DH_F, :].astype(BF16)
                vt_s[x, DH_F:FOX_V_ROWS, pl.ds(rs, tq)] = ones_rows.astype(BF16)
            return carry

        lax.fori_loop(0, n_blocks, chunk, 0)

    qs = pl.multiple_of(qi * tq, tq)
    c_q = c_ref[pl.ds(qs, tq), :]
    q2 = q_ref[...] * (DH_F ** -0.5)
    q_aug = []
    for x in range(2):
        cq3 = _split3(head_col(c_q, x))
        entries = [(3 * x + j, -1.0) for j in range(3)] + [(AUG_CQ + j, cq3[j]) for j in range(3)]
        q_head = jnp.where((lane < DH_F) == (x == 0), q2, jnp.zeros_like(q2))
        q_aug.append(jnp.concatenate([q_head, aug_tile(entries)], axis=1))
        m_s[x][...] = jnp.full(m_s[x].shape, NEG_INF, F32)
        acc_s[x][...] = jnp.zeros(acc_s[x].shape, F32)

    def block(kb, masked):
        ks = pl.multiple_of(kb * tq, tq)
        k_blk = kaug_s[pl.ds(ks, tq), :]
        scores = [lax.dot_general(k_blk, q_aug[x], (((1,), (1,)), ((), ())), preferred_element_type=F32)
                  for x in range(2)]
        for x in range(2):
            st = scores[x]
            if masked:
                k_pos = lax.broadcasted_iota(jnp.int32, (tq, tq), 0)
                q_pos = lax.broadcasted_iota(jnp.int32, (tq, tq), 1)
                st = jnp.where(k_pos <= q_pos, st, NEG_INF)
            m_old = m_s[x][...]
            m_new = jnp.maximum(m_old, jnp.max(st, axis=0, keepdims=True))
            p = jnp.exp(st - m_new).astype(BF16)
            acc_s[x][...] = (jnp.exp(m_old - m_new) * acc_s[x][...]
                             + jnp.dot(vt_s[x, :, pl.ds(ks, tq)], p, preferred_element_type=F32))
            m_s[x][...] = m_new

    def body(kb, carry):
        block(kb, False)
        return carry

    lax.fori_loop(0, qi, body, 0)
    block(qi, True)
    out_t = jnp.concatenate([acc_s[x][0:DH_F, :] / acc_s[x][DH_F:DH_F + 1, :] for x in range(2)], axis=0)
    o_ref[...] = out_t.T.astype(o_ref.dtype)


def _fox_prompt(q, k, v, c, batch, after):
    _, t, _ = q.shape
    tq = 512
    head_pair = lambda i, h, j: (batch, 0, h)
    return pl.pallas_call(
        functools.partial(_fox_prompt_kernel, tq=tq, n_blocks=t // tq),
        grid=(1, H_F // 2, t // tq),
        in_specs=[pl.BlockSpec((None, tq, LANES), lambda i, h, j: (batch, j, h)),
                  pl.BlockSpec((None, t, LANES), head_pair),
                  pl.BlockSpec((None, t, LANES), head_pair),
                  pl.BlockSpec((None, t, LANES), lambda i, h, j: (batch, 0, 0)),
                  ORDERED],
        out_specs=pl.BlockSpec((tq, LANES), lambda i, h, j: (j, h)),
        out_shape=jax.ShapeDtypeStruct((t, W_F), BF16),
        scratch_shapes=[pltpu.VMEM((t, 2 * LANES), BF16),
                        pltpu.VMEM((2, FOX_V_ROWS, t), BF16),
                        pltpu.VMEM((1, tq), F32), pltpu.VMEM((1, tq), F32),
                        pltpu.VMEM((FOX_V_ROWS, tq), F32), pltpu.VMEM((FOX_V_ROWS, tq), F32)],
        compiler_params=_cparams("parallel", "parallel", "arbitrary"),
        name="fox_prompt",
    )(q, k, v, c, after)


def _fox_sample_kernel(q_ref, kc_ref, vc_ref, kn_ref, vn_ref, cq_ref, ctc_ref, ctn_ref, o_ref,
                       m_s, l_s, acc_s, *, tn):
    kb = pl.program_id(1)

    @pl.when(kb == 0)
    def _():
        m_s[...] = jnp.full_like(m_s, NEG_INF)
        l_s[...] = jnp.zeros_like(l_s)
        acc_s[...] = jnp.zeros_like(acc_s)

    lane = lax.broadcasted_iota(jnp.int32, (tn, LANES), 1)
    c_blk = cq_ref[...]
    nt = (((1,), (1,)), ((), ()))

    def head_query(hd):
        q = q_ref[:, hd * DH_F:(hd + 1) * DH_F] * (DH_F ** -0.5)
        cq = jnp.sum(jnp.where(lane == COL_FF + hd, c_blk, 0.0), axis=1, keepdims=True)
        return q, cq

    for hd in range(H_F):
        q, cq = head_query(hd)
        k_t = kc_ref[hd].astype(BF16)
        v_t = vc_ref[hd].astype(BF16)
        s = jnp.dot(q, k_t, preferred_element_type=F32) + cq - ctc_ref[hd // 2, hd % 2:hd % 2 + 1, :]
        _online_softmax_step(s, lambda p: lax.dot_general(p, v_t, nt, preferred_element_type=F32),
                             m_s.at[hd], l_s.at[hd], acc_s.at[hd])

    @pl.when(kb == pl.num_programs(1) - 1)
    def _():
        q_pos = lax.broadcasted_iota(jnp.int32, (tn, tn), 0)
        k_pos = lax.broadcasted_iota(jnp.int32, (tn, tn), 1)
        for hd in range(H_F):
            q, cq = head_query(hd)
            k_new = kn_ref[:, hd * DH_F:(hd + 1) * DH_F]
            v_new = vn_ref[:, hd * DH_F:(hd + 1) * DH_F]
            s = (lax.dot_general(q, k_new, nt, preferred_element_type=F32)
                 + cq - ctn_ref[hd // 2, hd % 2:hd % 2 + 1, :])
            s = jnp.where(k_pos <= q_pos, s, NEG_INF)
            _online_softmax_step(s, lambda p: jnp.dot(p, v_new, preferred_element_type=F32),
                                 m_s.at[hd], l_s.at[hd], acc_s.at[hd])
            o_ref[:, hd * DH_F:(hd + 1) * DH_F] = (acc_s[hd] / l_s[hd]).astype(o_ref.dtype)


def _fox_sample(q, k_cache_t, v_cache_t, k_new, v_new, cq, ct_cache, ct_new):
    b, tn, _ = q.shape
    p = k_cache_t.shape[3]
    tk = 1024
    new = pl.BlockSpec((None, tn, W_F), lambda i, j: (i, 0, 0))
    cache = pl.BlockSpec((None, H_F, DH_F, tk), lambda i, j: (i, 0, 0, j))
    return pl.pallas_call(
        functools.partial(_fox_sample_kernel, tn=tn),
        grid=(b, p // tk),
        in_specs=[new, cache, cache, new, new,
                  pl.BlockSpec((None, tn, LANES), lambda i, j: (i, 0, 0)),
                  pl.BlockSpec((None, H_F // 2, 2, tk), lambda i, j: (i, 0, 0, j)),
                  pl.BlockSpec((None, H_F // 2, 2, tn), lambda i, j: (i, 0, 0, 0))],
        out_specs=new,
        out_shape=jax.ShapeDtypeStruct((b, tn, W_F), BF16),
        scratch_shapes=[pltpu.VMEM((H_F, tn, 1), F32),
                        pltpu.VMEM((H_F, tn, 1), F32),
                        pltpu.VMEM((H_F, tn, DH_F), F32)],
        compiler_params=_cparams("parallel", "arbitrary"),
        name="fox_sample",
    )(q, k_cache_t, v_cache_t, k_new, v_new, cq, ct_cache, ct_new)


def _merge_kernel(h_ref, ym_ref, yf_ref, gm_ref, gf_ref, wm_ref, wf_ref, wo_ref, after_ref, o_ref):
    del after_ref
    a = jnp.dot(ym_ref[...], wm_ref[...], preferred_element_type=F32)
    b = jnp.dot(yf_ref[...], wf_ref[...], preferred_element_type=F32)
    merge = _sigmoid(gm_ref[...]) * a + _sigmoid(gf_ref[...]) * b
    o_ref[...] = h_ref[...] + jnp.dot(merge.astype(BF16), wo_ref[...], preferred_element_type=F32)


def _merge(h, ym, yf, gates, wm, wf, wo, first_token, after):
    n, d = yf.shape
    tm = min(512, n)
    first = first_token // tm
    tok = lambda c: pl.BlockSpec((tm, d), lambda i: (first + i, c))
    local = pl.BlockSpec((tm, d), lambda i: (i, 0))
    wspec = pl.BlockSpec((d, d), lambda i: (0, 0))
    return pl.pallas_call(
        _merge_kernel,
        grid=(n // tm,),
        in_specs=[tok(0), tok(0), local, tok(1), tok(2), wspec, wspec, wspec, ORDERED],
        out_specs=local,
        out_shape=jax.ShapeDtypeStruct((n, d), F32),
        compiler_params=_cparams("parallel"),
        name="merge",
    )(h, ym, yf, gates, gates, wm, wf, wo, after)


def _top_rows(s, count, ids=None, payload=None):
    if ids is None:
        ids = lax.broadcasted_iota(jnp.int32, s.shape, 0)
    big = jnp.int32(2 ** 30)
    vals, sel, pay = [], [], []
    for _ in range(count):
        m = jnp.max(s, axis=0, keepdims=True)
        am = jnp.min(jnp.where(s == m, ids, big), axis=0, keepdims=True)
        hit = ids == am
        vals.append(m)
        sel.append(am)
        if payload is not None:
            pay.append(jnp.max(jnp.where(hit, payload, -1), axis=0, keepdims=True))
        s = jnp.where(hit, NEG_INF, s)
    cat = lambda xs: jnp.concatenate(xs, axis=0)
    return cat(vals), cat(sel), (cat(pay) if payload is not None else None)


def _pair_candidates(v1, i1, v2, i2):
    t = v1.shape[1]
    half = PEER_TOPK // 2
    r16 = lax.broadcasted_iota(jnp.int32, (PEER_TOPK, t), 0)
    r8 = lax.broadcasted_iota(jnp.int32, (half, t), 0)
    sums = [v1[0:1, :] + v2]
    flat = [r16]
    expert = [i1[0:1, :] * PEER_KEYS + i2]
    for a in range(1, half):
        sums.append(v1[a:a + 1, :] + v2[0:half, :])
        flat.append(r8 + a * PEER_TOPK)
        expert.append(i1[a:a + 1, :] * PEER_KEYS + i2[0:half, :])
    sums.append(v1[half:, :] + v2[0:1, :])
    flat.append((r8 + half) * PEER_TOPK)
    expert.append(i1[half:, :] * PEER_KEYS + i2[0:1, :])
    cat = lambda xs: jnp.concatenate(xs, axis=0)
    return cat(sums), cat(flat), cat(expert)


def _peer_score_kernel(h_ref, g_ref, wqt_ref, k1_ref, k2_ref, xn_ref, idx_ref, gate_ref):
    xn = _rms(h_ref[...], g_ref[...])
    xn_ref[...] = xn.astype(xn_ref.dtype)
    qt = lax.dot_general(wqt_ref[...], xn, (((1,), (1,)), ((), ())),
                         precision=HIGHEST, preferred_element_type=F32)
    half = PEER_DQ // 2
    idx_rows, gate_rows = [], []
    for hd in range(PEER_HEADS):
        q1 = qt[hd * PEER_DQ:hd * PEER_DQ + half, :]
        q2 = qt[hd * PEER_DQ + half:(hd + 1) * PEER_DQ, :]
        s1 = jnp.dot(k1_ref[hd], q1, precision=HIGHEST, preferred_element_type=F32)
        s2 = jnp.dot(k2_ref[hd], q2, precision=HIGHEST, preferred_element_type=F32)
        v1, i1, _ = _top_rows(s1, PEER_TOPK)
        v2, i2, _ = _top_rows(s2, PEER_TOPK)
        cand, flat, expert = _pair_candidates(v1, i1, v2, i2)
        sc, _, ex = _top_rows(cand, PEER_TOPK, ids=flat, payload=expert)
        e = jnp.exp(sc - sc[0:1, :])
        gate_rows.append(e / jnp.sum(e, axis=0, keepdims=True))
        idx_rows.append(ex)
    gate_ref[...] = jnp.concatenate(gate_rows, axis=0).T
    idx_ref[...] = jnp.concatenate(idx_rows, axis=0).T


def _peer_score(h, g, wqt, k1, k2):
    n, d = h.shape
    tt = min(256, n)
    return pl.pallas_call(
        _peer_score_kernel,
        grid=(n // tt,),
        in_specs=[pl.BlockSpec((tt, d), lambda i: (i, 0)),
                  pl.BlockSpec((1, d), lambda i: (0, 0)),
                  pl.BlockSpec(wqt.shape, lambda i: (0, 0)),
                  pl.BlockSpec(k1.shape, lambda i: (0, 0, 0)),
                  pl.BlockSpec(k2.shape, lambda i: (0, 0, 0))],
        out_specs=[pl.BlockSpec((tt, d), lambda i: (i, 0)),
                   pl.BlockSpec((tt, PEER_SEL), lambda i: (i, 0)),
                   pl.BlockSpec((tt, PEER_SEL), lambda i: (i, 0))],
        out_shape=[jax.ShapeDtypeStruct((n, d), F32),
                   jax.ShapeDtypeStruct((n, PEER_SEL), jnp.int32),
                   jax.ShapeDtypeStruct((n, PEER_SEL), F32)],
        compiler_params=_cparams("parallel"),
        name="peer_score",
    )(h, g, wqt, k1, k2)


SC_CORES = 2
SC_SUBCORES = 16
SC_WORKERS = SC_CORES * SC_SUBCORES
SC_ROWS = 64


def _pack_table(tab):
    half = tab.shape[1] // 2
    bits = lax.bitcast_convert_type(tab.astype(BF16), jnp.uint16).astype(jnp.uint32)
    word = bits[:, :half] | (bits[:, half:] << 16)
    return lax.bitcast_convert_type(word, jnp.int32)


SC_TOK = 8
SC_LANES = 16
STEPS_PER_TOKEN = PEER_SEL // SC_ROWS
SC_PARAMS = pltpu.CompilerParams(needs_layout_passes=False)


def _row_source(table_hbm, idx_v, local_step, global_step):
    del global_step
    return table_hbm.at[idx_v.at[local_step]]


def _sc_unpack(wd):
    lo = lax.bitcast_convert_type(wd << 16, F32)
    hi = lax.bitcast_convert_type(wd & jnp.int32(-65536), F32)
    return lo, hi


def _sc_token_blocks(n, body_block):
    per_worker = n // SC_WORKERS
    assert per_worker * SC_WORKERS == n and per_worker % SC_TOK == 0
    wid = lax.axis_index("s") * SC_CORES + lax.axis_index("c")

    @pl.loop(0, per_worker // SC_TOK)
    def _(blk):
        body_block(wid * per_worker + blk * SC_TOK)


def _sc_pipelined_steps(table_hbm, idx_v, rows_v, sems, first_step, consume):
    n_steps = SC_TOK * STEPS_PER_TOKEN

    def row_gather(j, slot):
        return pltpu.make_async_copy(_row_source(table_hbm, idx_v, j, first_step + j),
                                     rows_v.at[slot], sems.at[slot])

    row_gather(0, 0).start()

    @pl.loop(0, n_steps, step=2)
    def _(i):
        for slot in range(2):
            j = i + slot

            @pl.when(j + 1 < n_steps)
            def _():
                row_gather(j + 1, 1 - slot).start()

            row_gather(j, slot).wait()
            consume(slot, i // 2, slot)


def _sc_expert_dot(table, idx_steps, x):
    n, d = x.shape
    w = d // 2
    assert STEPS_PER_TOKEN == 2 and table.shape[1] == w
    mesh = plsc.VectorSubcoreMesh(core_axis_name="c", subcore_axis_name="s")
    n_steps = SC_TOK * STEPS_PER_TOKEN
    group = 4

    @functools.partial(
        pl.kernel, mesh=mesh,
        out_type=jax.ShapeDtypeStruct((n, PEER_SEL), F32),
        scratch_types=[pltpu.VMEM((n_steps, SC_ROWS), jnp.int32),
                       pltpu.VMEM((SC_TOK, d), F32),
                       pltpu.VMEM((2, SC_ROWS, w), jnp.int32),
                       pltpu.VMEM((SC_TOK, PEER_SEL), F32),
                       pltpu.SemaphoreType.DMA((2,))],
        compiler_params=SC_PARAMS,
    )
    def expert_dot(table_hbm, idx_hbm, x_hbm, act_hbm, idx_v, x_v, rows_v, act_v, sems):
        lanes = lax.iota(jnp.int32, SC_LANES)
        zero = jnp.zeros((SC_LANES,), F32)

        def consume(slot, tl, half):
            @pl.loop(0, SC_ROWS // SC_LANES)
            def _(g):
                act_vec = zero
                for q in range(SC_LANES // group):
                    r0 = g * SC_LANES + q * group

                    def chunk(j, accs):
                        c = pl.multiple_of(j * SC_LANES, SC_LANES)
                        x_lo = x_v[tl, pl.ds(c, SC_LANES)]
                        x_hi = x_v[tl, pl.ds(w + c, SC_LANES)]
                        out = []
                        for rr in range(group):
                            lo, hi = _sc_unpack(rows_v[slot, r0 + rr, pl.ds(c, SC_LANES)])
                            out.append(accs[rr] + lo * x_lo + hi * x_hi)
                        return tuple(out)

                    accs = lax.fori_loop(0, w // SC_LANES, chunk, (zero,) * group, unroll=2)
                    for rr in range(group):
                        act_vec = jnp.where(lanes == q * group + rr, jnp.sum(accs[rr]), act_vec)
                act_v[tl, pl.ds(half * SC_ROWS + g * SC_LANES, SC_LANES)] = act_vec

        def block(t0):
            s0 = t0 * STEPS_PER_TOKEN
            pltpu.sync_copy(idx_hbm.at[pl.ds(s0, n_steps)], idx_v)
            pltpu.sync_copy(x_hbm.at[pl.ds(t0, SC_TOK)], x_v)
            _sc_pipelined_steps(table_hbm, idx_v, rows_v, sems, s0, consume)
            pltpu.sync_copy(act_v, act_hbm.at[pl.ds(t0, SC_TOK)])

        _sc_token_blocks(n, block)

    return expert_dot(table, idx_steps, x)


def _sc_expert_sum(table, idx_steps, wgt):
    n = wgt.shape[0]
    w = table.shape[1]
    d = 2 * w
    assert STEPS_PER_TOKEN == 2
    mesh = plsc.VectorSubcoreMesh(core_axis_name="c", subcore_axis_name="s")
    n_steps = SC_TOK * STEPS_PER_TOKEN
    cols = 8

    @functools.partial(
        pl.kernel, mesh=mesh,
        out_type=jax.ShapeDtypeStruct((n, d), F32),
        scratch_types=[pltpu.VMEM((n_steps, SC_ROWS), jnp.int32),
                       pltpu.VMEM((SC_TOK, PEER_SEL), F32),
                       pltpu.VMEM((2, SC_ROWS, w), jnp.int32),
                       pltpu.VMEM((SC_TOK, d), F32),
                       pltpu.VMEM((SC_ROWS, SC_LANES), F32),
                       pltpu.SemaphoreType.DMA((2,))],
        compiler_params=SC_PARAMS,
    )
    def expert_sum(table_hbm, idx_hbm, wgt_hbm, out_hbm, idx_v, wgt_v, rows_v, out_v, splat_v, sems):
        zero = jnp.zeros((SC_LANES,), F32)
        lanes = lax.iota(jnp.int32, SC_LANES)

        def consume(slot, tl, half):
            @pl.loop(0, SC_ROWS // SC_LANES)
            def _(g):
                w16 = wgt_v[tl, pl.ds(half * SC_ROWS + g * SC_LANES, SC_LANES)]
                for rr in range(SC_LANES):
                    one = jnp.sum(jnp.where(lanes == rr, w16, 0.0))
                    splat_v[g * SC_LANES + rr, :] = jnp.full((SC_LANES,), one, F32)

            for cb in range(w // (cols * SC_LANES)):
                base = cb * cols * SC_LANES

                def row(r, accs):
                    wv = splat_v[r, :]
                    out = []
                    for jj in range(cols):
                        lo, hi = _sc_unpack(rows_v[slot, r, pl.ds(base + jj * SC_LANES, SC_LANES)])
                        out.append(accs[2 * jj] + wv * lo)
                        out.append(accs[2 * jj + 1] + wv * hi)
                    return tuple(out)

                accs = lax.fori_loop(0, SC_ROWS, row, (zero,) * (2 * cols), unroll=2)
                for jj in range(cols):
                    c = base + jj * SC_LANES
                    if half == 0:
                        out_v[tl, pl.ds(c, SC_LANES)] = accs[2 * jj]
                        out_v[tl, pl.ds(w + c, SC_LANES)] = accs[2 * jj + 1]
                    else:
                        out_v[tl, pl.ds(c, SC_LANES)] = out_v[tl, pl.ds(c, SC_LANES)] + accs[2 * jj]
                        out_v[tl, pl.ds(w + c, SC_LANES)] = out_v[tl, pl.ds(w + c, SC_LANES)] + accs[2 * jj + 1]

        def block(t0):
            s0 = t0 * STEPS_PER_TOKEN
            pltpu.sync_copy(idx_hbm.at[pl.ds(s0, n_steps)], idx_v)
            pltpu.sync_copy(wgt_hbm.at[pl.ds(t0, SC_TOK)], wgt_v)
            _sc_pipelined_steps(table_hbm, idx_v, rows_v, sems, s0, consume)
            pltpu.sync_copy(out_v, out_hbm.at[pl.ds(t0, SC_TOK)])

        _sc_token_blocks(n, block)

    return expert_sum(table, idx_steps, wgt)


def _gelu_exact(x):
    return 0.5 * x * (1.0 + lax.erf(x * (2.0 ** -0.5)))


def _peer_weight_kernel(act_ref, gate_ref, after_ref, o_ref):
    del after_ref
    o_ref[...] = gate_ref[...] * _gelu_exact(act_ref[...])


def _peer_weight(act, gate, after):
    n = act.shape[0]
    tt = min(2048, n)
    spec = pl.BlockSpec((tt, PEER_SEL), lambda i: (i, 0))
    return pl.pallas_call(
        _peer_weight_kernel,
        grid=(n // tt,),
        in_specs=[spec, spec, ORDERED],
        out_specs=spec,
        out_shape=jax.ShapeDtypeStruct((n, PEER_SEL), F32),
        compiler_params=_cparams("parallel"),
        name="peer_weight",
    )(act, gate, after)


def _ple_kernel(h_ref, peer_ref, p_ref, gp_ref, gfin_ref, wg_ref, wp_ref, o_ref):
    h = h_ref[...] + peer_ref[...]
    e = _rms(h, gp_ref[...]).astype(BF16)
    gate = _sigmoid(jnp.dot(e, wg_ref[...], preferred_element_type=F32))
    proj = jnp.dot(p_ref[...].astype(BF16), wp_ref[...], preferred_element_type=F32)
    o_ref[...] = _rms(h + gate * proj, gfin_ref[...])


def _ple_final(h, peer, p, g_ple, g_final, wg, wp, first_token):
    n, d = h.shape
    tm = min(512, n)
    first = first_token // tm
    return pl.pallas_call(
        _ple_kernel,
        grid=(n // tm,),
        in_specs=[pl.BlockSpec((tm, d), lambda i: (i, 0)),
                  pl.BlockSpec((tm, d), lambda i: (i, 0)),
                  pl.BlockSpec((tm, D_PLE), lambda i: (first + i, 0)),
                  pl.BlockSpec((1, d), lambda i: (0, 0)),
                  pl.BlockSpec((1, d), lambda i: (0, 0)),
                  pl.BlockSpec((d, d), lambda i: (0, 0)),
                  pl.BlockSpec((D_PLE, d), lambda i: (0, 0))],
        out_specs=pl.BlockSpec((tm, d), lambda i: (i, 0)),
        out_shape=jax.ShapeDtypeStruct((n, d), F32),
        compiler_params=_cparams("parallel"),
        name="ple_final",
    )(h, peer, p, g_ple, g_final, wg, wp)


def _prep_weights(norm_mix, w_in, mlstm_b_i, mlstm_b_f, fox_b_f, w_br_m, w_br_f, w_out, norm_ffn,
                  peer_w_q, peer_keys1, peer_keys2, peer_u, peer_v, norm_ple, w_ple_gate, w_ple_proj,
                  norm_final):
    o = [0]
    for s in (W_M, W_M, W_M, W_M, H_M, H_M, W_F, W_F, W_F, H_F, D_MODEL, D_MODEL):
        o.append(o[-1] + s)
    seg = lambda a, b: w_in[:, o[a]:o[b]]
    w_gate = jnp.concatenate([seg(4, 6), seg(9, 10)], axis=1)
    w_gate = jnp.pad(w_gate, ((0, 0), (0, GATE_COLS - w_gate.shape[1])))
    b_gate = jnp.concatenate([mlstm_b_i, mlstm_b_f, fox_b_f]).astype(F32)
    b_gate = jnp.pad(b_gate, (0, GATE_COLS - b_gate.shape[0]))[None, :]
    row = lambda v: v.astype(F32)[None, :]
    return dict(
        norm_mix=row(norm_mix),
        w_mqkv=seg(0, 3).astype(BF16),
        w_og=jnp.concatenate([seg(3, 4), seg(10, 12)], axis=1).astype(BF16),
        w_fq=seg(6, 7).astype(BF16), w_fk=seg(7, 8).astype(BF16), w_fv=seg(8, 9).astype(BF16),
        w_gate=w_gate.astype(F32), b_gate=b_gate,
        w_br_m=w_br_m.astype(BF16), w_br_f=w_br_f.astype(BF16), w_out=w_out.astype(BF16),
        norm_ffn=row(norm_ffn), wqt=peer_w_q.T.astype(F32),
        k1=peer_keys1.astype(F32), k2=peer_keys2.astype(F32),
        u_pk=_pack_table(peer_u), v_pk=_pack_table(peer_v),
        norm_ple=row(norm_ple), w_ple_gate=w_ple_gate.astype(BF16), w_ple_proj=w_ple_proj.astype(BF16),
        norm_final=row(norm_final),
    )


def _ct_rows(c):
    b, t, _ = c.shape
    return jnp.swapaxes(c[:, :, COL_FF:COL_FF + H_F], 1, 2).reshape(b, H_F // 2, 2, t)


def _peer_first_pass(h, w):
    n = h.shape[0]
    xn, idx, gate = _peer_score(h, w["norm_ffn"], w["wqt"], w["k1"], w["k2"])
    steps = idx.reshape(n * STEPS_PER_TOKEN, SC_ROWS)
    return steps, gate, _sc_expert_dot(w["u_pk"], steps, xn)


def _peer_second_pass(first, w, after):
    steps, gate, act = first
    wgt = _peer_weight(act, gate, after)
    return _sc_expert_sum(w["v_pk"], steps, wgt), wgt


def _layer(x, p, w, mstate, fox_cache):
    b, t, d = x.shape
    n = b * t
    h = x.reshape(n, d)
    g = w["norm_mix"]
    qkv = _norm_matmul(h, g, w["w_mqkv"], BF16, "proj_mlstm_qkv")
    og = _norm_matmul(h, g, w["w_og"], F32, "proj_gates")
    fq = _norm_matmul(h, g, w["w_fq"], BF16, "proj_fox_q")
    fk, fk_heads = _norm_matmul_heads(h, g, w["w_fk"], "proj_fox_k")
    fv, fv_heads = _norm_matmul_heads(h, g, w["w_fv"], "proj_fox_v")
    gates = _gates(h, g, w["w_gate"], w["b_gate"])

    c0, n0, m0 = mstate
    ym, c_new, n_new, m_new = _mlstm(
        qkv.reshape(b, t, 3 * W_M), og.reshape(b, t, 3 * D_MODEL), gates.reshape(b, t, GATE_COLS),
        c0.astype(F32), n0.astype(F32)[:, :, None, :],
        jnp.broadcast_to(m0.astype(F32)[:, :, None, None], (b, H_M, 1, LANES)))

    def merged(yf, first_token, after):
        return _merge(h, ym.reshape(n, W_M), yf, og, w["w_br_m"], w["w_br_f"], w["w_out"], first_token, after)

    def output(hm, peer, first_token):
        return _ple_final(hm, peer, p.reshape(n, D_PLE), w["norm_ple"], w["norm_final"],
                          w["w_ple_gate"], w["w_ple_proj"], first_token)

    neutral = w["norm_ffn"]
    gates3 = gates.reshape(b, t, GATE_COLS)
    if fox_cache is None:
        c = _cumsum_tokens(gates3)
        seqs = (fq.reshape(b, t, W_F), fk.reshape(b, t, W_F), fv.reshape(b, t, W_F))
        hms, firsts, peers, wgts = [], [], [], []
        for bi in range(b):
            yf = _fox_prompt(*seqs, c, bi, wgts[bi - 2] if bi >= 2 else neutral)
            hms.append(merged(yf, bi * t, peers[bi - 3] if bi >= 3 else neutral))
            firsts.append(_peer_first_pass(hms[bi], w))
            if bi >= 1:
                peer, wgt = _peer_second_pass(firsts[bi - 1], w, firsts[bi][1])
                peers.append(peer)
                wgts.append(wgt)
        peer, wgt = _peer_second_pass(firsts[b - 1], w, neutral)
        peers.append(peer)
        wgts.append(wgt)
        y = jnp.concatenate([output(hms[bi], peers[bi], bi * t) for bi in range(b)], axis=0)
    else:
        ck, cv, clf = fox_cache
        past = ck.shape[1]
        lf = jnp.pad(clf.astype(F32), ((0, 0), (0, 0), (COL_FF, GATE_COLS - COL_FF - H_F)))
        lf = jnp.concatenate([lf, gates3], axis=1)
        pad_t = (-lf.shape[1]) % 256
        c = _cumsum_tokens(jnp.pad(lf, ((0, 0), (0, pad_t), (0, 0))))
        ct = _ct_rows(c)
        keys_minor = lambda cache: jnp.transpose(cache.astype(F32), (0, 2, 3, 1))
        yf = _fox_sample(fq.reshape(b, t, W_F), keys_minor(ck), keys_minor(cv),
                         fk.reshape(b, t, W_F), fv.reshape(b, t, W_F),
                         c[:, past:past + t, :], ct[..., :past], ct[..., past:past + t])
        hm = merged(yf.reshape(n, W_F), 0, neutral)
        peer, wgt = _peer_second_pass(_peer_first_pass(hm, w), w, neutral)
        peers, wgts = [peer], [wgt]
        y = output(hm, peer, 0)

    state = (fk_heads.reshape(1, b, t, H_F, DH_F), fv_heads.reshape(1, b, t, H_F, DH_F),
             gates3[None, :, :, COL_FF:COL_FF + H_F],
             c_new[None], n_new[None, :, :, 0, :], m_new[None, :, :, 0, 0])
    return y.reshape(b, t, d), state, (peers, wgts)


def kernel(x_prompt, x_sample, p_prompt, p_sample, cache_fox_k, cache_fox_v, cache_fox_logf, state_mlstm_C, state_mlstm_n, state_mlstm_m, norm_mix, w_in, mlstm_b_i, mlstm_b_f, fox_b_f, w_br_m, w_br_f, w_out, norm_ffn, peer_w_q, peer_keys1, peer_keys2, peer_u, peer_v, norm_ple, w_ple_gate, w_ple_proj, norm_final):
    assert w_in.shape[0] == 1, "single-layer trunk"
    w = _prep_weights(norm_mix[0], w_in[0], mlstm_b_i[0], mlstm_b_f[0], fox_b_f[0], w_br_m[0], w_br_f[0],
                      w_out[0], norm_ffn[0], peer_w_q[0], peer_keys1[0], peer_keys2[0], peer_u[0], peer_v[0],
                      norm_ple[0], w_ple_gate[0], w_ple_proj[0], norm_final)
    bp = x_prompt.shape[0]
    zeros = (jnp.zeros((bp, H_M, DH_M, DH_M), F32), jnp.zeros((bp, H_M, DH_M), F32), jnp.zeros((bp, H_M), F32))
    y_p, sp, (peers, wgts) = _layer(x_prompt, p_prompt[0], w, zeros, None)
    bs, ts, d = x_sample.shape
    x_s = _ordered(x_sample.reshape(bs * ts, d), peers[-3], wgts[-2]).reshape(bs, ts, d)
    y_s, ss, _ = _layer(x_s, p_sample[0], w,
                        (state_mlstm_C[0], state_mlstm_n[0], state_mlstm_m[0]),
                        (cache_fox_k[0], cache_fox_v[0], cache_fox_logf[0]))
    return (y_p, y_s) + sp + ss
```

```python
import functools

import jax
import jax.numpy as jnp
from jax import lax
from jax.experimental import pallas as pl
from jax.experimental.pallas import tpu as pltpu
from jax.experimental.pallas import tpu_sc as plsc

D_MODEL = 1024
CHUNK = 64
H_M = 4
DH_M = 256
W_M = H_M * DH_M
H_F = 16
DH_F = 64
W_F = H_F * DH_F
D_PLE = 256
PEER_HEADS = 8
PEER_KEYS = 128
PEER_DQ = 256
PEER_TOPK = 16
PEER_SEL = PEER_HEADS * PEER_TOPK
EPS = 1e-6

LANES = 128
GATE_COLS = LANES
COL_I, COL_F, COL_FF = 0, H_M, 2 * H_M
VMEM_LIMIT = 56 * 1024 * 1024
HIGHEST = lax.Precision.HIGHEST
F32 = jnp.float32
BF16 = jnp.bfloat16
NEG_INF = float("-inf")


def _cparams(*sem):
    return pltpu.CompilerParams(dimension_semantics=sem, vmem_limit_bytes=VMEM_LIMIT)


ORDERED = pl.BlockSpec(memory_space=pl.ANY)


def _ordered_kernel(x_ref, after_a, after_b, o_ref):
    del after_a, after_b
    o_ref[...] = x_ref[...]


def _ordered(x, after_a, after_b):
    n, d = x.shape
    spec = pl.BlockSpec((n, d), lambda i: (0, 0))
    return pl.pallas_call(
        _ordered_kernel, grid=(1,), in_specs=[spec, ORDERED, ORDERED], out_specs=spec,
        out_shape=jax.ShapeDtypeStruct((n, d), x.dtype),
        compiler_params=_cparams("arbitrary"), name="ordered_copy",
    )(x, after_a, after_b)


def _rms(x, g):
    return x * lax.rsqrt(jnp.mean(x * x, axis=-1, keepdims=True) + EPS) * g


def _sigmoid(x):
    return 1.0 / (1.0 + jnp.exp(-x))


def _norm_matmul_kernel(x_ref, g_ref, w_ref, o_ref, a_ref):
    @pl.when(pl.program_id(1) == 0)
    def _():
        a_ref[...] = _rms(x_ref[...], g_ref[...]).astype(a_ref.dtype)

    o_ref[...] = jnp.dot(a_ref[...], w_ref[...], preferred_element_type=F32).astype(o_ref.dtype)


def _norm_matmul(x, g, w, out_dtype, name):
    n, d = x.shape
    cols = w.shape[1]
    tm = min(1024, n)
    tn = 1024
    return pl.pallas_call(
        _norm_matmul_kernel,
        grid=(n // tm, cols // tn),
        in_specs=[pl.BlockSpec((tm, d), lambda i, j: (i, 0)),
                  pl.BlockSpec((1, d), lambda i, j: (0, 0)),
                  pl.BlockSpec((d, tn), lambda i, j: (0, j))],
        out_specs=pl.BlockSpec((tm, tn), lambda i, j: (i, j)),
        out_shape=jax.ShapeDtypeStruct((n, cols), out_dtype),
        scratch_shapes=[pltpu.VMEM((tm, d), BF16)],
        compiler_params=_cparams("parallel", "arbitrary"),
        name=name,
    )(x, g, w)


def _norm_matmul_heads_kernel(x_ref, g_ref, w_ref, o_ref, oh_ref):
    a = _rms(x_ref[...], g_ref[...]).astype(BF16)
    z = jnp.dot(a, w_ref[...], preferred_element_type=F32)
    o_ref[...] = z.astype(o_ref.dtype)
    for h in range(H_F):
        oh_ref[:, h, :] = z[:, h * DH_F:(h + 1) * DH_F]


def _norm_matmul_heads(x, g, w, name):
    n, d = x.shape
    tm = min(512, n)
    return pl.pallas_call(
        _norm_matmul_heads_kernel,
        grid=(n // tm,),
        in_specs=[pl.BlockSpec((tm, d), lambda i: (i, 0)),
                  pl.BlockSpec((1, d), lambda i: (0, 0)),
                  pl.BlockSpec((d, W_F), lambda i: (0, 0))],
        out_specs=[pl.BlockSpec((tm, W_F), lambda i: (i, 0)),
                   pl.BlockSpec((tm, H_F, DH_F), lambda i: (i, 0, 0))],
        out_shape=[jax.ShapeDtypeStruct((n, W_F), BF16),
                   jax.ShapeDtypeStruct((n, H_F, DH_F), F32)],
        compiler_params=_cparams("parallel"),
        name=name,
    )(x, g, w)


def _gate_kernel(x_ref, g_ref, w_ref, b_ref, o_ref):
    a = _rms(x_ref[...], g_ref[...])
    z = jnp.dot(a, w_ref[...], precision=HIGHEST, preferred_element_type=F32) + b_ref[...]
    col = lax.broadcasted_iota(jnp.int32, z.shape, 1)
    log_sig = jnp.minimum(z, 0.0) - jnp.log1p(jnp.exp(-jnp.abs(z)))
    o_ref[...] = jnp.where(col < COL_F, z, log_sig)


def _gates(x, g, w, b):
    n, d = x.shape
    tm = min(512, n)
    return pl.pallas_call(
        _gate_kernel,
        grid=(n // tm,),
        in_specs=[pl.BlockSpec((tm, d), lambda i: (i, 0)),
                  pl.BlockSpec((1, d), lambda i: (0, 0)),
                  pl.BlockSpec((d, GATE_COLS), lambda i: (0, 0)),
                  pl.BlockSpec((1, GATE_COLS), lambda i: (0, 0))],
        out_specs=pl.BlockSpec((tm, GATE_COLS), lambda i: (i, 0)),
        out_shape=jax.ShapeDtypeStruct((n, GATE_COLS), F32),
        compiler_params=_cparams("parallel"),
        name="gates",
    )(x, g, w, b)


def _cumsum_kernel(x_ref, o_ref, carry_ref):
    @pl.when(pl.program_id(1) == 0)
    def _():
        carry_ref[...] = jnp.zeros_like(carry_ref)

    x = x_ref[...]
    tb = x.shape[0]
    row = lax.broadcasted_iota(jnp.int32, (tb, tb), 0)
    col = lax.broadcasted_iota(jnp.int32, (tb, tb), 1)
    tril = jnp.where(col <= row, 1.0, 0.0).astype(F32)
    c = jnp.dot(tril, x, precision=HIGHEST, preferred_element_type=F32) + carry_ref[...]
    o_ref[...] = c
    carry_ref[...] = c[tb - 1:tb, :]


def _cumsum_tokens(x):
    b, t, w = x.shape
    tb = 256
    return pl.pallas_call(
        _cumsum_kernel,
        grid=(b, t // tb),
        in_specs=[pl.BlockSpec((None, tb, w), lambda i, j: (i, j, 0))],
        out_specs=pl.BlockSpec((None, tb, w), lambda i, j: (i, j, 0)),
        out_shape=jax.ShapeDtypeStruct((b, t, w), F32),
        scratch_shapes=[pltpu.VMEM((1, w), F32)],
        compiler_params=_cparams("parallel", "arbitrary"),
        name="cumsum",
    )(x)


def _mlstm_kernel(qkv_ref, og_ref, g_ref, c0_ref, n0_ref, m0_ref,
                  y_ref, cn_ref, nn_ref, mn_ref, c_s, n_s, m_s, *, bb_n, blk):
    step = pl.program_id(1)

    @pl.when(step == 0)
    def _():
        c_s[...] = c0_ref[...]
        n_s[...] = n0_ref[...]
        m_s[...] = m0_ref[...]

    row = lax.broadcasted_iota(jnp.int32, (blk, blk), 0)
    col = lax.broadcasted_iota(jnp.int32, (blk, blk), 1)
    tril = col <= row
    triu = row <= col
    eye = col == row

    def to_row(x_col):
        return jnp.sum(jnp.where(eye, x_col, 0.0), axis=0, keepdims=True)

    for bb in range(bb_n):
        g = g_ref[bb]
        for h in range(H_M):
            q = qkv_ref[bb, :, h * DH_M:(h + 1) * DH_M]
            k = qkv_ref[bb, :, (H_M + h) * DH_M:(H_M + h + 1) * DH_M] * (DH_M ** -0.5)
            v = qkv_ref[bb, :, (2 * H_M + h) * DH_M:(2 * H_M + h + 1) * DH_M]
            i_col = g[:, COL_I + h:COL_I + h + 1]
            f_col = g[:, COL_F + h:COL_F + h + 1]
            i_row = to_row(i_col)
            f_row = to_row(f_col)
            b_col = jnp.sum(jnp.where(tril, f_row, 0.0), axis=1, keepdims=True)
            b_row = jnp.sum(jnp.where(triu, f_col, 0.0), axis=0, keepdims=True)
            m_prev = m_s[bb, h][:, 0:1]
            dmat = jnp.where(tril, b_col - b_row + i_row, NEG_INF)
            g_col = b_col + m_prev
            mt = jnp.maximum(g_col, jnp.max(dmat, axis=1, keepdims=True))
            w_d = jnp.exp(dmat - mt)
            w_g = jnp.exp(g_col - mt)
            qk = lax.dot_general(q, k, (((1,), (1,)), ((), ())), preferred_element_type=F32) * w_d
            c_prev = c_s[bb, h]
            n_prev = n_s[bb, h]
            num = (w_g * jnp.dot(q, c_prev.astype(BF16), preferred_element_type=F32)
                   + jnp.dot(qk.astype(BF16), v, preferred_element_type=F32))
            den = (w_g * jnp.sum(q.astype(F32) * n_prev, axis=1, keepdims=True)
                   + jnp.sum(qk, axis=1, keepdims=True))
            hid = num / jnp.maximum(jnp.abs(den), jnp.exp(-mt))
            o_gate = _sigmoid(og_ref[bb, :, h * DH_M:(h + 1) * DH_M])
            y_ref[bb, :, h * DH_M:(h + 1) * DH_M] = (o_gate * hid).astype(y_ref.dtype)
            m_new = mt[blk - 1:blk, :]
            b_last = b_col[blk - 1:blk, :]
            w_c = jnp.exp(b_last + m_prev - m_new)
            w_s = jnp.exp(b_last - b_col + i_col - m_new)
            kw = k.astype(F32) * w_s
            c_s[bb, h] = w_c * c_prev + lax.dot_general(
                kw.astype(BF16), v, (((0,), (0,)), ((), ())), preferred_element_type=F32)
            n_s[bb, h] = w_c * n_prev + jnp.sum(kw, axis=0, keepdims=True)
            m_s[bb, h] = jnp.broadcast_to(m_new, (1, LANES))

    @pl.when(step == pl.num_programs(1) - 1)
    def _():
        cn_ref[...] = c_s[...]
        nn_ref[...] = n_s[...]
        mn_ref[...] = m_s[...]


def _mlstm(qkv, ogate, gates, c0, n0, m0):
    b, t, _ = qkv.shape
    blk = min(CHUNK, t)
    bb_n = 4
    state_spec = lambda shape: pl.BlockSpec((bb_n,) + shape, lambda i, j: (i, 0, 0, 0))
    return pl.pallas_call(
        functools.partial(_mlstm_kernel, bb_n=bb_n, blk=blk),
        grid=(b // bb_n, t // blk),
        in_specs=[pl.BlockSpec((bb_n, blk, 3 * W_M), lambda i, j: (i, j, 0)),
                  pl.BlockSpec((bb_n, blk, W_M), lambda i, j: (i, j, 0)),
                  pl.BlockSpec((bb_n, blk, GATE_COLS), lambda i, j: (i, j, 0)),
                  state_spec((H_M, DH_M, DH_M)),
                  state_spec((H_M, 1, DH_M)),
                  state_spec((H_M, 1, LANES))],
        out_specs=[pl.BlockSpec((bb_n, blk, W_M), lambda i, j: (i, j, 0)),
                   state_spec((H_M, DH_M, DH_M)),
                   state_spec((H_M, 1, DH_M)),
                   state_spec((H_M, 1, LANES))],
        out_shape=[jax.ShapeDtypeStruct((b, t, W_M), BF16),
                   jax.ShapeDtypeStruct((b, H_M, DH_M, DH_M), F32),
                   jax.ShapeDtypeStruct((b, H_M, 1, DH_M), F32),
                   jax.ShapeDtypeStruct((b, H_M, 1, LANES), F32)],
        scratch_shapes=[pltpu.VMEM((bb_n, H_M, DH_M, DH_M), F32),
                        pltpu.VMEM((bb_n, H_M, 1, DH_M), F32),
                        pltpu.VMEM((bb_n, H_M, 1, LANES), F32)],
        compiler_params=_cparams("parallel", "arbitrary"),
        name="mlstm",
    )(qkv, ogate, gates, c0, n0, m0)


def _online_softmax_step(s, pv, m_ref, l_ref, acc_ref):
    m_old = m_ref[...]
    m_new = jnp.maximum(m_old, jnp.max(s, axis=1, keepdims=True))
    alpha = jnp.exp(m_old - m_new)
    p = jnp.exp(s - m_new)
    l_ref[...] = alpha * l_ref[...] + jnp.sum(p, axis=1, keepdims=True)
    acc_ref[...] = alpha * acc_ref[...] + pv(p.astype(BF16))
    m_ref[...] = m_new


def _split3(x):
    hi = x.astype(BF16).astype(F32)
    mid = (x - hi).astype(BF16).astype(F32)
    lo = (x - hi - mid).astype(BF16).astype(F32)
    return hi, mid, lo


FOX_V_ROWS = DH_F + 16
AUG_CQ = 6


def _fox_prompt_kernel(q_ref, k_ref, v_ref, c_ref, after_ref, o_ref, kaug_s, vt_s, m0_s, m1_s, acc0_s,
                       acc1_s, *, tq, n_blocks):
    del after_ref
    hp = pl.program_id(1)
    qi = pl.program_id(2)
    lane = lax.broadcasted_iota(jnp.int32, (tq, LANES), 1)
    m_s, acc_s = (m0_s, m1_s), (acc0_s, acc1_s)

    def head_col(c_blk, x):
        return jnp.sum(jnp.where(lane == COL_FF + 2 * hp + x, c_blk, 0.0), axis=1, keepdims=True)

    def aug_tile(entries):
        tile = jnp.zeros((tq, LANES), F32)
        for l, val in entries:
            tile = jnp.where(lane == l, val, tile)
        return tile.astype(BF16)

    @pl.when(qi == 0)
    def _():
        ones_rows = jnp.where(lax.broadcasted_iota(jnp.int32, (FOX_V_ROWS - DH_F, tq), 0) == 0, 1.0, 0.0)

        def chunk(i, carry):
            rs = pl.multiple_of(i * tq, tq)
            c_blk = c_ref[pl.ds(rs, tq), :]
            pieces = _split3(head_col(c_blk, 0)) + _split3(head_col(c_blk, 1))
            entries = list(enumerate(pieces)) + [(AUG_CQ + j, 1.0) for j in range(3)]
            kaug_s[pl.ds(rs, tq), 0:LANES] = k_ref[pl.ds(rs, tq), :]
            kaug_s[pl.ds(rs, tq), LANES:2 * LANES] = aug_tile(entries)
            vt = v_ref[pl.ds(rs, tq), :].astype(F32).T
            for x in range(2):
                vt_s[x, 0:DH_F, pl.ds(rs, tq)] = vt[x * DH_F:(x + 1) * DH_F, :].astype(BF16)
                vt_s[x, DH_F:FOX_V_ROWS, pl.ds(rs, tq)] = ones_rows.astype(BF16)
            return carry

        lax.fori_loop(0, n_blocks, chunk, 0)

    qs = pl.multiple_of(qi * tq, tq)
    c_q = c_ref[pl.ds(qs, tq), :]
    q2 = q_ref[...] * (DH_F ** -0.5)
    q_aug = []
    for x in range(2):
        cq3 = _split3(head_col(c_q, x))
        entries = [(3 * x + j, -1.0) for j in range(3)] + [(AUG_CQ + j, cq3[j]) for j in range(3)]
        q_head = jnp.where((lane < DH_F) == (x == 0), q2, jnp.zeros_like(q2))
        q_aug.append(jnp.concatenate([q_head, aug_tile(entries)], axis=1))
        m_s[x][...] = jnp.full(m_s[x].shape, NEG_INF, F32)
        acc_s[x][...] = jnp.zeros(acc_s[x].shape, F32)

    def block(kb, masked):
        ks = pl.multiple_of(kb * tq, tq)
        k_blk = kaug_s[pl.ds(ks, tq), :]
        scores = [lax.dot_general(k_blk, q_aug[x], (((1,), (1,)), ((), ())), preferred_element_type=F32)
                  for x in range(2)]
        for x in range(2):
            st = scores[x]
            if masked:
                k_pos = lax.broadcasted_iota(jnp.int32, (tq, tq), 0)
                q_pos = lax.broadcasted_iota(jnp.int32, (tq, tq), 1)
                st = jnp.where(k_pos <= q_pos, st, NEG_INF)
            m_old = m_s[x][...]
            m_new = jnp.maximum(m_old, jnp.max(st, axis=0, keepdims=True))
            p = jnp.exp(st - m_new).astype(BF16)
            acc_s[x][...] = (jnp.exp(m_old - m_new) * acc_s[x][...]
                             + jnp.dot(vt_s[x, :, pl.ds(ks, tq)], p, preferred_element_type=F32))
            m_s[x][...] = m_new

    def body(kb, carry):
        block(kb, False)
        return carry

    lax.fori_loop(0, qi, body, 0)
    block(qi, True)
    out_t = jnp.concatenate([acc_s[x][0:DH_F, :] / acc_s[x][DH_F:DH_F + 1, :] for x in range(2)], axis=0)
    o_ref[...] = out_t.T.astype(o_ref.dtype)


def _fox_prompt(q, k, v, c, batch, after):
    _, t, _ = q.shape
    tq = 512
    head_pair = lambda i, h, j: (batch, 0, h)
    return pl.pallas_call(
        functools.partial(_fox_prompt_kernel, tq=tq, n_blocks=t // tq),
        grid=(1, H_F // 2, t // tq),
        in_specs=[pl.BlockSpec((None, tq, LANES), lambda i, h, j: (batch, j, h)),
                  pl.BlockSpec((None, t, LANES), head_pair),
                  pl.BlockSpec((None, t, LANES), head_pair),
                  pl.BlockSpec((None, t, LANES), lambda i, h, j: (batch, 0, 0)),
                  ORDERED],
        out_specs=pl.BlockSpec((tq, LANES), lambda i, h, j: (j, h)),
        out_shape=jax.ShapeDtypeStruct((t, W_F), BF16),
        scratch_shapes=[pltpu.VMEM((t, 2 * LANES), BF16),
                        pltpu.VMEM((2, FOX_V_ROWS, t), BF16),
                        pltpu.VMEM((1, tq), F32), pltpu.VMEM((1, tq), F32),
                        pltpu.VMEM((FOX_V_ROWS, tq), F32), pltpu.VMEM((FOX_V_ROWS, tq), F32)],
        compiler_params=_cparams("parallel", "parallel", "arbitrary"),
        name="fox_prompt",
    )(q, k, v, c, after)


def _fox_sample_kernel(q_ref, kc_ref, vc_ref, kn_ref, vn_ref, cq_ref, ctc_ref, ctn_ref, o_ref,
                       m_s, l_s, acc_s, *, tn):
    kb = pl.program_id(1)

    @pl.when(kb == 0)
    def _():
        m_s[...] = jnp.full_like(m_s, NEG_INF)
        l_s[...] = jnp.zeros_like(l_s)
        acc_s[...] = jnp.zeros_like(acc_s)

    lane = lax.broadcasted_iota(jnp.int32, (tn, LANES), 1)
    c_blk = cq_ref[...]
    nt = (((1,), (1,)), ((), ()))

    def head_query(hd):
        q = q_ref[:, hd * DH_F:(hd + 1) * DH_F] * (DH_F ** -0.5)
        cq = jnp.sum(jnp.where(lane == COL_FF + hd, c_blk, 0.0), axis=1, keepdims=True)
        return q, cq

    for hd in range(H_F):
        q, cq = head_query(hd)
        k_t = kc_ref[hd].astype(BF16)
        v_t = vc_ref[hd].astype(BF16)
        s = jnp.dot(q, k_t, preferred_element_type=F32) + cq - ctc_ref[hd // 2, hd % 2:hd % 2 + 1, :]
        _online_softmax_step(s, lambda p: lax.dot_general(p, v_t, nt, preferred_element_type=F32),
                             m_s.at[hd], l_s.at[hd], acc_s.at[hd])

    @pl.when(kb == pl.num_programs(1) - 1)
    def _():
        q_pos = lax.broadcasted_iota(jnp.int32, (tn, tn), 0)
        k_pos = lax.broadcasted_iota(jnp.int32, (tn, tn), 1)
        for hd in range(H_F):
            q, cq = head_query(hd)
            k_new = kn_ref[:, hd * DH_F:(hd + 1) * DH_F]
            v_new = vn_ref[:, hd * DH_F:(hd + 1) * DH_F]
            s = (lax.dot_general(q, k_new, nt, preferred_element_type=F32)
                 + cq - ctn_ref[hd // 2, hd % 2:hd % 2 + 1, :])
            s = jnp.where(k_pos <= q_pos, s, NEG_INF)
            _online_softmax_step(s, lambda p: jnp.dot(p, v_new, preferred_element_type=F32),
                                 m_s.at[hd], l_s.at[hd], acc_s.at[hd])
            o_ref[:, hd * DH_F:(hd + 1) * DH_F] = (acc_s[hd] / l_s[hd]).astype(o_ref.dtype)


def _fox_sample(q, k_cache_t, v_cache_t, k_new, v_new, cq, ct_cache, ct_new):
    b, tn, _ = q.shape
    p = k_cache_t.shape[3]
    tk = 1024
    new = pl.BlockSpec((None, tn, W_F), lambda i, j: (i, 0, 0))
    cache = pl.BlockSpec((None, H_F, DH_F, tk), lambda i, j: (i, 0, 0, j))
    return pl.pallas_call(
        functools.partial(_fox_sample_kernel, tn=tn),
        grid=(b, p // tk),
        in_specs=[new, cache, cache, new, new,
                  pl.BlockSpec((None, tn, LANES), lambda i, j: (i, 0, 0)),
                  pl.BlockSpec((None, H_F // 2, 2, tk), lambda i, j: (i, 0, 0, j)),
                  pl.BlockSpec((None, H_F // 2, 2, tn), lambda i, j: (i, 0, 0, 0))],
        out_specs=new,
        out_shape=jax.ShapeDtypeStruct((b, tn, W_F), BF16),
        scratch_shapes=[pltpu.VMEM((H_F, tn, 1), F32),
                        pltpu.VMEM((H_F, tn, 1), F32),
                        pltpu.VMEM((H_F, tn, DH_F), F32)],
        compiler_params=_cparams("parallel", "arbitrary"),
        name="fox_sample",
    )(q, k_cache_t, v_cache_t, k_new, v_new, cq, ct_cache, ct_new)


def _merge_kernel(h_ref, ym_ref, yf_ref, gm_ref, gf_ref, wm_ref, wf_ref, wo_ref, after_ref, o_ref):
    del after_ref
    a = jnp.dot(ym_ref[...], wm_ref[...], preferred_element_type=F32)
    b = jnp.dot(yf_ref[...], wf_ref[...], preferred_element_type=F32)
    merge = _sigmoid(gm_ref[...]) * a + _sigmoid(gf_ref[...]) * b
    o_ref[...] = h_ref[...] + jnp.dot(merge.astype(BF16), wo_ref[...], preferred_element_type=F32)


def _merge(h, ym, yf, gates, wm, wf, wo, first_token, after):
    n, d = yf.shape
    tm = min(512, n)
    first = first_token // tm
    tok = lambda c: pl.BlockSpec((tm, d), lambda i: (first + i, c))
    local = pl.BlockSpec((tm, d), lambda i: (i, 0))
    wspec = pl.BlockSpec((d, d), lambda i: (0, 0))
    return pl.pallas_call(
        _merge_kernel,
        grid=(n // tm,),
        in_specs=[tok(0), tok(0), local, tok(1), tok(2), wspec, wspec, wspec, ORDERED],
        out_specs=local,
        out_shape=jax.ShapeDtypeStruct((n, d), F32),
        compiler_params=_cparams("parallel"),
        name="merge",
    )(h, ym, yf, gates, gates, wm, wf, wo, after)


def _top_rows(s, count, ids=None, payload=None):
    if ids is None:
        ids = lax.broadcasted_iota(jnp.int32, s.shape, 0)
    big = jnp.int32(2 ** 30)
    vals, sel, pay = [], [], []
    for _ in range(count):
        m = jnp.max(s, axis=0, keepdims=True)
        am = jnp.min(jnp.where(s == m, ids, big), axis=0, keepdims=True)
        hit = ids == am
        vals.append(m)
        sel.append(am)
        if payload is not None:
            pay.append(jnp.max(jnp.where(hit, payload, -1), axis=0, keepdims=True))
        s = jnp.where(hit, NEG_INF, s)
    cat = lambda xs: jnp.concatenate(xs, axis=0)
    return cat(vals), cat(sel), (cat(pay) if payload is not None else None)


def _pair_candidates(v1, i1, v2, i2):
    t = v1.shape[1]
    half = PEER_TOPK // 2
    r16 = lax.broadcasted_iota(jnp.int32, (PEER_TOPK, t), 0)
    r8 = lax.broadcasted_iota(jnp.int32, (half, t), 0)
    sums = [v1[0:1, :] + v2]
    flat = [r16]
    expert = [i1[0:1, :] * PEER_KEYS + i2]
    for a in range(1, half):
        sums.append(v1[a:a + 1, :] + v2[0:half, :])
        flat.append(r8 + a * PEER_TOPK)
        expert.append(i1[a:a + 1, :] * PEER_KEYS + i2[0:half, :])
    sums.append(v1[half:, :] + v2[0:1, :])
    flat.append((r8 + half) * PEER_TOPK)
    expert.append(i1[half:, :] * PEER_KEYS + i2[0:1, :])
    cat = lambda xs: jnp.concatenate(xs, axis=0)
    return cat(sums), cat(flat), cat(expert)


def _peer_score_kernel(h_ref, g_ref, wqt_ref, k1_ref, k2_ref, xn_ref, idx_ref, gate_ref):
    xn = _rms(h_ref[...], g_ref[...])
    xn_ref[...] = xn.astype(xn_ref.dtype)
    qt = lax.dot_general(wqt_ref[...], xn, (((1,), (1,)), ((), ())),
                         precision=HIGHEST, preferred_element_type=F32)
    half = PEER_DQ // 2
    idx_rows, gate_rows = [], []
    for hd in range(PEER_HEADS):
        q1 = qt[hd * PEER_DQ:hd * PEER_DQ + half, :]
        q2 = qt[hd * PEER_DQ + half:(hd + 1) * PEER_DQ, :]
        s1 = jnp.dot(k1_ref[hd], q1, precision=HIGHEST, preferred_element_type=F32)
        s2 = jnp.dot(k2_ref[hd], q2, precision=HIGHEST, preferred_element_type=F32)
        v1, i1, _ = _top_rows(s1, PEER_TOPK)
        v2, i2, _ = _top_rows(s2, PEER_TOPK)
        cand, flat, expert = _pair_candidates(v1, i1, v2, i2)
        sc, _, ex = _top_rows(cand, PEER_TOPK, ids=flat, payload=expert)
        e = jnp.exp(sc - sc[0:1, :])
        gate_rows.append(e / jnp.sum(e, axis=0, keepdims=True))
        idx_rows.append(ex)
    gate_ref[...] = jnp.concatenate(gate_rows, axis=0).T
    idx_ref[...] = jnp.concatenate(idx_rows, axis=0).T


def _peer_score(h, g, wqt, k1, k2):
    n, d = h.shape
    tt = min(256, n)
    return pl.pallas_call(
        _peer_score_kernel,
        grid=(n // tt,),
        in_specs=[pl.BlockSpec((tt, d), lambda i: (i, 0)),
                  pl.BlockSpec((1, d), lambda i: (0, 0)),
                  pl.BlockSpec(wqt.shape, lambda i: (0, 0)),
                  pl.BlockSpec(k1.shape, lambda i: (0, 0, 0)),
                  pl.BlockSpec(k2.shape, lambda i: (0, 0, 0))],
        out_specs=[pl.BlockSpec((tt, d), lambda i: (i, 0)),
                   pl.BlockSpec((tt, PEER_SEL), lambda i: (i, 0)),
                   pl.BlockSpec((tt, PEER_SEL), lambda i: (i, 0))],
        out_shape=[jax.ShapeDtypeStruct((n, d), F32),
                   jax.ShapeDtypeStruct((n, PEER_SEL), jnp.int32),
                   jax.ShapeDtypeStruct((n, PEER_SEL), F32)],
        compiler_params=_cparams("parallel"),
        name="peer_score",
    )(h, g, wqt, k1, k2)


SC_CORES = 2
SC_SUBCORES = 16
SC_WORKERS = SC_CORES * SC_SUBCORES
SC_ROWS = 64


def _pack_table(tab):
    half = tab.shape[1] // 2
    bits = lax.bitcast_convert_type(tab.astype(BF16), jnp.uint16).astype(jnp.uint32)
    word = bits[:, :half] | (bits[:, half:] << 16)
    return lax.bitcast_convert_type(word, jnp.int32)


SC_TOK = 8
SC_SLOTS = 3
SC_LANES = 16
STEPS_PER_TOKEN = PEER_SEL // SC_ROWS
SC_PARAMS = pltpu.CompilerParams(needs_layout_passes=False)


def _row_source(table_hbm, idx_v, local_step, global_step):
    del global_step
    return table_hbm.at[idx_v.at[local_step]]


def _sc_unpack(wd):
    lo = lax.bitcast_convert_type(wd << 16, F32)
    hi = lax.bitcast_convert_type(wd & jnp.int32(-65536), F32)
    return lo, hi


def _sc_token_blocks(n, body_block):
    per_worker = n // SC_WORKERS
    assert per_worker * SC_WORKERS == n and per_worker % SC_TOK == 0
    wid = lax.axis_index("s") * SC_CORES + lax.axis_index("c")

    @pl.loop(0, per_worker // SC_TOK)
    def _(blk):
        body_block(wid * per_worker + blk * SC_TOK)


def _sc_pipelined_steps(table_hbm, idx_v, rows_v, sems, first_step, consume):
    n_steps = SC_TOK * STEPS_PER_TOKEN
    ahead = SC_SLOTS - 1

    def row_gather(j):
        slot = j % SC_SLOTS
        return pltpu.make_async_copy(_row_source(table_hbm, idx_v, j, first_step + j),
                                     rows_v.at[slot], sems.at[slot])

    for j in range(ahead):
        row_gather(j).start()

    @pl.loop(0, n_steps)
    def _(j):
        @pl.when(j + ahead < n_steps)
        def _():
            row_gather(j + ahead).start()

        row_gather(j).wait()
        consume(j % SC_SLOTS, j // STEPS_PER_TOKEN, j % STEPS_PER_TOKEN)


def _sc_expert_dot(table, idx_steps, x):
    n, d = x.shape
    w = d // 2
    assert STEPS_PER_TOKEN == 2 and table.shape[1] == w
    mesh = plsc.VectorSubcoreMesh(core_axis_name="c", subcore_axis_name="s")
    n_steps = SC_TOK * STEPS_PER_TOKEN
    group = 4

    @functools.partial(
        pl.kernel, mesh=mesh,
        out_type=jax.ShapeDtypeStruct((n, PEER_SEL), F32),
        scratch_types=[pltpu.VMEM((n_steps, SC_ROWS), jnp.int32),
                       pltpu.VMEM((SC_TOK, d), F32),
                       pltpu.VMEM((SC_SLOTS, SC_ROWS, w), jnp.int32),
                       pltpu.VMEM((SC_TOK, PEER_SEL), F32),
                       pltpu.SemaphoreType.DMA((SC_SLOTS,))],
        compiler_params=SC_PARAMS,
    )
    def expert_dot(table_hbm, idx_hbm, x_hbm, act_hbm, idx_v, x_v, rows_v, act_v, sems):
        lanes = lax.iota(jnp.int32, SC_LANES)
        zero = jnp.zeros((SC_LANES,), F32)

        def consume(slot, tl, half):
            @pl.loop(0, SC_ROWS // SC_LANES)
            def _(g):
                act_vec = zero
                for q in range(SC_LANES // group):
                    r0 = g * SC_LANES + q * group

                    def chunk(j, accs):
                        c = pl.multiple_of(j * SC_LANES, SC_LANES)
                        x_lo = x_v[tl, pl.ds(c, SC_LANES)]
                        x_hi = x_v[tl, pl.ds(w + c, SC_LANES)]
                        out = []
                        for rr in range(group):
                            lo, hi = _sc_unpack(rows_v[slot, r0 + rr, pl.ds(c, SC_LANES)])
                            out.append(accs[rr] + lo * x_lo + hi * x_hi)
                        return tuple(out)

                    accs = lax.fori_loop(0, w // SC_LANES, chunk, (zero,) * group, unroll=2)
                    for rr in range(group):
                        act_vec = jnp.where(lanes == q * group + rr, jnp.sum(accs[rr]), act_vec)
                act_v[tl, pl.ds(half * SC_ROWS + g * SC_LANES, SC_LANES)] = act_vec

        def block(t0):
            s0 = t0 * STEPS_PER_TOKEN
            pltpu.sync_copy(idx_hbm.at[pl.ds(s0, n_steps)], idx_v)
            pltpu.sync_copy(x_hbm.at[pl.ds(t0, SC_TOK)], x_v)
            _sc_pipelined_steps(table_hbm, idx_v, rows_v, sems, s0, consume)
            pltpu.sync_copy(act_v, act_hbm.at[pl.ds(t0, SC_TOK)])

        _sc_token_blocks(n, block)

    return expert_dot(table, idx_steps, x)


def _sc_expert_sum(table, idx_steps, wgt):
    n = wgt.shape[0]
    w = table.shape[1]
    d = 2 * w
    assert STEPS_PER_TOKEN == 2
    mesh = plsc.VectorSubcoreMesh(core_axis_name="c", subcore_axis_name="s")
    n_steps = SC_TOK * STEPS_PER_TOKEN
    cols = 8

    @functools.partial(
        pl.kernel, mesh=mesh,
        out_type=jax.ShapeDtypeStruct((n, d), F32),
        scratch_types=[pltpu.VMEM((n_steps, SC_ROWS), jnp.int32),
                       pltpu.VMEM((SC_TOK, PEER_SEL), F32),
                       pltpu.VMEM((SC_SLOTS, SC_ROWS, w), jnp.int32),
                       pltpu.VMEM((SC_TOK, d), F32),
                       pltpu.VMEM((SC_ROWS, SC_LANES), F32),
                       pltpu.SemaphoreType.DMA((SC_SLOTS,))],
        compiler_params=SC_PARAMS,
    )
    def expert_sum(table_hbm, idx_hbm, wgt_hbm, out_hbm, idx_v, wgt_v, rows_v, out_v, splat_v, sems):
        zero = jnp.zeros((SC_LANES,), F32)
        lanes = lax.iota(jnp.int32, SC_LANES)

        def consume(slot, tl, half):
            @pl.loop(0, SC_ROWS // SC_LANES)
            def _(g):
                w16 = wgt_v[tl, pl.ds(half * SC_ROWS + g * SC_LANES, SC_LANES)]
                for rr in range(SC_LANES):
                    one = jnp.sum(jnp.where(lanes == rr, w16, 0.0))
                    splat_v[g * SC_LANES + rr, :] = jnp.full((SC_LANES,), one, F32)

            for cb in range(w // (cols * SC_LANES)):
                base = cb * cols * SC_LANES

                def row(r, accs):
                    wv = splat_v[r, :]
                    out = []
                    for jj in range(cols):
                        lo, hi = _sc_unpack(rows_v[slot, r, pl.ds(base + jj * SC_LANES, SC_LANES)])
                        out.append(accs[2 * jj] + wv * lo)
                        out.append(accs[2 * jj + 1] + wv * hi)
                    return tuple(out)

                accs = lax.fori_loop(0, SC_ROWS, row, (zero,) * (2 * cols), unroll=2)
                for jj in range(cols):
                    c = base + jj * SC_LANES
                    out_v[tl, pl.ds(c, SC_LANES)] = out_v[tl, pl.ds(c, SC_LANES)] + accs[2 * jj]
                    out_v[tl, pl.ds(w + c, SC_LANES)] = out_v[tl, pl.ds(w + c, SC_LANES)] + accs[2 * jj + 1]

        def block(t0):
            s0 = t0 * STEPS_PER_TOKEN
            pltpu.sync_copy(idx_hbm.at[pl.ds(s0, n_steps)], idx_v)
            pltpu.sync_copy(wgt_hbm.at[pl.ds(t0, SC_TOK)], wgt_v)

            @pl.loop(0, SC_TOK)
            def _(tl):
                for c in range(0, d, SC_LANES):
                    out_v[tl, pl.ds(c, SC_LANES)] = zero
            _sc_pipelined_steps(table_hbm, idx_v, rows_v, sems, s0, consume)
            pltpu.sync_copy(out_v, out_hbm.at[pl.ds(t0, SC_TOK)])

        _sc_token_blocks(n, block)

    return expert_sum(table, idx_steps, wgt)


def _gelu_exact(x):
    return 0.5 * x * (1.0 + lax.erf(x * (2.0 ** -0.5)))


def _peer_weight_kernel(act_ref, gate_ref, after_ref, o_ref):
    del after_ref
    o_ref[...] = gate_ref[...] * _gelu_exact(act_ref[...])


def _peer_weight(act, gate, after):
    n = act.shape[0]
    tt = min(2048, n)
    spec = pl.BlockSpec((tt, PEER_SEL), lambda i: (i, 0))
    return pl.pallas_call(
        _peer_weight_kernel,
        grid=(n // tt,),
        in_specs=[spec, spec, ORDERED],
        out_specs=spec,
        out_shape=jax.ShapeDtypeStruct((n, PEER_SEL), F32),
        compiler_params=_cparams("parallel"),
        name="peer_weight",
    )(act, gate, after)


def _ple_kernel(h_ref, peer_ref, p_ref, gp_ref, gfin_ref, wg_ref, wp_ref, o_ref):
    h = h_ref[...] + peer_ref[...]
    e = _rms(h, gp_ref[...]).astype(BF16)
    gate = _sigmoid(jnp.dot(e, wg_ref[...], preferred_element_type=F32))
    proj = jnp.dot(p_ref[...].astype(BF16), wp_ref[...], preferred_element_type=F32)
    o_ref[...] = _rms(h + gate * proj, gfin_ref[...])


def _ple_final(h, peer, p, g_ple, g_final, wg, wp, first_token):
    n, d = h.shape
    tm = min(512, n)
    first = first_token // tm
    return pl.pallas_call(
        _ple_kernel,
        grid=(n // tm,),
        in_specs=[pl.BlockSpec((tm, d), lambda i: (i, 0)),
                  pl.BlockSpec((tm, d), lambda i: (i, 0)),
                  pl.BlockSpec((tm, D_PLE), lambda i: (first + i, 0)),
                  pl.BlockSpec((1, d), lambda i: (0, 0)),
                  pl.BlockSpec((1, d), lambda i: (0, 0)),
                  pl.BlockSpec((d, d), lambda i: (0, 0)),
                  pl.BlockSpec((D_PLE, d), lambda i: (0, 0))],
        out_specs=pl.BlockSpec((tm, d), lambda i: (i, 0)),
        out_shape=jax.ShapeDtypeStruct((n, d), F32),
        compiler_params=_cparams("parallel"),
        name="ple_final",
    )(h, peer, p, g_ple, g_final, wg, wp)


def _prep_weights(norm_mix, w_in, mlstm_b_i, mlstm_b_f, fox_b_f, w_br_m, w_br_f, w_out, norm_ffn,
                  peer_w_q, peer_keys1, peer_keys2, peer_u, peer_v, norm_ple, w_ple_gate, w_ple_proj,
                  norm_final):
    o = [0]
    for s in (W_M, W_M, W_M, W_M, H_M, H_M, W_F, W_F, W_F, H_F, D_MODEL, D_MODEL):
        o.append(o[-1] + s)
    seg = lambda a, b: w_in[:, o[a]:o[b]]
    w_gate = jnp.concatenate([seg(4, 6), seg(9, 10)], axis=1)
    w_gate = jnp.pad(w_gate, ((0, 0), (0, GATE_COLS - w_gate.shape[1])))
    b_gate = jnp.concatenate([mlstm_b_i, mlstm_b_f, fox_b_f]).astype(F32)
    b_gate = jnp.pad(b_gate, (0, GATE_COLS - b_gate.shape[0]))[None, :]
    row = lambda v: v.astype(F32)[None, :]
    return dict(
        norm_mix=row(norm_mix),
        w_mqkv=seg(0, 3).astype(BF16),
        w_og=jnp.concatenate([seg(3, 4), seg(10, 12)], axis=1).astype(BF16),
        w_fq=seg(6, 7).astype(BF16), w_fk=seg(7, 8).astype(BF16), w_fv=seg(8, 9).astype(BF16),
        w_gate=w_gate.astype(F32), b_gate=b_gate,
        w_br_m=w_br_m.astype(BF16), w_br_f=w_br_f.astype(BF16), w_out=w_out.astype(BF16),
        norm_ffn=row(norm_ffn), wqt=peer_w_q.T.astype(F32),
        k1=peer_keys1.astype(F32), k2=peer_keys2.astype(F32),
        u_pk=_pack_table(peer_u), v_pk=_pack_table(peer_v),
        norm_ple=row(norm_ple), w_ple_gate=w_ple_gate.astype(BF16), w_ple_proj=w_ple_proj.astype(BF16),
        norm_final=row(norm_final),
    )


def _ct_rows(c):
    b, t, _ = c.shape
    return jnp.swapaxes(c[:, :, COL_FF:COL_FF + H_F], 1, 2).reshape(b, H_F // 2, 2, t)


def _peer_first_pass(h, w):
    n = h.shape[0]
    xn, idx, gate = _peer_score(h, w["norm_ffn"], w["wqt"], w["k1"], w["k2"])
    steps = idx.reshape(n * STEPS_PER_TOKEN, SC_ROWS)
    return steps, gate, _sc_expert_dot(w["u_pk"], steps, xn)


def _peer_second_pass(first, w, after):
    steps, gate, act = first
    wgt = _peer_weight(act, gate, after)
    return _sc_expert_sum(w["v_pk"], steps, wgt), wgt


def _layer(x, p, w, mstate, fox_cache):
    b, t, d = x.shape
    n = b * t
    h = x.reshape(n, d)
    g = w["norm_mix"]
    qkv = _norm_matmul(h, g, w["w_mqkv"], BF16, "proj_mlstm_qkv")
    og = _norm_matmul(h, g, w["w_og"], F32, "proj_gates")
    fq = _norm_matmul(h, g, w["w_fq"], BF16, "proj_fox_q")
    fk, fk_heads = _norm_matmul_heads(h, g, w["w_fk"], "proj_fox_k")
    fv, fv_heads = _norm_matmul_heads(h, g, w["w_fv"], "proj_fox_v")
    gates = _gates(h, g, w["w_gate"], w["b_gate"])

    c0, n0, m0 = mstate
    ym, c_new, n_new, m_new = _mlstm(
        qkv.reshape(b, t, 3 * W_M), og.reshape(b, t, 3 * D_MODEL), gates.reshape(b, t, GATE_COLS),
        c0.astype(F32), n0.astype(F32)[:, :, None, :],
        jnp.broadcast_to(m0.astype(F32)[:, :, None, None], (b, H_M, 1, LANES)))

    def merged(yf, first_token, after):
        return _merge(h, ym.reshape(n, W_M), yf, og, w["w_br_m"], w["w_br_f"], w["w_out"], first_token, after)

    def output(hm, peer, first_token):
        return _ple_final(hm, peer, p.reshape(n, D_PLE), w["norm_ple"], w["norm_final"],
                          w["w_ple_gate"], w["w_ple_proj"], first_token)

    neutral = w["norm_ffn"]
    gates3 = gates.reshape(b, t, GATE_COLS)
    if fox_cache is None:
        c = _cumsum_tokens(gates3)
        seqs = (fq.reshape(b, t, W_F), fk.reshape(b, t, W_F), fv.reshape(b, t, W_F))
        hms, firsts, peers, wgts = [], [], [], []
        for bi in range(b):
            yf = _fox_prompt(*seqs, c, bi, wgts[bi - 2] if bi >= 2 else neutral)
            hms.append(merged(yf, bi * t, peers[bi - 3] if bi >= 3 else neutral))
            firsts.append(_peer_first_pass(hms[bi], w))
            if bi >= 1:
                peer, wgt = _peer_second_pass(firsts[bi - 1], w, firsts[bi][1])
                peers.append(peer)
                wgts.append(wgt)
        peer, wgt = _peer_second_pass(firsts[b - 1], w, neutral)
        peers.append(peer)
        wgts.append(wgt)
        y = jnp.concatenate([output(hms[bi], peers[bi], bi * t) for bi in range(b)], axis=0)
    else:
        ck, cv, clf = fox_cache
        past = ck.shape[1]
        lf = jnp.pad(clf.astype(F32), ((0, 0), (0, 0), (COL_FF, GATE_COLS - COL_FF - H_F)))
        lf = jnp.concatenate([lf, gates3], axis=1)
        pad_t = (-lf.shape[1]) % 256
        c = _cumsum_tokens(jnp.pad(lf, ((0, 0), (0, pad_t), (0, 0))))
        ct = _ct_rows(c)
        keys_minor = lambda cache: jnp.transpose(cache.astype(F32), (0, 2, 3, 1))
        yf = _fox_sample(fq.reshape(b, t, W_F), keys_minor(ck), keys_minor(cv),
                         fk.reshape(b, t, W_F), fv.reshape(b, t, W_F),
                         c[:, past:past + t, :], ct[..., :past], ct[..., past:past + t])
        hm = merged(yf.reshape(n, W_F), 0, neutral)
        peer, wgt = _peer_second_pass(_peer_first_pass(hm, w), w, neutral)
        peers, wgts = [peer], [wgt]
        y = output(hm, peer, 0)

    state = (fk_heads.reshape(1, b, t, H_F, DH_F), fv_heads.reshape(1, b, t, H_F, DH_F),
             gates3[None, :, :, COL_FF:COL_FF + H_F],
             c_new[None], n_new[None, :, :, 0, :], m_new[None, :, :, 0, 0])
    return y.reshape(b, t, d), state, (peers, wgts)


def kernel(x_prompt, x_sample, p_prompt, p_sample, cache_fox_k, cache_fox_v, cache_fox_logf, state_mlstm_C, state_mlstm_n, state_mlstm_m, norm_mix, w_in, mlstm_b_i, mlstm_b_f, fox_b_f, w_br_m, w_br_f, w_out, norm_ffn, peer_w_q, peer_keys1, peer_keys2, peer_u, peer_v, norm_ple, w_ple_gate, w_ple_proj, norm_final):
    assert w_in.shape[0] == 1, "single-layer trunk"
    w = _prep_weights(norm_mix[0], w_in[0], mlstm_b_i[0], mlstm_b_f[0], fox_b_f[0], w_br_m[0], w_br_f[0],
                      w_out[0], norm_ffn[0], peer_w_q[0], peer_keys1[0], peer_keys2[0], peer_u[0], peer_v[0],
                      norm_ple[0], w_ple_gate[0], w_ple_proj[0], norm_final)
    bp = x_prompt.shape[0]
    zeros = (jnp.zeros((bp, H_M, DH_M, DH_M), F32), jnp.zeros((bp, H_M, DH_M), F32), jnp.zeros((bp, H_M), F32))
    y_p, sp, (peers, wgts) = _layer(x_prompt, p_prompt[0], w, zeros, None)
    bs, ts, d = x_sample.shape
    x_s = _ordered(x_sample.reshape(bs * ts, d), peers[-3], wgts[-2]).reshape(bs, ts, d)
    y_s, ss, _ = _layer(x_s, p_sample[0], w,
                        (state_mlstm_C[0], state_mlstm_n[0], state_mlstm_m[0]),
                        (cache_fox_k[0], cache_fox_v[0], cache_fox_logf[0]))
    return (y_p, y_s) + sp + ss
```

```python
import functools

import jax
import jax.numpy as jnp
from jax import lax
from jax.experimental import pallas as pl
from jax.experimental.pallas import tpu as pltpu
from jax.experimental.pallas import tpu_sc as plsc

D_MODEL = 1024
CHUNK = 64
H_M = 4
DH_M = 256
W_M = H_M * DH_M
H_F = 16
DH_F = 64
W_F = H_F * DH_F
D_PLE = 256
PEER_HEADS = 8
PEER_KEYS = 128
PEER_DQ = 256
PEER_TOPK = 16
PEER_SEL = PEER_HEADS * PEER_TOPK
EPS = 1e-6

LANES = 128
GATE_COLS = LANES
COL_I, COL_F, COL_FF = 0, H_M, 2 * H_M
VMEM_LIMIT = 56 * 1024 * 1024
HIGHEST = lax.Precision.HIGHEST
F32 = jnp.float32
BF16 = jnp.bfloat16
NEG_INF = float("-inf")


def _cparams(*sem):
    return pltpu.CompilerParams(dimension_semantics=sem, vmem_limit_bytes=VMEM_LIMIT)


ORDERED = pl.BlockSpec(memory_space=pl.ANY)


def _ordered_kernel(x_ref, after_a, after_b, o_ref):
    del after_a, after_b
    o_ref[...] = x_ref[...]


def _ordered(x, after_a, after_b):
    n, d = x.shape
    spec = pl.BlockSpec((n, d), lambda i: (0, 0))
    return pl.pallas_call(
        _ordered_kernel, grid=(1,), in_specs=[spec, ORDERED, ORDERED], out_specs=spec,
        out_shape=jax.ShapeDtypeStruct((n, d), x.dtype),
        compiler_params=_cparams("arbitrary"), name="ordered_copy",
    )(x, after_a, after_b)


def _rms(x, g):
    return x * lax.rsqrt(jnp.mean(x * x, axis=-1, keepdims=True) + EPS) * g


def _sigmoid(x):
    return 1.0 / (1.0 + jnp.exp(-x))


def _norm_matmul_kernel(x_ref, g_ref, w_ref, o_ref, a_ref):
    @pl.when(pl.program_id(1) == 0)
    def _():
        a_ref[...] = _rms(x_ref[...], g_ref[...]).astype(a_ref.dtype)

    o_ref[...] = jnp.dot(a_ref[...], w_ref[...], preferred_element_type=F32).astype(o_ref.dtype)


def _norm_matmul(x, g, w, out_dtype, name):
    n, d = x.shape
    cols = w.shape[1]
    tm = min(1024, n)
    tn = 1024
    return pl.pallas_call(
        _norm_matmul_kernel,
        grid=(n // tm, cols // tn),
        in_specs=[pl.BlockSpec((tm, d), lambda i, j: (i, 0)),
                  pl.BlockSpec((1, d), lambda i, j: (0, 0)),
                  pl.BlockSpec((d, tn), lambda i, j: (0, j))],
        out_specs=pl.BlockSpec((tm, tn), lambda i, j: (i, j)),
        out_shape=jax.ShapeDtypeStruct((n, cols), out_dtype),
        scratch_shapes=[pltpu.VMEM((tm, d), BF16)],
        compiler_params=_cparams("parallel", "arbitrary"),
        name=name,
    )(x, g, w)


def _norm_matmul_heads_kernel(x_ref, g_ref, w_ref, o_ref, oh_ref):
    a = _rms(x_ref[...], g_ref[...]).astype(BF16)
    z = jnp.dot(a, w_ref[...], preferred_element_type=F32)
    o_ref[...] = z.astype(o_ref.dtype)
    for h in range(H_F):
        oh_ref[:, h, :] = z[:, h * DH_F:(h + 1) * DH_F]


def _norm_matmul_heads(x, g, w, name):
    n, d = x.shape
    tm = min(512, n)
    return pl.pallas_call(
        _norm_matmul_heads_kernel,
        grid=(n // tm,),
        in_specs=[pl.BlockSpec((tm, d), lambda i: (i, 0)),
                  pl.BlockSpec((1, d), lambda i: (0, 0)),
                  pl.BlockSpec((d, W_F), lambda i: (0, 0))],
        out_specs=[pl.BlockSpec((tm, W_F), lambda i: (i, 0)),
                   pl.BlockSpec((tm, H_F, DH_F), lambda i: (i, 0, 0))],
        out_shape=[jax.ShapeDtypeStruct((n, W_F), BF16),
                   jax.ShapeDtypeStruct((n, H_F, DH_F), F32)],
        compiler_params=_cparams("parallel"),
        name=name,
    )(x, g, w)


def _gate_kernel(x_ref, g_ref, w_ref, b_ref, o_ref):
    a = _rms(x_ref[...], g_ref[...])
    z = jnp.dot(a, w_ref[...], precision=HIGHEST, preferred_element_type=F32) + b_ref[...]
    col = lax.broadcasted_iota(jnp.int32, z.shape, 1)
    log_sig = jnp.minimum(z, 0.0) - jnp.log1p(jnp.exp(-jnp.abs(z)))
    o_ref[...] = jnp.where(col < COL_F, z, log_sig)


def _gates(x, g, w, b):
    n, d = x.shape
    tm = min(512, n)
    return pl.pallas_call(
        _gate_kernel,
        grid=(n // tm,),
        in_specs=[pl.BlockSpec((tm, d), lambda i: (i, 0)),
                  pl.BlockSpec((1, d), lambda i: (0, 0)),
                  pl.BlockSpec((d, GATE_COLS), lambda i: (0, 0)),
                  pl.BlockSpec((1, GATE_COLS), lambda i: (0, 0))],
        out_specs=pl.BlockSpec((tm, GATE_COLS), lambda i: (i, 0)),
        out_shape=jax.ShapeDtypeStruct((n, GATE_COLS), F32),
        compiler_params=_cparams("parallel"),
        name="gates",
    )(x, g, w, b)


def _cumsum_kernel(x_ref, o_ref, carry_ref):
    @pl.when(pl.program_id(1) == 0)
    def _():
        carry_ref[...] = jnp.zeros_like(carry_ref)

    x = x_ref[...]
    tb = x.shape[0]
    row = lax.broadcasted_iota(jnp.int32, (tb, tb), 0)
    col = lax.broadcasted_iota(jnp.int32, (tb, tb), 1)
    tril = jnp.where(col <= row, 1.0, 0.0).astype(F32)
    c = jnp.dot(tril, x, precision=HIGHEST, preferred_element_type=F32) + carry_ref[...]
    o_ref[...] = c
    carry_ref[...] = c[tb - 1:tb, :]


def _cumsum_tokens(x):
    b, t, w = x.shape
    tb = 256
    return pl.pallas_call(
        _cumsum_kernel,
        grid=(b, t // tb),
        in_specs=[pl.BlockSpec((None, tb, w), lambda i, j: (i, j, 0))],
        out_specs=pl.BlockSpec((None, tb, w), lambda i, j: (i, j, 0)),
        out_shape=jax.ShapeDtypeStruct((b, t, w), F32),
        scratch_shapes=[pltpu.VMEM((1, w), F32)],
        compiler_params=_cparams("parallel", "arbitrary"),
        name="cumsum",
    )(x)


def _mlstm_kernel(qkv_ref, og_ref, g_ref, c0_ref, n0_ref, m0_ref,
                  y_ref, cn_ref, nn_ref, mn_ref, c_s, n_s, m_s, *, bb_n, blk):
    step = pl.program_id(1)

    @pl.when(step == 0)
    def _():
        c_s[...] = c0_ref[...]
        n_s[...] = n0_ref[...]
        m_s[...] = m0_ref[...]

    row = lax.broadcasted_iota(jnp.int32, (blk, blk), 0)
    col = lax.broadcasted_iota(jnp.int32, (blk, blk), 1)
    tril = col <= row
    triu = row <= col
    eye = col == row

    def to_row(x_col):
        return jnp.sum(jnp.where(eye, x_col, 0.0), axis=0, keepdims=True)

    for bb in range(bb_n):
        g = g_ref[bb]
        for h in range(H_M):
            q = qkv_ref[bb, :, h * DH_M:(h + 1) * DH_M]
            k = qkv_ref[bb, :, (H_M + h) * DH_M:(H_M + h + 1) * DH_M] * (DH_M ** -0.5)
            v = qkv_ref[bb, :, (2 * H_M + h) * DH_M:(2 * H_M + h + 1) * DH_M]
            i_col = g[:, COL_I + h:COL_I + h + 1]
            f_col = g[:, COL_F + h:COL_F + h + 1]
            i_row = to_row(i_col)
            f_row = to_row(f_col)
            b_col = jnp.sum(jnp.where(tril, f_row, 0.0), axis=1, keepdims=True)
            b_row = jnp.sum(jnp.where(triu, f_col, 0.0), axis=0, keepdims=True)
            m_prev = m_s[bb, h][:, 0:1]
            dmat = jnp.where(tril, b_col - b_row + i_row, NEG_INF)
            g_col = b_col + m_prev
            mt = jnp.maximum(g_col, jnp.max(dmat, axis=1, keepdims=True))
            w_d = jnp.exp(dmat - mt)
            w_g = jnp.exp(g_col - mt)
            qk = lax.dot_general(q, k, (((1,), (1,)), ((), ())), preferred_element_type=F32) * w_d
            c_prev = c_s[bb, h]
            n_prev = n_s[bb, h]
            num = (w_g * jnp.dot(q, c_prev.astype(BF16), preferred_element_type=F32)
                   + jnp.dot(qk.astype(BF16), v, preferred_element_type=F32))
            den = (w_g * jnp.sum(q.astype(F32) * n_prev, axis=1, keepdims=True)
                   + jnp.sum(qk, axis=1, keepdims=True))
            hid = num / jnp.maximum(jnp.abs(den), jnp.exp(-mt))
            o_gate = _sigmoid(og_ref[bb, :, h * DH_M:(h + 1) * DH_M])
            y_ref[bb, :, h * DH_M:(h + 1) * DH_M] = (o_gate * hid).astype(y_ref.dtype)
            m_new = mt[blk - 1:blk, :]
            b_last = b_col[blk - 1:blk, :]
            w_c = jnp.exp(b_last + m_prev - m_new)
            w_s = jnp.exp(b_last - b_col + i_col - m_new)
            kw = k.astype(F32) * w_s
            c_s[bb, h] = w_c * c_prev + lax.dot_general(
                kw.astype(BF16), v, (((0,), (0,)), ((), ())), preferred_element_type=F32)
            n_s[bb, h] = w_c * n_prev + jnp.sum(kw, axis=0, keepdims=True)
            m_s[bb, h] = jnp.broadcast_to(m_new, (1, LANES))

    @pl.when(step == pl.num_programs(1) - 1)
    def _():
        cn_ref[...] = c_s[...]
        nn_ref[...] = n_s[...]
        mn_ref[...] = m_s[...]


def _mlstm(qkv, ogate, gates, c0, n0, m0):
    b, t, _ = qkv.shape
    blk = min(CHUNK, t)
    bb_n = 4
    state_spec = lambda shape: pl.BlockSpec((bb_n,) + shape, lambda i, j: (i, 0, 0, 0))
    return pl.pallas_call(
        functools.partial(_mlstm_kernel, bb_n=bb_n, blk=blk),
        grid=(b // bb_n, t // blk),
        in_specs=[pl.BlockSpec((bb_n, blk, 3 * W_M), lambda i, j: (i, j, 0)),
                  pl.BlockSpec((bb_n, blk, W_M), lambda i, j: (i, j, 0)),
                  pl.BlockSpec((bb_n, blk, GATE_COLS), lambda i, j: (i, j, 0)),
                  state_spec((H_M, DH_M, DH_M)),
                  state_spec((H_M, 1, DH_M)),
                  state_spec((H_M, 1, LANES))],
        out_specs=[pl.BlockSpec((bb_n, blk, W_M), lambda i, j: (i, j, 0)),
                   state_spec((H_M, DH_M, DH_M)),
                   state_spec((H_M, 1, DH_M)),
                   state_spec((H_M, 1, LANES))],
        out_shape=[jax.ShapeDtypeStruct((b, t, W_M), BF16),
                   jax.ShapeDtypeStruct((b, H_M, DH_M, DH_M), F32),
                   jax.ShapeDtypeStruct((b, H_M, 1, DH_M), F32),
                   jax.ShapeDtypeStruct((b, H_M, 1, LANES), F32)],
        scratch_shapes=[pltpu.VMEM((bb_n, H_M, DH_M, DH_M), F32),
                        pltpu.VMEM((bb_n, H_M, 1, DH_M), F32),
                        pltpu.VMEM((bb_n, H_M, 1, LANES), F32)],
        compiler_params=_cparams("parallel", "arbitrary"),
        name="mlstm",
    )(qkv, ogate, gates, c0, n0, m0)


def _online_softmax_step(s, pv, m_ref, l_ref, acc_ref):
    m_old = m_ref[...]
    m_new = jnp.maximum(m_old, jnp.max(s, axis=1, keepdims=True))
    alpha = jnp.exp(m_old - m_new)
    p = jnp.exp(s - m_new)
    l_ref[...] = alpha * l_ref[...] + jnp.sum(p, axis=1, keepdims=True)
    acc_ref[...] = alpha * acc_ref[...] + pv(p.astype(BF16))
    m_ref[...] = m_new


def _split3(x):
    hi = x.astype(BF16).astype(F32)
    mid = (x - hi).astype(BF16).astype(F32)
    lo = (x - hi - mid).astype(BF16).astype(F32)
    return hi, mid, lo


FOX_V_ROWS = DH_F + 16
AUG_CQ = 6


def _fox_prompt_kernel(q_ref, k_ref, v_ref, c_ref, after_ref, o_ref, kaug_s, vt_s, m0_s, m1_s, acc0_s,
                       acc1_s, *, tq, n_blocks):
    del after_ref
    hp = pl.program_id(1)
    qi = pl.program_id(2)
    lane = lax.broadcasted_iota(jnp.int32, (tq, LANES), 1)
    m_s, acc_s = (m0_s, m1_s), (acc0_s, acc1_s)

    def head_col(c_blk, x):
        return jnp.sum(jnp.where(lane == COL_FF + 2 * hp + x, c_blk, 0.0), axis=1, keepdims=True)

    def aug_tile(entries):
        tile = jnp.zeros((tq, LANES), F32)
        for l, val in entries:
            tile = jnp.where(lane == l, val, tile)
        return tile.astype(BF16)

    @pl.when(qi == 0)
    def _():
        ones_rows = jnp.where(lax.broadcasted_iota(jnp.int32, (FOX_V_ROWS - DH_F, tq), 0) == 0, 1.0, 0.0)

        def chunk(i, carry):
            rs = pl.multiple_of(i * tq, tq)
            c_blk = c_ref[pl.ds(rs, tq), :]
            pieces = _split3(head_col(c_blk, 0)) + _split3(head_col(c_blk, 1))
            entries = list(enumerate(pieces)) + [(AUG_CQ + j, 1.0) for j in range(3)]
            kaug_s[pl.ds(rs, tq), 0:LANES] = k_ref[pl.ds(rs, tq), :]
            kaug_s[pl.ds(rs, tq), LANES:2 * LANES] = aug_tile(entries)
            vt = v_ref[pl.ds(rs, tq), :].astype(F32).T
            for x in range(2):
                vt_s[x, 0:DH_F, pl.ds(rs, tq)] = vt[x * DH_F:(x + 1) * DH_F, :].astype(BF16)
                vt_s[x, DH_F:FOX_V_ROWS, pl.ds(rs, tq)] = ones_rows.astype(BF16)
            return carry

        lax.fori_loop(0, n_blocks, chunk, 0)

    qs = pl.multiple_of(qi * tq, tq)
    c_q = c_ref[pl.ds(qs, tq), :]
    q2 = q_ref[...] * (DH_F ** -0.5)
    q_aug = []
    for x in range(2):
        cq3 = _split3(head_col(c_q, x))
        entries = [(3 * x + j, -1.0) for j in range(3)] + [(AUG_CQ + j, cq3[j]) for j in range(3)]
        q_head = jnp.where((lane < DH_F) == (x == 0), q2, jnp.zeros_like(q2))
        q_aug.append(jnp.concatenate([q_head, aug_tile(entries)], axis=1))
        m_s[x][...] = jnp.full(m_s[x].shape, NEG_INF, F32)
        acc_s[x][...] = jnp.zeros(acc_s[x].shape, F32)

    def block(kb, masked):
        ks = pl.multiple_of(kb * tq, tq)
        k_blk = kaug_s[pl.ds(ks, tq), :]
        scores = [lax.dot_general(k_blk, q_aug[x], (((1,), (1,)), ((), ())), preferred_element_type=F32)
                  for x in range(2)]
        for x in range(2):
            st = scores[x]
            if masked:
                k_pos = lax.broadcasted_iota(jnp.int32, (tq, tq), 0)
                q_pos = lax.broadcasted_iota(jnp.int32, (tq, tq), 1)
                st = jnp.where(k_pos <= q_pos, st, NEG_INF)
            m_old = m_s[x][...]
            m_new = jnp.maximum(m_old, jnp.max(st, axis=0, keepdims=True))
            p = jnp.exp(st - m_new).astype(BF16)
            acc_s[x][...] = (jnp.exp(m_old - m_new) * acc_s[x][...]
                             + jnp.dot(vt_s[x, :, pl.ds(ks, tq)], p, preferred_element_type=F32))
            m_s[x][...] = m_new

    def body(kb, carry):
        block(kb, False)
        return carry

    lax.fori_loop(0, qi, body, 0)
    block(qi, True)
    out_t = jnp.concatenate([acc_s[x][0:DH_F, :] / acc_s[x][DH_F:DH_F + 1, :] for x in range(2)], axis=0)
    o_ref[...] = out_t.T.astype(o_ref.dtype)


def _fox_prompt(q, k, v, c, batch, after):
    _, t, _ = q.shape
    tq = 512
    head_pair = lambda i, h, j: (batch, 0, h)
    return pl.pallas_call(
        functools.partial(_fox_prompt_kernel, tq=tq, n_blocks=t // tq),
        grid=(1, H_F // 2, t // tq),
        in_specs=[pl.BlockSpec((None, tq, LANES), lambda i, h, j: (batch, j, h)),
                  pl.BlockSpec((None, t, LANES), head_pair),
                  pl.BlockSpec((None, t, LANES), head_pair),
                  pl.BlockSpec((None, t, LANES), lambda i, h, j: (batch, 0, 0)),
                  ORDERED],
        out_specs=pl.BlockSpec((tq, LANES), lambda i, h, j: (j, h)),
        out_shape=jax.ShapeDtypeStruct((t, W_F), BF16),
        scratch_shapes=[pltpu.VMEM((t, 2 * LANES), BF16),
                        pltpu.VMEM((2, FOX_V_ROWS, t), BF16),
                        pltpu.VMEM((1, tq), F32), pltpu.VMEM((1, tq), F32),
                        pltpu.VMEM((FOX_V_ROWS, tq), F32), pltpu.VMEM((FOX_V_ROWS, tq), F32)],
        compiler_params=_cparams("parallel", "parallel", "arbitrary"),
        name="fox_prompt",
    )(q, k, v, c, after)


def _fox_sample_kernel(q_ref, kc_ref, vc_ref, kn_ref, vn_ref, cq_ref, ctc_ref, ctn_ref, after_ref, o_ref,
                       m_s, l_s, acc_s, *, tn):
    del after_ref
    kb = pl.program_id(1)

    @pl.when(kb == 0)
    def _():
        m_s[...] = jnp.full_like(m_s, NEG_INF)
        l_s[...] = jnp.zeros_like(l_s)
        acc_s[...] = jnp.zeros_like(acc_s)

    lane = lax.broadcasted_iota(jnp.int32, (tn, LANES), 1)
    c_blk = cq_ref[...]
    nt = (((1,), (1,)), ((), ()))

    def head_query(hd):
        q = q_ref[:, hd * DH_F:(hd + 1) * DH_F] * (DH_F ** -0.5)
        cq = jnp.sum(jnp.where(lane == COL_FF + hd, c_blk, 0.0), axis=1, keepdims=True)
        return q, cq

    for hd in range(H_F):
        q, cq = head_query(hd)
        k_t = kc_ref[hd].astype(BF16)
        v_t = vc_ref[hd].astype(BF16)
        s = jnp.dot(q, k_t, preferred_element_type=F32) + cq - ctc_ref[hd // 2, hd % 2:hd % 2 + 1, :]
        _online_softmax_step(s, lambda p: lax.dot_general(p, v_t, nt, preferred_element_type=F32),
                             m_s.at[hd], l_s.at[hd], acc_s.at[hd])

    @pl.when(kb == pl.num_programs(1) - 1)
    def _():
        q_pos = lax.broadcasted_iota(jnp.int32, (tn, tn), 0)
        k_pos = lax.broadcasted_iota(jnp.int32, (tn, tn), 1)
        for hd in range(H_F):
            q, cq = head_query(hd)
            k_new = kn_ref[:, hd * DH_F:(hd + 1) * DH_F]
            v_new = vn_ref[:, hd * DH_F:(hd + 1) * DH_F]
            s = (lax.dot_general(q, k_new, nt, preferred_element_type=F32)
                 + cq - ctn_ref[hd // 2, hd % 2:hd % 2 + 1, :])
            s = jnp.where(k_pos <= q_pos, s, NEG_INF)
            _online_softmax_step(s, lambda p: jnp.dot(p, v_new, preferred_element_type=F32),
                                 m_s.at[hd], l_s.at[hd], acc_s.at[hd])
            o_ref[:, hd * DH_F:(hd + 1) * DH_F] = (acc_s[hd] / l_s[hd]).astype(o_ref.dtype)


def _fox_sample(q, k_cache_t, v_cache_t, k_new, v_new, cq, ct_cache, ct_new, after):
    b, tn, _ = q.shape
    p = k_cache_t.shape[3]
    tk = 1024
    new = pl.BlockSpec((None, tn, W_F), lambda i, j: (i, 0, 0))
    cache = pl.BlockSpec((None, H_F, DH_F, tk), lambda i, j: (i, 0, 0, j))
    return pl.pallas_call(
        functools.partial(_fox_sample_kernel, tn=tn),
        grid=(b, p // tk),
        in_specs=[new, cache, cache, new, new,
                  pl.BlockSpec((None, tn, LANES), lambda i, j: (i, 0, 0)),
                  pl.BlockSpec((None, H_F // 2, 2, tk), lambda i, j: (i, 0, 0, j)),
                  pl.BlockSpec((None, H_F // 2, 2, tn), lambda i, j: (i, 0, 0, 0)),
                  ORDERED],
        out_specs=new,
        out_shape=jax.ShapeDtypeStruct((b, tn, W_F), BF16),
        scratch_shapes=[pltpu.VMEM((H_F, tn, 1), F32),
                        pltpu.VMEM((H_F, tn, 1), F32),
                        pltpu.VMEM((H_F, tn, DH_F), F32)],
        compiler_params=_cparams("parallel", "arbitrary"),
        name="fox_sample",
    )(q, k_cache_t, v_cache_t, k_new, v_new, cq, ct_cache, ct_new, after)


def _merge_kernel(h_ref, ym_ref, yf_ref, gm_ref, gf_ref, wm_ref, wf_ref, wo_ref, after_ref, o_ref):
    del after_ref
    a = jnp.dot(ym_ref[...], wm_ref[...], preferred_element_type=F32)
    b = jnp.dot(yf_ref[...], wf_ref[...], preferred_element_type=F32)
    merge = _sigmoid(gm_ref[...]) * a + _sigmoid(gf_ref[...]) * b
    o_ref[...] = h_ref[...] + jnp.dot(merge.astype(BF16), wo_ref[...], preferred_element_type=F32)


def _merge(h, ym, yf, gates, wm, wf, wo, first_token, after):
    n, d = yf.shape
    tm = min(512, n)
    first = first_token // tm
    tok = lambda c: pl.BlockSpec((tm, d), lambda i: (first + i, c))
    local = pl.BlockSpec((tm, d), lambda i: (i, 0))
    wspec = pl.BlockSpec((d, d), lambda i: (0, 0))
    return pl.pallas_call(
        _merge_kernel,
        grid=(n // tm,),
        in_specs=[tok(0), tok(0), local, tok(1), tok(2), wspec, wspec, wspec, ORDERED],
        out_specs=local,
        out_shape=jax.ShapeDtypeStruct((n, d), F32),
        compiler_params=_cparams("parallel"),
        name="merge",
    )(h, ym, yf, gates, gates, wm, wf, wo, after)


def _top_rows(s, count, ids=None, payload=None):
    if ids is None:
        ids = lax.broadcasted_iota(jnp.int32, s.shape, 0)
    big = jnp.int32(2 ** 30)
    vals, sel, pay = [], [], []
    for _ in range(count):
        m = jnp.max(s, axis=0, keepdims=True)
        am = jnp.min(jnp.where(s == m, ids, big), axis=0, keepdims=True)
        hit = ids == am
        vals.append(m)
        sel.append(am)
        if payload is not None:
            pay.append(jnp.max(jnp.where(hit, payload, -1), axis=0, keepdims=True))
        s = jnp.where(hit, NEG_INF, s)
    cat = lambda xs: jnp.concatenate(xs, axis=0)
    return cat(vals), cat(sel), (cat(pay) if payload is not None else None)


def _pair_candidates(v1, i1, v2, i2):
    t = v1.shape[1]
    half = PEER_TOPK // 2
    r16 = lax.broadcasted_iota(jnp.int32, (PEER_TOPK, t), 0)
    r8 = lax.broadcasted_iota(jnp.int32, (half, t), 0)
    sums = [v1[0:1, :] + v2]
    flat = [r16]
    expert = [i1[0:1, :] * PEER_KEYS + i2]
    for a in range(1, half):
        sums.append(v1[a:a + 1, :] + v2[0:half, :])
        flat.append(r8 + a * PEER_TOPK)
        expert.append(i1[a:a + 1, :] * PEER_KEYS + i2[0:half, :])
    sums.append(v1[half:, :] + v2[0:1, :])
    flat.append((r8 + half) * PEER_TOPK)
    expert.append(i1[half:, :] * PEER_KEYS + i2[0:1, :])
    cat = lambda xs: jnp.concatenate(xs, axis=0)
    return cat(sums), cat(flat), cat(expert)


def _peer_score_kernel(h_ref, g_ref, wqt_ref, k1_ref, k2_ref, xn_ref, idx_ref, gate_ref):
    xn = _rms(h_ref[...], g_ref[...])
    xn_ref[...] = xn.astype(xn_ref.dtype)
    qt = lax.dot_general(wqt_ref[...], xn, (((1,), (1,)), ((), ())),
                         precision=HIGHEST, preferred_element_type=F32)
    half = PEER_DQ // 2
    idx_rows, gate_rows = [], []
    for hd in range(PEER_HEADS):
        q1 = qt[hd * PEER_DQ:hd * PEER_DQ + half, :]
        q2 = qt[hd * PEER_DQ + half:(hd + 1) * PEER_DQ, :]
        s1 = jnp.dot(k1_ref[hd], q1, precision=HIGHEST, preferred_element_type=F32)
        s2 = jnp.dot(k2_ref[hd], q2, precision=HIGHEST, preferred_element_type=F32)
        v1, i1, _ = _top_rows(s1, PEER_TOPK)
        v2, i2, _ = _top_rows(s2, PEER_TOPK)
        cand, flat, expert = _pair_candidates(v1, i1, v2, i2)
        sc, _, ex = _top_rows(cand, PEER_TOPK, ids=flat, payload=expert)
        e = jnp.exp(sc - sc[0:1, :])
        gate_rows.append(e / jnp.sum(e, axis=0, keepdims=True))
        idx_rows.append(ex)
    gate_ref[...] = jnp.concatenate(gate_rows, axis=0).T
    idx_ref[...] = jnp.concatenate(idx_rows, axis=0).T


def _peer_score(h, g, wqt, k1, k2):
    n, d = h.shape
    tt = min(256, n)
    return pl.pallas_call(
        _peer_score_kernel,
        grid=(n // tt,),
        in_specs=[pl.BlockSpec((tt, d), lambda i: (i, 0)),
                  pl.BlockSpec((1, d), lambda i: (0, 0)),
                  pl.BlockSpec(wqt.shape, lambda i: (0, 0)),
                  pl.BlockSpec(k1.shape, lambda i: (0, 0, 0)),
                  pl.BlockSpec(k2.shape, lambda i: (0, 0, 0))],
        out_specs=[pl.BlockSpec((tt, d), lambda i: (i, 0)),
                   pl.BlockSpec((tt, PEER_SEL), lambda i: (i, 0)),
                   pl.BlockSpec((tt, PEER_SEL), lambda i: (i, 0))],
        out_shape=[jax.ShapeDtypeStruct((n, d), F32),
                   jax.ShapeDtypeStruct((n, PEER_SEL), jnp.int32),
                   jax.ShapeDtypeStruct((n, PEER_SEL), F32)],
        compiler_params=_cparams("parallel"),
        name="peer_score",
    )(h, g, wqt, k1, k2)


SC_CORES = 2
SC_SUBCORES = 16
SC_WORKERS = SC_CORES * SC_SUBCORES
SC_ROWS = 64


def _pack_table(tab):
    half = tab.shape[1] // 2
    bits = lax.bitcast_convert_type(tab.astype(BF16), jnp.uint16).astype(jnp.uint32)
    word = bits[:, :half] | (bits[:, half:] << 16)
    return lax.bitcast_convert_type(word, jnp.int32)


SC_TOK = 8
SC_LANES = 16
STEPS_PER_TOKEN = PEER_SEL // SC_ROWS
SC_PARAMS = pltpu.CompilerParams(needs_layout_passes=False)


def _row_source(table_hbm, idx_v, local_step, global_step):
    del global_step
    return table_hbm.at[idx_v.at[local_step]]


def _sc_unpack(wd):
    lo = lax.bitcast_convert_type(wd << 16, F32)
    hi = lax.bitcast_convert_type(wd & jnp.int32(-65536), F32)
    return lo, hi


def _sc_token_blocks(n, body_block):
    per_worker = n // SC_WORKERS
    assert per_worker * SC_WORKERS == n and per_worker % SC_TOK == 0
    wid = lax.axis_index("s") * SC_CORES + lax.axis_index("c")

    @pl.loop(0, per_worker // SC_TOK)
    def _(blk):
        body_block(wid * per_worker + blk * SC_TOK)


def _sc_pipelined_steps(table_hbm, idx_v, rows_v, sems, first_step, consume):
    n_steps = SC_TOK * STEPS_PER_TOKEN

    def row_gather(j, slot):
        return pltpu.make_async_copy(_row_source(table_hbm, idx_v, j, first_step + j),
                                     rows_v.at[slot], sems.at[slot])

    row_gather(0, 0).start()

    @pl.loop(0, n_steps, step=2)
    def _(i):
        for slot in range(2):
            j = i + slot

            @pl.when(j + 1 < n_steps)
            def _():
                row_gather(j + 1, 1 - slot).start()

            row_gather(j, slot).wait()
            consume(slot, i // 2, slot)


def _sc_expert_dot(table, idx_steps, x):
    n, d = x.shape
    w = d // 2
    assert STEPS_PER_TOKEN == 2 and table.shape[1] == w
    mesh = plsc.VectorSubcoreMesh(core_axis_name="c", subcore_axis_name="s")
    n_steps = SC_TOK * STEPS_PER_TOKEN
    group = 4

    @functools.partial(
        pl.kernel, mesh=mesh,
        out_type=jax.ShapeDtypeStruct((n, PEER_SEL), F32),
        scratch_types=[pltpu.VMEM((n_steps, SC_ROWS), jnp.int32),
                       pltpu.VMEM((SC_TOK, d), F32),
                       pltpu.VMEM((2, SC_ROWS, w), jnp.int32),
                       pltpu.VMEM((SC_TOK, PEER_SEL), F32),
                       pltpu.SemaphoreType.DMA((2,))],
        compiler_params=SC_PARAMS,
    )
    def expert_dot(table_hbm, idx_hbm, x_hbm, act_hbm, idx_v, x_v, rows_v, act_v, sems):
        lanes = lax.iota(jnp.int32, SC_LANES)
        zero = jnp.zeros((SC_LANES,), F32)

        def consume(slot, tl, half):
            @pl.loop(0, SC_ROWS // SC_LANES)
            def _(g):
                act_vec = zero
                for q in range(SC_LANES // group):
                    r0 = g * SC_LANES + q * group

                    def chunk(j, accs):
                        c = pl.multiple_of(j * SC_LANES, SC_LANES)
                        x_lo = x_v[tl, pl.ds(c, SC_LANES)]
                        x_hi = x_v[tl, pl.ds(w + c, SC_LANES)]
                        out = []
                        for rr in range(group):
                            lo, hi = _sc_unpack(rows_v[slot, r0 + rr, pl.ds(c, SC_LANES)])
                            out.append(accs[rr] + lo * x_lo + hi * x_hi)
                        return tuple(out)

                    accs = lax.fori_loop(0, w // SC_LANES, chunk, (zero,) * group, unroll=2)
                    for rr in range(group):
                        act_vec = jnp.where(lanes == q * group + rr, jnp.sum(accs[rr]), act_vec)
                act_v[tl, pl.ds(half * SC_ROWS + g * SC_LANES, SC_LANES)] = act_vec

        def block(t0):
            s0 = t0 * STEPS_PER_TOKEN
            pltpu.sync_copy(idx_hbm.at[pl.ds(s0, n_steps)], idx_v)
            pltpu.sync_copy(x_hbm.at[pl.ds(t0, SC_TOK)], x_v)
            _sc_pipelined_steps(table_hbm, idx_v, rows_v, sems, s0, consume)
            pltpu.sync_copy(act_v, act_hbm.at[pl.ds(t0, SC_TOK)])

        _sc_token_blocks(n, block)

    return expert_dot(table, idx_steps, x)


def _sc_expert_sum(table, idx_steps, wgt):
    n = wgt.shape[0]
    w = table.shape[1]
    d = 2 * w
    assert STEPS_PER_TOKEN == 2
    mesh = plsc.VectorSubcoreMesh(core_axis_name="c", subcore_axis_name="s")
    n_steps = SC_TOK * STEPS_PER_TOKEN
    cols = 8

    @functools.partial(
        pl.kernel, mesh=mesh,
        out_type=jax.ShapeDtypeStruct((n, d), F32),
        scratch_types=[pltpu.VMEM((n_steps, SC_ROWS), jnp.int32),
                       pltpu.VMEM((SC_TOK, PEER_SEL), F32),
                       pltpu.VMEM((2, SC_ROWS, w), jnp.int32),
                       pltpu.VMEM((SC_TOK, d), F32),
                       pltpu.VMEM((SC_ROWS, SC_LANES), F32),
                       pltpu.SemaphoreType.DMA((2,))],
        compiler_params=SC_PARAMS,
    )
    def expert_sum(table_hbm, idx_hbm, wgt_hbm, out_hbm, idx_v, wgt_v, rows_v, out_v, splat_v, sems):
        zero = jnp.zeros((SC_LANES,), F32)
        lanes = lax.iota(jnp.int32, SC_LANES)

        def consume(slot, tl, half):
            @pl.loop(0, SC_ROWS // SC_LANES)
            def _(g):
                w16 = wgt_v[tl, pl.ds(half * SC_ROWS + g * SC_LANES, SC_LANES)]
                for rr in range(SC_LANES):
                    one = jnp.sum(jnp.where(lanes == rr, w16, 0.0))
                    splat_v[g * SC_LANES + rr, :] = jnp.full((SC_LANES,), one, F32)

            for cb in range(w // (cols * SC_LANES)):
                base = cb * cols * SC_LANES

                def row(r, accs):
                    wv = splat_v[r, :]
                    out = []
                    for jj in range(cols):
                        lo, hi = _sc_unpack(rows_v[slot, r, pl.ds(base + jj * SC_LANES, SC_LANES)])
                        out.append(accs[2 * jj] + wv * lo)
                        out.append(accs[2 * jj + 1] + wv * hi)
                    return tuple(out)

                accs = lax.fori_loop(0, SC_ROWS, row, (zero,) * (2 * cols), unroll=2)
                for jj in range(cols):
                    c = base + jj * SC_LANES
                    if half == 0:
                        out_v[tl, pl.ds(c, SC_LANES)] = accs[2 * jj]
                        out_v[tl, pl.ds(w + c, SC_LANES)] = accs[2 * jj + 1]
                    else:
                        out_v[tl, pl.ds(c, SC_LANES)] = out_v[tl, pl.ds(c, SC_LANES)] + accs[2 * jj]
                        out_v[tl, pl.ds(w + c, SC_LANES)] = out_v[tl, pl.ds(w + c, SC_LANES)] + accs[2 * jj + 1]

        def block(t0):
            s0 = t0 * STEPS_PER_TOKEN
            pltpu.sync_copy(idx_hbm.at[pl.ds(s0, n_steps)], idx_v)
            pltpu.sync_copy(wgt_hbm.at[pl.ds(t0, SC_TOK)], wgt_v)
            _sc_pipelined_steps(table_hbm, idx_v, rows_v, sems, s0, consume)
            pltpu.sync_copy(out_v, out_hbm.at[pl.ds(t0, SC_TOK)])

        _sc_token_blocks(n, block)

    return expert_sum(table, idx_steps, wgt)


def _gelu_exact(x):
    return 0.5 * x * (1.0 + lax.erf(x * (2.0 ** -0.5)))


def _peer_weight_kernel(act_ref, gate_ref, after_ref, o_ref):
    del after_ref
    o_ref[...] = gate_ref[...] * _gelu_exact(act_ref[...])


def _peer_weight(act, gate, after):
    n = act.shape[0]
    tt = min(2048, n)
    spec = pl.BlockSpec((tt, PEER_SEL), lambda i: (i, 0))
    return pl.pallas_call(
        _peer_weight_kernel,
        grid=(n // tt,),
        in_specs=[spec, spec, ORDERED],
        out_specs=spec,
        out_shape=jax.ShapeDtypeStruct((n, PEER_SEL), F32),
        compiler_params=_cparams("parallel"),
        name="peer_weight",
    )(act, gate, after)


def _ple_kernel(h_ref, peer_ref, p_ref, gp_ref, gfin_ref, wg_ref, wp_ref, o_ref):
    h = h_ref[...] + peer_ref[...]
    e = _rms(h, gp_ref[...]).astype(BF16)
    gate = _sigmoid(jnp.dot(e, wg_ref[...], preferred_element_type=F32))
    proj = jnp.dot(p_ref[...].astype(BF16), wp_ref[...], preferred_element_type=F32)
    o_ref[...] = _rms(h + gate * proj, gfin_ref[...])


def _ple_final(h, peer, p, g_ple, g_final, wg, wp, first_token):
    n, d = h.shape
    tm = min(512, n)
    first = first_token // tm
    return pl.pallas_call(
        _ple_kernel,
        grid=(n // tm,),
        in_specs=[pl.BlockSpec((tm, d), lambda i: (i, 0)),
                  pl.BlockSpec((tm, d), lambda i: (i, 0)),
                  pl.BlockSpec((tm, D_PLE), lambda i: (first + i, 0)),
                  pl.BlockSpec((1, d), lambda i: (0, 0)),
                  pl.BlockSpec((1, d), lambda i: (0, 0)),
                  pl.BlockSpec((d, d), lambda i: (0, 0)),
                  pl.BlockSpec((D_PLE, d), lambda i: (0, 0))],
        out_specs=pl.BlockSpec((tm, d), lambda i: (i, 0)),
        out_shape=jax.ShapeDtypeStruct((n, d), F32),
        compiler_params=_cparams("parallel"),
        name="ple_final",
    )(h, peer, p, g_ple, g_final, wg, wp)


def _prep_weights(norm_mix, w_in, mlstm_b_i, mlstm_b_f, fox_b_f, w_br_m, w_br_f, w_out, norm_ffn,
                  peer_w_q, peer_keys1, peer_keys2, peer_u, peer_v, norm_ple, w_ple_gate, w_ple_proj,
                  norm_final):
    o = [0]
    for s in (W_M, W_M, W_M, W_M, H_M, H_M, W_F, W_F, W_F, H_F, D_MODEL, D_MODEL):
        o.append(o[-1] + s)
    seg = lambda a, b: w_in[:, o[a]:o[b]]
    w_gate = jnp.concatenate([seg(4, 6), seg(9, 10)], axis=1)
    w_gate = jnp.pad(w_gate, ((0, 0), (0, GATE_COLS - w_gate.shape[1])))
    b_gate = jnp.concatenate([mlstm_b_i, mlstm_b_f, fox_b_f]).astype(F32)
    b_gate = jnp.pad(b_gate, (0, GATE_COLS - b_gate.shape[0]))[None, :]
    row = lambda v: v.astype(F32)[None, :]
    return dict(
        norm_mix=row(norm_mix),
        w_mqkv=seg(0, 3).astype(BF16),
        w_og=jnp.concatenate([seg(3, 4), seg(10, 12)], axis=1).astype(BF16),
        w_fq=seg(6, 7).astype(BF16), w_fk=seg(7, 8).astype(BF16), w_fv=seg(8, 9).astype(BF16),
        w_gate=w_gate.astype(F32), b_gate=b_gate,
        w_br_m=w_br_m.astype(BF16), w_br_f=w_br_f.astype(BF16), w_out=w_out.astype(BF16),
        norm_ffn=row(norm_ffn), wqt=peer_w_q.T.astype(F32),
        k1=peer_keys1.astype(F32), k2=peer_keys2.astype(F32),
        u_pk=_pack_table(peer_u), v_pk=_pack_table(peer_v),
        norm_ple=row(norm_ple), w_ple_gate=w_ple_gate.astype(BF16), w_ple_proj=w_ple_proj.astype(BF16),
        norm_final=row(norm_final),
    )


def _ct_rows(c):
    b, t, _ = c.shape
    return jnp.swapaxes(c[:, :, COL_FF:COL_FF + H_F], 1, 2).reshape(b, H_F // 2, 2, t)


def _peer_first_pass(h, w):
    n = h.shape[0]
    xn, idx, gate = _peer_score(h, w["norm_ffn"], w["wqt"], w["k1"], w["k2"])
    steps = idx.reshape(n * STEPS_PER_TOKEN, SC_ROWS)
    return steps, gate, _sc_expert_dot(w["u_pk"], steps, xn)


def _peer_second_pass(first, w, after):
    steps, gate, act = first
    wgt = _peer_weight(act, gate, after)
    return _sc_expert_sum(w["v_pk"], steps, wgt), wgt


def _layer(x, p, w, mstate, fox_cache, after):
    b, t, d = x.shape
    n = b * t
    h = x.reshape(n, d)
    g = w["norm_mix"]
    qkv = _norm_matmul(h, g, w["w_mqkv"], BF16, "proj_mlstm_qkv")
    og = _norm_matmul(h, g, w["w_og"], F32, "proj_gates")
    fq = _norm_matmul(h, g, w["w_fq"], BF16, "proj_fox_q")
    fk, fk_heads = _norm_matmul_heads(h, g, w["w_fk"], "proj_fox_k")
    fv, fv_heads = _norm_matmul_heads(h, g, w["w_fv"], "proj_fox_v")
    gates = _gates(h, g, w["w_gate"], w["b_gate"])

    c0, n0, m0 = mstate
    ym, c_new, n_new, m_new = _mlstm(
        qkv.reshape(b, t, 3 * W_M), og.reshape(b, t, 3 * D_MODEL), gates.reshape(b, t, GATE_COLS),
        c0.astype(F32), n0.astype(F32)[:, :, None, :],
        jnp.broadcast_to(m0.astype(F32)[:, :, None, None], (b, H_M, 1, LANES)))

    def merged(yf, first_token, after):
        return _merge(h, ym.reshape(n, W_M), yf, og, w["w_br_m"], w["w_br_f"], w["w_out"], first_token, after)

    def output(hm, peer, first_token):
        return _ple_final(hm, peer, p.reshape(n, D_PLE), w["norm_ple"], w["norm_final"],
                          w["w_ple_gate"], w["w_ple_proj"], first_token)

    neutral = w["norm_ffn"]
    gates3 = gates.reshape(b, t, GATE_COLS)
    if fox_cache is None:
        c = _cumsum_tokens(gates3)
        seqs = (fq.reshape(b, t, W_F), fk.reshape(b, t, W_F), fv.reshape(b, t, W_F))
        hms, firsts, peers, wgts = [], [], [], []
        for bi in range(b):
            yf = _fox_prompt(*seqs, c, bi, peers[bi - 2] if bi >= 2 else neutral)
            if bi >= 1:
                peer, wgt = _peer_second_pass(firsts[bi - 1], w, yf)
                peers.append(peer)
                wgts.append(wgt)
            hms.append(merged(yf, bi * t, wgts[bi - 1] if bi >= 1 else neutral))
            firsts.append(_peer_first_pass(hms[bi], w))
        peer, wgt = _peer_second_pass(firsts[b - 1], w, neutral)
        peers.append(peer)
        wgts.append(wgt)
        y = jnp.concatenate([output(hms[bi], peers[bi], bi * t) for bi in range(b)], axis=0)
        tail = (peers[-2], firsts[-1][1], wgts[-1], peers[-1])
    else:
        ck, cv, clf = fox_cache
        past = ck.shape[1]
        lf = jnp.pad(clf.astype(F32), ((0, 0), (0, 0), (COL_FF, GATE_COLS - COL_FF - H_F)))
        lf = jnp.concatenate([lf, gates3], axis=1)
        pad_t = (-lf.shape[1]) % 256
        c = _cumsum_tokens(jnp.pad(lf, ((0, 0), (0, pad_t), (0, 0))))
        ct = _ct_rows(c)
        keys_minor = lambda cache: jnp.transpose(cache.astype(F32), (0, 2, 3, 1))
        yf = _fox_sample(fq.reshape(b, t, W_F), keys_minor(ck), keys_minor(cv),
                         fk.reshape(b, t, W_F), fv.reshape(b, t, W_F),
                         c[:, past:past + t, :], ct[..., :past], ct[..., past:past + t], after[0])
        hm = merged(yf.reshape(n, W_F), 0, after[1])
        peer, wgt = _peer_second_pass(_peer_first_pass(hm, w), w, neutral)
        tail = (peer, wgt, wgt, peer)
        y = output(hm, peer, 0)

    state = (fk_heads.reshape(1, b, t, H_F, DH_F), fv_heads.reshape(1, b, t, H_F, DH_F),
             gates3[None, :, :, COL_FF:COL_FF + H_F],
             c_new[None], n_new[None, :, :, 0, :], m_new[None, :, :, 0, 0])
    return y.reshape(b, t, d), state, tail


def kernel(x_prompt, x_sample, p_prompt, p_sample, cache_fox_k, cache_fox_v, cache_fox_logf, state_mlstm_C, state_mlstm_n, state_mlstm_m, norm_mix, w_in, mlstm_b_i, mlstm_b_f, fox_b_f, w_br_m, w_br_f, w_out, norm_ffn, peer_w_q, peer_keys1, peer_keys2, peer_u, peer_v, norm_ple, w_ple_gate, w_ple_proj, norm_final):
    assert w_in.shape[0] == 1, "single-layer trunk"
    w = _prep_weights(norm_mix[0], w_in[0], mlstm_b_i[0], mlstm_b_f[0], fox_b_f[0], w_br_m[0], w_br_f[0],
                      w_out[0], norm_ffn[0], peer_w_q[0], peer_keys1[0], peer_keys2[0], peer_u[0], peer_v[0],
                      norm_ple[0], w_ple_gate[0], w_ple_proj[0], norm_final)
    bp = x_prompt.shape[0]
    zeros = (jnp.zeros((bp, H_M, DH_M, DH_M), F32), jnp.zeros((bp, H_M, DH_M), F32), jnp.zeros((bp, H_M), F32))
    y_p, sp, tail = _layer(x_prompt, p_prompt[0], w, zeros, None, None)
    bs, ts, d = x_sample.shape
    x_s = _ordered(x_sample.reshape(bs * ts, d), tail[0], tail[1]).reshape(bs, ts, d)
    y_s, ss, _ = _layer(x_s, p_sample[0], w,
                        (state_mlstm_C[0], state_mlstm_n[0], state_mlstm_m[0]),
                        (cache_fox_k[0], cache_fox_v[0], cache_fox_logf[0]), tail[2:])
    return (y_p, y_s) + sp + ss
```

```python
import functools

import jax
import jax.numpy as jnp
from jax import lax
from jax.experimental import pallas as pl
from jax.experimental.pallas import tpu as pltpu
from jax.experimental.pallas import tpu_sc as plsc

D_MODEL = 1024
CHUNK = 64
H_M = 4
DH_M = 256
W_M = H_M * DH_M
H_F = 16
DH_F = 64
W_F = H_F * DH_F
D_PLE = 256
PEER_HEADS = 8
PEER_KEYS = 128
PEER_DQ = 256
PEER_TOPK = 16
PEER_SEL = PEER_HEADS * PEER_TOPK
EPS = 1e-6

LANES = 128
GATE_COLS = LANES
COL_I, COL_F, COL_FF = 0, H_M, 2 * H_M
VMEM_LIMIT = 56 * 1024 * 1024
HIGHEST = lax.Precision.HIGHEST
F32 = jnp.float32
BF16 = jnp.bfloat16
NEG_INF = float("-inf")


def _cparams(*sem):
    return pltpu.CompilerParams(dimension_semantics=sem, vmem_limit_bytes=VMEM_LIMIT)


ORDERED = pl.BlockSpec(memory_space=pl.ANY)


def _ordered_kernel(x_ref, after_a, after_b, o_ref):
    del after_a, after_b
    o_ref[...] = x_ref[...]


def _ordered(x, after_a, after_b):
    n, d = x.shape
    spec = pl.BlockSpec((n, d), lambda i: (0, 0))
    return pl.pallas_call(
        _ordered_kernel, grid=(1,), in_specs=[spec, ORDERED, ORDERED], out_specs=spec,
        out_shape=jax.ShapeDtypeStruct((n, d), x.dtype),
        compiler_params=_cparams("arbitrary"), name="ordered_copy",
    )(x, after_a, after_b)


def _rms(x, g):
    return x * lax.rsqrt(jnp.mean(x * x, axis=-1, keepdims=True) + EPS) * g


def _sigmoid(x):
    return 1.0 / (1.0 + jnp.exp(-x))


def _norm_matmul_kernel(x_ref, g_ref, w_ref, o_ref, a_ref):
    @pl.when(pl.program_id(1) == 0)
    def _():
        a_ref[...] = _rms(x_ref[...], g_ref[...]).astype(a_ref.dtype)

    o_ref[...] = jnp.dot(a_ref[...], w_ref[...], preferred_element_type=F32).astype(o_ref.dtype)


def _norm_matmul(x, g, w, out_dtype, name):
    n, d = x.shape
    cols = w.shape[1]
    tm = min(1024, n)
    tn = 1024
    return pl.pallas_call(
        _norm_matmul_kernel,
        grid=(n // tm, cols // tn),
        in_specs=[pl.BlockSpec((tm, d), lambda i, j: (i, 0)),
                  pl.BlockSpec((1, d), lambda i, j: (0, 0)),
                  pl.BlockSpec((d, tn), lambda i, j: (0, j))],
        out_specs=pl.BlockSpec((tm, tn), lambda i, j: (i, j)),
        out_shape=jax.ShapeDtypeStruct((n, cols), out_dtype),
        scratch_shapes=[pltpu.VMEM((tm, d), BF16)],
        compiler_params=_cparams("parallel", "arbitrary"),
        name=name,
    )(x, g, w)


def _norm_matmul_heads_kernel(x_ref, g_ref, w_ref, o_ref, oh_ref):
    a = _rms(x_ref[...], g_ref[...]).astype(BF16)
    z = jnp.dot(a, w_ref[...], preferred_element_type=F32)
    o_ref[...] = z.astype(o_ref.dtype)
    for h in range(H_F):
        oh_ref[:, h, :] = z[:, h * DH_F:(h + 1) * DH_F]


def _norm_matmul_tokens_minor_kernel(x_ref, g_ref, w_ref, o_ref, ot_ref):
    a = _rms(x_ref[...], g_ref[...]).astype(BF16)
    z = jnp.dot(a, w_ref[...], preferred_element_type=F32)
    o_ref[...] = z.astype(o_ref.dtype)
    ot_ref[...] = z.T.reshape(ot_ref.shape)


def _norm_matmul_tokens_minor(x, g, w, seq_len, name):
    n, d = x.shape
    tm = 512
    per_seq = seq_len // tm
    assert per_seq * tm == seq_len and n % seq_len == 0
    return pl.pallas_call(
        _norm_matmul_tokens_minor_kernel,
        grid=(n // tm,),
        in_specs=[pl.BlockSpec((tm, d), lambda i: (i, 0)),
                  pl.BlockSpec((1, d), lambda i: (0, 0)),
                  pl.BlockSpec((d, W_F), lambda i: (0, 0))],
        out_specs=[pl.BlockSpec((tm, W_F), lambda i: (i, 0)),
                   pl.BlockSpec((None, H_F, DH_F, tm), lambda i: (i // per_seq, 0, 0, i % per_seq))],
        out_shape=[jax.ShapeDtypeStruct((n, W_F), BF16),
                   jax.ShapeDtypeStruct((n // seq_len, H_F, DH_F, seq_len), F32)],
        compiler_params=_cparams("parallel"),
        name=name,
    )(x, g, w)


def _norm_matmul_heads(x, g, w, name):
    n, d = x.shape
    tm = min(512, n)
    return pl.pallas_call(
        _norm_matmul_heads_kernel,
        grid=(n // tm,),
        in_specs=[pl.BlockSpec((tm, d), lambda i: (i, 0)),
                  pl.BlockSpec((1, d), lambda i: (0, 0)),
                  pl.BlockSpec((d, W_F), lambda i: (0, 0))],
        out_specs=[pl.BlockSpec((tm, W_F), lambda i: (i, 0)),
                   pl.BlockSpec((tm, H_F, DH_F), lambda i: (i, 0, 0))],
        out_shape=[jax.ShapeDtypeStruct((n, W_F), BF16),
                   jax.ShapeDtypeStruct((n, H_F, DH_F), F32)],
        compiler_params=_cparams("parallel"),
        name=name,
    )(x, g, w)


def _gate_kernel(x_ref, g_ref, w_ref, b_ref, o_ref):
    a = _rms(x_ref[...], g_ref[...])
    z = jnp.dot(a, w_ref[...], precision=HIGHEST, preferred_element_type=F32) + b_ref[...]
    col = lax.broadcasted_iota(jnp.int32, z.shape, 1)
    log_sig = jnp.minimum(z, 0.0) - jnp.log1p(jnp.exp(-jnp.abs(z)))
    o_ref[...] = jnp.where(col < COL_F, z, log_sig)


def _gates(x, g, w, b):
    n, d = x.shape
    tm = min(512, n)
    return pl.pallas_call(
        _gate_kernel,
        grid=(n // tm,),
        in_specs=[pl.BlockSpec((tm, d), lambda i: (i, 0)),
                  pl.BlockSpec((1, d), lambda i: (0, 0)),
                  pl.BlockSpec((d, GATE_COLS), lambda i: (0, 0)),
                  pl.BlockSpec((1, GATE_COLS), lambda i: (0, 0))],
        out_specs=pl.BlockSpec((tm, GATE_COLS), lambda i: (i, 0)),
        out_shape=jax.ShapeDtypeStruct((n, GATE_COLS), F32),
        compiler_params=_cparams("parallel"),
        name="gates",
    )(x, g, w, b)


def _cumsum_kernel(x_ref, o_ref, carry_ref):
    @pl.when(pl.program_id(1) == 0)
    def _():
        carry_ref[...] = jnp.zeros_like(carry_ref)

    x = x_ref[...]
    tb = x.shape[0]
    row = lax.broadcasted_iota(jnp.int32, (tb, tb), 0)
    col = lax.broadcasted_iota(jnp.int32, (tb, tb), 1)
    tril = jnp.where(col <= row, 1.0, 0.0).astype(F32)
    c = jnp.dot(tril, x, precision=HIGHEST, preferred_element_type=F32) + carry_ref[...]
    o_ref[...] = c
    carry_ref[...] = c[tb - 1:tb, :]


def _cumsum_tokens(x):
    b, t, w = x.shape
    tb = 256
    return pl.pallas_call(
        _cumsum_kernel,
        grid=(b, t // tb),
        in_specs=[pl.BlockSpec((None, tb, w), lambda i, j: (i, j, 0))],
        out_specs=pl.BlockSpec((None, tb, w), lambda i, j: (i, j, 0)),
        out_shape=jax.ShapeDtypeStruct((b, t, w), F32),
        scratch_shapes=[pltpu.VMEM((1, w), F32)],
        compiler_params=_cparams("parallel", "arbitrary"),
        name="cumsum",
    )(x)


def _mlstm_kernel(qkv_ref, og_ref, g_ref, c0_ref, n0_ref, m0_ref,
                  y_ref, cn_ref, nn_ref, mn_ref, c_s, n_s, m_s, *, bb_n, blk):
    step = pl.program_id(1)

    @pl.when(step == 0)
    def _():
        c_s[...] = c0_ref[...]
        n_s[...] = n0_ref[...]
        m_s[...] = m0_ref[...]

    row = lax.broadcasted_iota(jnp.int32, (blk, blk), 0)
    col = lax.broadcasted_iota(jnp.int32, (blk, blk), 1)
    tril = col <= row
    triu = row <= col
    eye = col == row

    def to_row(x_col):
        return jnp.sum(jnp.where(eye, x_col, 0.0), axis=0, keepdims=True)

    for bb in range(bb_n):
        g = g_ref[bb]
        for h in range(H_M):
            q = qkv_ref[bb, :, h * DH_M:(h + 1) * DH_M]
            k = qkv_ref[bb, :, (H_M + h) * DH_M:(H_M + h + 1) * DH_M] * (DH_M ** -0.5)
            v = qkv_ref[bb, :, (2 * H_M + h) * DH_M:(2 * H_M + h + 1) * DH_M]
            i_col = g[:, COL_I + h:COL_I + h + 1]
            f_col = g[:, COL_F + h:COL_F + h + 1]
            i_row = to_row(i_col)
            f_row = to_row(f_col)
            b_col = jnp.sum(jnp.where(tril, f_row, 0.0), axis=1, keepdims=True)
            b_row = jnp.sum(jnp.where(triu, f_col, 0.0), axis=0, keepdims=True)
            m_prev = m_s[bb, h][:, 0:1]
            dmat = jnp.where(tril, b_col - b_row + i_row, NEG_INF)
            g_col = b_col + m_prev
            mt = jnp.maximum(g_col, jnp.max(dmat, axis=1, keepdims=True))
            w_d = jnp.exp(dmat - mt)
            w_g = jnp.exp(g_col - mt)
            qk = lax.dot_general(q, k, (((1,), (1,)), ((), ())), preferred_element_type=F32) * w_d
            c_prev = c_s[bb, h]
            n_prev = n_s[bb, h]
            num = (w_g * jnp.dot(q, c_prev.astype(BF16), preferred_element_type=F32)
                   + jnp.dot(qk.astype(BF16), v, preferred_element_type=F32))
            den = (w_g * jnp.sum(q.astype(F32) * n_prev, axis=1, keepdims=True)
                   + jnp.sum(qk, axis=1, keepdims=True))
            hid = num / jnp.maximum(jnp.abs(den), jnp.exp(-mt))
            o_gate = _sigmoid(og_ref[bb, :, h * DH_M:(h + 1) * DH_M])
            y_ref[bb, :, h * DH_M:(h + 1) * DH_M] = (o_gate * hid).astype(y_ref.dtype)
            m_new = mt[blk - 1:blk, :]
            b_last = b_col[blk - 1:blk, :]
            w_c = jnp.exp(b_last + m_prev - m_new)
            w_s = jnp.exp(b_last - b_col + i_col - m_new)
            kw = k.astype(F32) * w_s
            c_s[bb, h] = w_c * c_prev + lax.dot_general(
                kw.astype(BF16), v, (((0,), (0,)), ((), ())), preferred_element_type=F32)
            n_s[bb, h] = w_c * n_prev + jnp.sum(kw, axis=0, keepdims=True)
            m_s[bb, h] = jnp.broadcast_to(m_new, (1, LANES))

    @pl.when(step == pl.num_programs(1) - 1)
    def _():
        cn_ref[...] = c_s[...]
        nn_ref[...] = n_s[...]
        mn_ref[...] = m_s[...]


def _mlstm(qkv, ogate, gates, c0, n0, m0):
    b, t, _ = qkv.shape
    blk = min(CHUNK, t)
    bb_n = 4
    state_spec = lambda shape: pl.BlockSpec((bb_n,) + shape, lambda i, j: (i, 0, 0, 0))
    return pl.pallas_call(
        functools.partial(_mlstm_kernel, bb_n=bb_n, blk=blk),
        grid=(b // bb_n, t // blk),
        in_specs=[pl.BlockSpec((bb_n, blk, 3 * W_M), lambda i, j: (i, j, 0)),
                  pl.BlockSpec((bb_n, blk, W_M), lambda i, j: (i, j, 0)),
                  pl.BlockSpec((bb_n, blk, GATE_COLS), lambda i, j: (i, j, 0)),
                  state_spec((H_M, DH_M, DH_M)),
                  state_spec((H_M, 1, DH_M)),
                  state_spec((H_M, 1, LANES))],
        out_specs=[pl.BlockSpec((bb_n, blk, W_M), lambda i, j: (i, j, 0)),
                   state_spec((H_M, DH_M, DH_M)),
                   state_spec((H_M, 1, DH_M)),
                   state_spec((H_M, 1, LANES))],
        out_shape=[jax.ShapeDtypeStruct((b, t, W_M), BF16),
                   jax.ShapeDtypeStruct((b, H_M, DH_M, DH_M), F32),
                   jax.ShapeDtypeStruct((b, H_M, 1, DH_M), F32),
                   jax.ShapeDtypeStruct((b, H_M, 1, LANES), F32)],
        scratch_shapes=[pltpu.VMEM((bb_n, H_M, DH_M, DH_M), F32),
                        pltpu.VMEM((bb_n, H_M, 1, DH_M), F32),
                        pltpu.VMEM((bb_n, H_M, 1, LANES), F32)],
        compiler_params=_cparams("parallel", "arbitrary"),
        name="mlstm",
    )(qkv, ogate, gates, c0, n0, m0)


def _online_softmax_step(s, pv, m_ref, l_ref, acc_ref):
    m_old = m_ref[...]
    m_new = jnp.maximum(m_old, jnp.max(s, axis=1, keepdims=True))
    alpha = jnp.exp(m_old - m_new)
    p = jnp.exp(s - m_new)
    l_ref[...] = alpha * l_ref[...] + jnp.sum(p, axis=1, keepdims=True)
    acc_ref[...] = alpha * acc_ref[...] + pv(p.astype(BF16))
    m_ref[...] = m_new


def _split3(x):
    hi = x.astype(BF16).astype(F32)
    mid = (x - hi).astype(BF16).astype(F32)
    lo = (x - hi - mid).astype(BF16).astype(F32)
    return hi, mid, lo


FOX_V_ROWS = DH_F + 16
AUG_CQ = 6


def _fox_prompt_kernel(q_ref, k_ref, v_ref, c_ref, after_ref, o_ref, kaug_s, vt_s, m0_s, m1_s, acc0_s,
                       acc1_s, *, tq, n_blocks):
    del after_ref
    hp = pl.program_id(1)
    qi = pl.program_id(2)
    lane = lax.broadcasted_iota(jnp.int32, (tq, LANES), 1)
    m_s, acc_s = (m0_s, m1_s), (acc0_s, acc1_s)

    def head_col(c_blk, x):
        return jnp.sum(jnp.where(lane == COL_FF + 2 * hp + x, c_blk, 0.0), axis=1, keepdims=True)

    def aug_tile(entries):
        tile = jnp.zeros((tq, LANES), F32)
        for l, val in entries:
            tile = jnp.where(lane == l, val, tile)
        return tile.astype(BF16)

    @pl.when(qi == 0)
    def _():
        ones_rows = jnp.where(lax.broadcasted_iota(jnp.int32, (FOX_V_ROWS - DH_F, tq), 0) == 0, 1.0, 0.0)

        def chunk(i, carry):
            rs = pl.multiple_of(i * tq, tq)
            c_blk = c_ref[pl.ds(rs, tq), :]
            pieces = _split3(head_col(c_blk, 0)) + _split3(head_col(c_blk, 1))
            entries = list(enumerate(pieces)) + [(AUG_CQ + j, 1.0) for j in range(3)]
            kaug_s[pl.ds(rs, tq), 0:LANES] = k_ref[pl.ds(rs, tq), :]
            kaug_s[pl.ds(rs, tq), LANES:2 * LANES] = aug_tile(entries)
            vt = v_ref[pl.ds(rs, tq), :].astype(F32).T
            for x in range(2):
                vt_s[x, 0:DH_F, pl.ds(rs, tq)] = vt[x * DH_F:(x + 1) * DH_F, :].astype(BF16)
                vt_s[x, DH_F:FOX_V_ROWS, pl.ds(rs, tq)] = ones_rows.astype(BF16)
            return carry

        lax.fori_loop(0, n_blocks, chunk, 0)

    qs = pl.multiple_of(qi * tq, tq)
    c_q = c_ref[pl.ds(qs, tq), :]
    q2 = q_ref[...] * (DH_F ** -0.5)
    q_aug = []
    for x in range(2):
        cq3 = _split3(head_col(c_q, x))
        entries = [(3 * x + j, -1.0) for j in range(3)] + [(AUG_CQ + j, cq3[j]) for j in range(3)]
        q_head = jnp.where((lane < DH_F) == (x == 0), q2, jnp.zeros_like(q2))
        q_aug.append(jnp.concatenate([q_head, aug_tile(entries)], axis=1))
        m_s[x][...] = jnp.full(m_s[x].shape, NEG_INF, F32)
        acc_s[x][...] = jnp.zeros(acc_s[x].shape, F32)

    def block(kb, masked):
        ks = pl.multiple_of(kb * tq, tq)
        k_blk = kaug_s[pl.ds(ks, tq), :]
        scores = [lax.dot_general(k_blk, q_aug[x], (((1,), (1,)), ((), ())), preferred_element_type=F32)
                  for x in range(2)]
        for x in range(2):
            st = scores[x]
            if masked:
                k_pos = lax.broadcasted_iota(jnp.int32, (tq, tq), 0)
                q_pos = lax.broadcasted_iota(jnp.int32, (tq, tq), 1)
                st = jnp.where(k_pos <= q_pos, st, NEG_INF)
            m_old = m_s[x][...]
            m_new = jnp.maximum(m_old, jnp.max(st, axis=0, keepdims=True))
            p = jnp.exp(st - m_new).astype(BF16)
            acc_s[x][...] = (jnp.exp(m_old - m_new) * acc_s[x][...]
                             + jnp.dot(vt_s[x, :, pl.ds(ks, tq)], p, preferred_element_type=F32))
            m_s[x][...] = m_new

    def body(kb, carry):
        block(kb, False)
        return carry

    lax.fori_loop(0, qi, body, 0)
    block(qi, True)
    out_t = jnp.concatenate([acc_s[x][0:DH_F, :] / acc_s[x][DH_F:DH_F + 1, :] for x in range(2)], axis=0)
    o_ref[...] = out_t.T.astype(o_ref.dtype)


def _fox_prompt(q, k, v, c, batch, after):
    _, t, _ = q.shape
    tq = 512
    head_pair = lambda i, h, j: (batch, 0, h)
    return pl.pallas_call(
        functools.partial(_fox_prompt_kernel, tq=tq, n_blocks=t // tq),
        grid=(1, H_F // 2, t // tq),
        in_specs=[pl.BlockSpec((None, tq, LANES), lambda i, h, j: (batch, j, h)),
                  pl.BlockSpec((None, t, LANES), head_pair),
                  pl.BlockSpec((None, t, LANES), head_pair),
                  pl.BlockSpec((None, t, LANES), lambda i, h, j: (batch, 0, 0)),
                  ORDERED],
        out_specs=pl.BlockSpec((tq, LANES), lambda i, h, j: (j, h)),
        out_shape=jax.ShapeDtypeStruct((t, W_F), BF16),
        scratch_shapes=[pltpu.VMEM((t, 2 * LANES), BF16),
                        pltpu.VMEM((2, FOX_V_ROWS, t), BF16),
                        pltpu.VMEM((1, tq), F32), pltpu.VMEM((1, tq), F32),
                        pltpu.VMEM((FOX_V_ROWS, tq), F32), pltpu.VMEM((FOX_V_ROWS, tq), F32)],
        compiler_params=_cparams("parallel", "parallel", "arbitrary"),
        name="fox_prompt",
    )(q, k, v, c, after)


def _fox_sample_kernel(q_ref, kc_ref, vc_ref, kn_ref, vn_ref, cq_ref, ctc_ref, ctn_ref, after_ref, o_ref,
                       m_s, l_s, acc_s, *, tn):
    del after_ref
    kb = pl.program_id(1)

    @pl.when(kb == 0)
    def _():
        m_s[...] = jnp.full_like(m_s, NEG_INF)
        l_s[...] = jnp.zeros_like(l_s)
        acc_s[...] = jnp.zeros_like(acc_s)

    lane = lax.broadcasted_iota(jnp.int32, (tn, LANES), 1)
    c_blk = cq_ref[...]
    nt = (((1,), (1,)), ((), ()))

    def head_query(hd):
        q = q_ref[:, hd * DH_F:(hd + 1) * DH_F] * (DH_F ** -0.5)
        cq = jnp.sum(jnp.where(lane == COL_FF + hd, c_blk, 0.0), axis=1, keepdims=True)
        return q, cq

    for hd in range(H_F):
        q, cq = head_query(hd)
        k_t = kc_ref[hd].astype(BF16)
        v_t = vc_ref[hd].astype(BF16)
        s = jnp.dot(q, k_t, preferred_element_type=F32) + cq - ctc_ref[hd // 2, hd % 2:hd % 2 + 1, :]
        _online_softmax_step(s, lambda p: lax.dot_general(p, v_t, nt, preferred_element_type=F32),
                             m_s.at[hd], l_s.at[hd], acc_s.at[hd])

    @pl.when(kb == pl.num_programs(1) - 1)
    def _():
        q_pos = lax.broadcasted_iota(jnp.int32, (tn, tn), 0)
        k_pos = lax.broadcasted_iota(jnp.int32, (tn, tn), 1)
        for hd in range(H_F):
            q, cq = head_query(hd)
            k_new = kn_ref[:, hd * DH_F:(hd + 1) * DH_F]
            v_new = vn_ref[:, hd * DH_F:(hd + 1) * DH_F]
            s = (lax.dot_general(q, k_new, nt, preferred_element_type=F32)
                 + cq - ctn_ref[hd // 2, hd % 2:hd % 2 + 1, :])
            s = jnp.where(k_pos <= q_pos, s, NEG_INF)
            _online_softmax_step(s, lambda p: jnp.dot(p, v_new, preferred_element_type=F32),
                                 m_s.at[hd], l_s.at[hd], acc_s.at[hd])
            o_ref[:, hd * DH_F:(hd + 1) * DH_F] = (acc_s[hd] / l_s[hd]).astype(o_ref.dtype)


def _fox_sample(q, k_cache_t, v_cache_t, k_new, v_new, cq, ct_cache, ct_new, after):
    b, tn, _ = q.shape
    p = k_cache_t.shape[3]
    tk = 1024
    new = pl.BlockSpec((None, tn, W_F), lambda i, j: (i, 0, 0))
    cache = pl.BlockSpec((None, H_F, DH_F, tk), lambda i, j: (i, 0, 0, j))
    return pl.pallas_call(
        functools.partial(_fox_sample_kernel, tn=tn),
        grid=(b, p // tk),
        in_specs=[new, cache, cache, new, new,
                  pl.BlockSpec((None, tn, LANES), lambda i, j: (i, 0, 0)),
                  pl.BlockSpec((None, H_F // 2, 2, tk), lambda i, j: (i, 0, 0, j)),
                  pl.BlockSpec((None, H_F // 2, 2, tn), lambda i, j: (i, 0, 0, 0)),
                  ORDERED],
        out_specs=new,
        out_shape=jax.ShapeDtypeStruct((b, tn, W_F), BF16),
        scratch_shapes=[pltpu.VMEM((H_F, tn, 1), F32),
                        pltpu.VMEM((H_F, tn, 1), F32),
                        pltpu.VMEM((H_F, tn, DH_F), F32)],
        compiler_params=_cparams("parallel", "arbitrary"),
        name="fox_sample",
    )(q, k_cache_t, v_cache_t, k_new, v_new, cq, ct_cache, ct_new, after)


def _merge_kernel(h_ref, ym_ref, yf_ref, gm_ref, gf_ref, wm_ref, wf_ref, wo_ref, after_ref, o_ref):
    del after_ref
    a = jnp.dot(ym_ref[...], wm_ref[...], preferred_element_type=F32)
    b = jnp.dot(yf_ref[...], wf_ref[...], preferred_element_type=F32)
    merge = _sigmoid(gm_ref[...]) * a + _sigmoid(gf_ref[...]) * b
    o_ref[...] = h_ref[...] + jnp.dot(merge.astype(BF16), wo_ref[...], preferred_element_type=F32)


def _merge(h, ym, yf, gates, wm, wf, wo, first_token, after):
    n, d = yf.shape
    tm = min(512, n)
    first = first_token // tm
    tok = lambda c: pl.BlockSpec((tm, d), lambda i: (first + i, c))
    local = pl.BlockSpec((tm, d), lambda i: (i, 0))
    wspec = pl.BlockSpec((d, d), lambda i: (0, 0))
    return pl.pallas_call(
        _merge_kernel,
        grid=(n // tm,),
        in_specs=[tok(0), tok(0), local, tok(1), tok(2), wspec, wspec, wspec, ORDERED],
        out_specs=local,
        out_shape=jax.ShapeDtypeStruct((n, d), F32),
        compiler_params=_cparams("parallel"),
        name="merge",
    )(h, ym, yf, gates, gates, wm, wf, wo, after)


def _top_rows(s, count, ids=None, payload=None):
    if ids is None:
        ids = lax.broadcasted_iota(jnp.int32, s.shape, 0)
    big = jnp.int32(2 ** 30)
    vals, sel, pay = [], [], []
    for _ in range(count):
        m = jnp.max(s, axis=0, keepdims=True)
        am = jnp.min(jnp.where(s == m, ids, big), axis=0, keepdims=True)
        hit = ids == am
        vals.append(m)
        sel.append(am)
        if payload is not None:
            pay.append(jnp.max(jnp.where(hit, payload, -1), axis=0, keepdims=True))
        s = jnp.where(hit, NEG_INF, s)
    cat = lambda xs: jnp.concatenate(xs, axis=0)
    return cat(vals), cat(sel), (cat(pay) if payload is not None else None)


def _pair_candidates(v1, i1, v2, i2):
    t = v1.shape[1]
    half = PEER_TOPK // 2
    r16 = lax.broadcasted_iota(jnp.int32, (PEER_TOPK, t), 0)
    r8 = lax.broadcasted_iota(jnp.int32, (half, t), 0)
    sums = [v1[0:1, :] + v2]
    flat = [r16]
    expert = [i1[0:1, :] * PEER_KEYS + i2]
    for a in range(1, half):
        sums.append(v1[a:a + 1, :] + v2[0:half, :])
        flat.append(r8 + a * PEER_TOPK)
        expert.append(i1[a:a + 1, :] * PEER_KEYS + i2[0:half, :])
    sums.append(v1[half:, :] + v2[0:1, :])
    flat.append((r8 + half) * PEER_TOPK)
    expert.append(i1[half:, :] * PEER_KEYS + i2[0:1, :])
    cat = lambda xs: jnp.concatenate(xs, axis=0)
    return cat(sums), cat(flat), cat(expert)


def _peer_score_kernel(h_ref, g_ref, wqt_ref, k1_ref, k2_ref, xn_ref, idx_ref, gate_ref):
    xn = _rms(h_ref[...], g_ref[...])
    xn_ref[...] = xn.astype(xn_ref.dtype)
    qt = lax.dot_general(wqt_ref[...], xn, (((1,), (1,)), ((), ())),
                         precision=HIGHEST, preferred_element_type=F32)
    half = PEER_DQ // 2
    idx_rows, gate_rows = [], []
    for hd in range(PEER_HEADS):
        q1 = qt[hd * PEER_DQ:hd * PEER_DQ + half, :]
        q2 = qt[hd * PEER_DQ + half:(hd + 1) * PEER_DQ, :]
        s1 = jnp.dot(k1_ref[hd], q1, precision=HIGHEST, preferred_element_type=F32)
        s2 = jnp.dot(k2_ref[hd], q2, precision=HIGHEST, preferred_element_type=F32)
        v1, i1, _ = _top_rows(s1, PEER_TOPK)
        v2, i2, _ = _top_rows(s2, PEER_TOPK)
        cand, flat, expert = _pair_candidates(v1, i1, v2, i2)
        sc, _, ex = _top_rows(cand, PEER_TOPK, ids=flat, payload=expert)
        e = jnp.exp(sc - sc[0:1, :])
        gate_rows.append(e / jnp.sum(e, axis=0, keepdims=True))
        idx_rows.append(ex)
    gate_ref[...] = jnp.concatenate(gate_rows, axis=0).T
    idx_ref[...] = jnp.concatenate(idx_rows, axis=0).T


def _peer_score(h, g, wqt, k1, k2):
    n, d = h.shape
    tt = min(256, n)
    return pl.pallas_call(
        _peer_score_kernel,
        grid=(n // tt,),
        in_specs=[pl.BlockSpec((tt, d), lambda i: (i, 0)),
                  pl.BlockSpec((1, d), lambda i: (0, 0)),
                  pl.BlockSpec(wqt.shape, lambda i: (0, 0)),
                  pl.BlockSpec(k1.shape, lambda i: (0, 0, 0)),
                  pl.BlockSpec(k2.shape, lambda i: (0, 0, 0))],
        out_specs=[pl.BlockSpec((tt, d), lambda i: (i, 0)),
                   pl.BlockSpec((tt, PEER_SEL), lambda i: (i, 0)),
                   pl.BlockSpec((tt, PEER_SEL), lambda i: (i, 0))],
        out_shape=[jax.ShapeDtypeStruct((n, d), F32),
                   jax.ShapeDtypeStruct((n, PEER_SEL), jnp.int32),
                   jax.ShapeDtypeStruct((n, PEER_SEL), F32)],
        compiler_params=_cparams("parallel"),
        name="peer_score",
    )(h, g, wqt, k1, k2)


SC_CORES = 2
SC_SUBCORES = 16
SC_WORKERS = SC_CORES * SC_SUBCORES
SC_ROWS = 64


def _pack_table(tab):
    half = tab.shape[1] // 2
    bits = lax.bitcast_convert_type(tab.astype(BF16), jnp.uint16).astype(jnp.uint32)
    word = bits[:, :half] | (bits[:, half:] << 16)
    return lax.bitcast_convert_type(word, jnp.int32)


SC_TOK = 8
SC_LANES = 16
STEPS_PER_TOKEN = PEER_SEL // SC_ROWS
SC_PARAMS = pltpu.CompilerParams(needs_layout_passes=False)


def _row_source(table_hbm, idx_v, local_step, global_step):
    del global_step
    return table_hbm.at[idx_v.at[local_step]]


def _sc_unpack(wd):
    lo = lax.bitcast_convert_type(wd << 16, F32)
    hi = lax.bitcast_convert_type(wd & jnp.int32(-65536), F32)
    return lo, hi


def _sc_token_blocks(n, body_block):
    per_worker = n // SC_WORKERS
    assert per_worker * SC_WORKERS == n and per_worker % SC_TOK == 0
    wid = lax.axis_index("s") * SC_CORES + lax.axis_index("c")

    @pl.loop(0, per_worker // SC_TOK)
    def _(blk):
        body_block(wid * per_worker + blk * SC_TOK)


def _sc_pipelined_steps(table_hbm, idx_v, rows_v, sems, first_step, consume):
    n_steps = SC_TOK * STEPS_PER_TOKEN

    def row_gather(j, slot):
        return pltpu.make_async_copy(_row_source(table_hbm, idx_v, j, first_step + j),
                                     rows_v.at[slot], sems.at[slot])

    row_gather(0, 0).start()

    @pl.loop(0, n_steps, step=2)
    def _(i):
        for slot in range(2):
            j = i + slot

            @pl.when(j + 1 < n_steps)
            def _():
                row_gather(j + 1, 1 - slot).start()

            row_gather(j, slot).wait()
            consume(slot, i // 2, slot)


def _sc_expert_dot(table, idx_steps, x):
    n, d = x.shape
    w = d // 2
    assert STEPS_PER_TOKEN == 2 and table.shape[1] == w
    mesh = plsc.VectorSubcoreMesh(core_axis_name="c", subcore_axis_name="s")
    n_steps = SC_TOK * STEPS_PER_TOKEN
    group = 4

    @functools.partial(
        pl.kernel, mesh=mesh,
        out_type=jax.ShapeDtypeStruct((n, PEER_SEL), F32),
        scratch_types=[pltpu.VMEM((n_steps, SC_ROWS), jnp.int32),
                       pltpu.VMEM((SC_TOK, d), F32),
                       pltpu.VMEM((2, SC_ROWS, w), jnp.int32),
                       pltpu.VMEM((SC_TOK, PEER_SEL), F32),
                       pltpu.SemaphoreType.DMA((2,))],
        compiler_params=SC_PARAMS,
    )
    def expert_dot(table_hbm, idx_hbm, x_hbm, act_hbm, idx_v, x_v, rows_v, act_v, sems):
        lanes = lax.iota(jnp.int32, SC_LANES)
        zero = jnp.zeros((SC_LANES,), F32)

        def consume(slot, tl, half):
            @pl.loop(0, SC_ROWS // SC_LANES)
            def _(g):
                act_vec = zero
                for q in range(SC_LANES // group):
                    r0 = g * SC_LANES + q * group

                    def chunk(j, accs):
                        c = pl.multiple_of(j * SC_LANES, SC_LANES)
                        x_lo = x_v[tl, pl.ds(c, SC_LANES)]
                        x_hi = x_v[tl, pl.ds(w + c, SC_LANES)]
                        out = []
                        for rr in range(group):
                            lo, hi = _sc_unpack(rows_v[slot, r0 + rr, pl.ds(c, SC_LANES)])
                            out.append(accs[rr] + lo * x_lo + hi * x_hi)
                        return tuple(out)

                    accs = lax.fori_loop(0, w // SC_LANES, chunk, (zero,) * group, unroll=2)
                    for rr in range(group):
                        act_vec = jnp.where(lanes == q * group + rr, jnp.sum(accs[rr]), act_vec)
                act_v[tl, pl.ds(half * SC_ROWS + g * SC_LANES, SC_LANES)] = act_vec

        def block(t0):
            s0 = t0 * STEPS_PER_TOKEN
            pltpu.sync_copy(idx_hbm.at[pl.ds(s0, n_steps)], idx_v)
            pltpu.sync_copy(x_hbm.at[pl.ds(t0, SC_TOK)], x_v)
            _sc_pipelined_steps(table_hbm, idx_v, rows_v, sems, s0, consume)
            pltpu.sync_copy(act_v, act_hbm.at[pl.ds(t0, SC_TOK)])

        _sc_token_blocks(n, block)

    return expert_dot(table, idx_steps, x)


def _sc_expert_sum(table, idx_steps, wgt):
    n = wgt.shape[0]
    w = table.shape[1]
    d = 2 * w
    assert STEPS_PER_TOKEN == 2
    mesh = plsc.VectorSubcoreMesh(core_axis_name="c", subcore_axis_name="s")
    n_steps = SC_TOK * STEPS_PER_TOKEN
    cols = 8

    @functools.partial(
        pl.kernel, mesh=mesh,
        out_type=jax.ShapeDtypeStruct((n, d), F32),
        scratch_types=[pltpu.VMEM((n_steps, SC_ROWS), jnp.int32),
                       pltpu.VMEM((SC_TOK, PEER_SEL), F32),
                       pltpu.VMEM((2, SC_ROWS, w), jnp.int32),
                       pltpu.VMEM((SC_TOK, d), F32),
                       pltpu.VMEM((SC_ROWS, SC_LANES), F32),
                       pltpu.SemaphoreType.DMA((2,))],
        compiler_params=SC_PARAMS,
    )
    def expert_sum(table_hbm, idx_hbm, wgt_hbm, out_hbm, idx_v, wgt_v, rows_v, out_v, splat_v, sems):
        zero = jnp.zeros((SC_LANES,), F32)
        lanes = lax.iota(jnp.int32, SC_LANES)

        def consume(slot, tl, half):
            @pl.loop(0, SC_ROWS // SC_LANES)
            def _(g):
                w16 = wgt_v[tl, pl.ds(half * SC_ROWS + g * SC_LANES, SC_LANES)]
                for rr in range(SC_LANES):
                    one = jnp.sum(jnp.where(lanes == rr, w16, 0.0))
                    splat_v[g * SC_LANES + rr, :] = jnp.full((SC_LANES,), one, F32)

            for cb in range(w // (cols * SC_LANES)):
                base = cb * cols * SC_LANES

                def row(r, accs):
                    wv = splat_v[r, :]
                    out = []
                    for jj in range(cols):
                        lo, hi = _sc_unpack(rows_v[slot, r, pl.ds(base + jj * SC_LANES, SC_LANES)])
                        out.append(accs[2 * jj] + wv * lo)
                        out.append(accs[2 * jj + 1] + wv * hi)
                    return tuple(out)

                accs = lax.fori_loop(0, SC_ROWS, row, (zero,) * (2 * cols), unroll=2)
                for jj in range(cols):
                    c = base + jj * SC_LANES
                    if half == 0:
                        out_v[tl, pl.ds(c, SC_LANES)] = accs[2 * jj]
                        out_v[tl, pl.ds(w + c, SC_LANES)] = accs[2 * jj + 1]
                    else:
                        out_v[tl, pl.ds(c, SC_LANES)] = out_v[tl, pl.ds(c, SC_LANES)] + accs[2 * jj]
                        out_v[tl, pl.ds(w + c, SC_LANES)] = out_v[tl, pl.ds(w + c, SC_LANES)] + accs[2 * jj + 1]

        def block(t0):
            s0 = t0 * STEPS_PER_TOKEN
            pltpu.sync_copy(idx_hbm.at[pl.ds(s0, n_steps)], idx_v)
            pltpu.sync_copy(wgt_hbm.at[pl.ds(t0, SC_TOK)], wgt_v)
            _sc_pipelined_steps(table_hbm, idx_v, rows_v, sems, s0, consume)
            pltpu.sync_copy(out_v, out_hbm.at[pl.ds(t0, SC_TOK)])

        _sc_token_blocks(n, block)

    return expert_sum(table, idx_steps, wgt)


def _gelu_exact(x):
    return 0.5 * x * (1.0 + lax.erf(x * (2.0 ** -0.5)))


def _peer_weight_kernel(act_ref, gate_ref, after_ref, o_ref):
    del after_ref
    o_ref[...] = gate_ref[...] * _gelu_exact(act_ref[...])


def _peer_weight(act, gate, after):
    n = act.shape[0]
    tt = min(2048, n)
    spec = pl.BlockSpec((tt, PEER_SEL), lambda i: (i, 0))
    return pl.pallas_call(
        _peer_weight_kernel,
        grid=(n // tt,),
        in_specs=[spec, spec, ORDERED],
        out_specs=spec,
        out_shape=jax.ShapeDtypeStruct((n, PEER_SEL), F32),
        compiler_params=_cparams("parallel"),
        name="peer_weight",
    )(act, gate, after)


def _ple_kernel(h_ref, peer_ref, p_ref, gp_ref, gfin_ref, wg_ref, wp_ref, o_ref):
    h = h_ref[...] + peer_ref[...]
    e = _rms(h, gp_ref[...]).astype(BF16)
    gate = _sigmoid(jnp.dot(e, wg_ref[...], preferred_element_type=F32))
    proj = jnp.dot(p_ref[...].astype(BF16), wp_ref[...], preferred_element_type=F32)
    o_ref[...] = _rms(h + gate * proj, gfin_ref[...])


def _ple_final(h, peer, p, g_ple, g_final, wg, wp, first_token):
    n, d = h.shape
    tm = min(512, n)
    first = first_token // tm
    return pl.pallas_call(
        _ple_kernel,
        grid=(n // tm,),
        in_specs=[pl.BlockSpec((tm, d), lambda i: (i, 0)),
                  pl.BlockSpec((tm, d), lambda i: (i, 0)),
                  pl.BlockSpec((tm, D_PLE), lambda i: (first + i, 0)),
                  pl.BlockSpec((1, d), lambda i: (0, 0)),
                  pl.BlockSpec((1, d), lambda i: (0, 0)),
                  pl.BlockSpec((d, d), lambda i: (0, 0)),
                  pl.BlockSpec((D_PLE, d), lambda i: (0, 0))],
        out_specs=pl.BlockSpec((tm, d), lambda i: (i, 0)),
        out_shape=jax.ShapeDtypeStruct((n, d), F32),
        compiler_params=_cparams("parallel"),
        name="ple_final",
    )(h, peer, p, g_ple, g_final, wg, wp)


def _prep_weights(norm_mix, w_in, mlstm_b_i, mlstm_b_f, fox_b_f, w_br_m, w_br_f, w_out, norm_ffn,
                  peer_w_q, peer_keys1, peer_keys2, peer_u, peer_v, norm_ple, w_ple_gate, w_ple_proj,
                  norm_final):
    o = [0]
    for s in (W_M, W_M, W_M, W_M, H_M, H_M, W_F, W_F, W_F, H_F, D_MODEL, D_MODEL):
        o.append(o[-1] + s)
    seg = lambda a, b: w_in[:, o[a]:o[b]]
    w_gate = jnp.concatenate([seg(4, 6), seg(9, 10)], axis=1)
    w_gate = jnp.pad(w_gate, ((0, 0), (0, GATE_COLS - w_gate.shape[1])))
    b_gate = jnp.concatenate([mlstm_b_i, mlstm_b_f, fox_b_f]).astype(F32)
    b_gate = jnp.pad(b_gate, (0, GATE_COLS - b_gate.shape[0]))[None, :]
    row = lambda v: v.astype(F32)[None, :]
    return dict(
        norm_mix=row(norm_mix),
        w_mqkv=seg(0, 3).astype(BF16),
        w_og=jnp.concatenate([seg(3, 4), seg(10, 12)], axis=1).astype(BF16),
        w_fq=seg(6, 7).astype(BF16), w_fk=seg(7, 8).astype(BF16), w_fv=seg(8, 9).astype(BF16),
        w_gate=w_gate.astype(F32), b_gate=b_gate,
        w_br_m=w_br_m.astype(BF16), w_br_f=w_br_f.astype(BF16), w_out=w_out.astype(BF16),
        norm_ffn=row(norm_ffn), wqt=peer_w_q.T.astype(F32),
        k1=peer_keys1.astype(F32), k2=peer_keys2.astype(F32),
        u_pk=_pack_table(peer_u), v_pk=_pack_table(peer_v),
        norm_ple=row(norm_ple), w_ple_gate=w_ple_gate.astype(BF16), w_ple_proj=w_ple_proj.astype(BF16),
        norm_final=row(norm_final),
    )


def _ct_rows(c):
    b, t, _ = c.shape
    return jnp.swapaxes(c[:, :, COL_FF:COL_FF + H_F], 1, 2).reshape(b, H_F // 2, 2, t)


def _peer_first_pass(h, w):
    n = h.shape[0]
    xn, idx, gate = _peer_score(h, w["norm_ffn"], w["wqt"], w["k1"], w["k2"])
    steps = idx.reshape(n * STEPS_PER_TOKEN, SC_ROWS)
    return steps, gate, _sc_expert_dot(w["u_pk"], steps, xn)


def _peer_second_pass(first, w, after):
    steps, gate, act = first
    wgt = _peer_weight(act, gate, after)
    return _sc_expert_sum(w["v_pk"], steps, wgt), wgt


def _layer(x, p, w, mstate, fox_cache, after):
    b, t, d = x.shape
    n = b * t
    h = x.reshape(n, d)
    g = w["norm_mix"]
    qkv = _norm_matmul(h, g, w["w_mqkv"], BF16, "proj_mlstm_qkv")
    og = _norm_matmul(h, g, w["w_og"], F32, "proj_gates")
    fq = _norm_matmul(h, g, w["w_fq"], BF16, "proj_fox_q")
    if fox_cache is None:
        fk, fk_t = _norm_matmul_tokens_minor(h, g, w["w_fk"], t, "proj_fox_k")
        fv, fv_t = _norm_matmul_tokens_minor(h, g, w["w_fv"], t, "proj_fox_v")
        fk_state, fv_state = (jnp.transpose(a, (0, 3, 1, 2))[None] for a in (fk_t, fv_t))
    else:
        fk, fk_heads = _norm_matmul_heads(h, g, w["w_fk"], "proj_fox_k")
        fv, fv_heads = _norm_matmul_heads(h, g, w["w_fv"], "proj_fox_v")
        fk_state, fv_state = (a.reshape(1, b, t, H_F, DH_F) for a in (fk_heads, fv_heads))
    gates = _gates(h, g, w["w_gate"], w["b_gate"])

    c0, n0, m0 = mstate
    ym, c_new, n_new, m_new = _mlstm(
        qkv.reshape(b, t, 3 * W_M), og.reshape(b, t, 3 * D_MODEL), gates.reshape(b, t, GATE_COLS),
        c0.astype(F32), n0.astype(F32)[:, :, None, :],
        jnp.broadcast_to(m0.astype(F32)[:, :, None, None], (b, H_M, 1, LANES)))

    def merged(yf, first_token, after):
        return _merge(h, ym.reshape(n, W_M), yf, og, w["w_br_m"], w["w_br_f"], w["w_out"], first_token, after)

    def output(hm, peer, first_token):
        return _ple_final(hm, peer, p.reshape(n, D_PLE), w["norm_ple"], w["norm_final"],
                          w["w_ple_gate"], w["w_ple_proj"], first_token)

    neutral = w["norm_ffn"]
    gates3 = gates.reshape(b, t, GATE_COLS)
    if fox_cache is None:
        c = _cumsum_tokens(gates3)
        seqs = (fq.reshape(b, t, W_F), fk.reshape(b, t, W_F), fv.reshape(b, t, W_F))
        hms, firsts, peers, wgts = [], [], [], []
        for bi in range(b):
            yf = _fox_prompt(*seqs, c, bi, peers[bi - 2] if bi >= 2 else neutral)
            if bi >= 1:
                peer, wgt = _peer_second_pass(firsts[bi - 1], w, yf)
                peers.append(peer)
                wgts.append(wgt)
            hms.append(merged(yf, bi * t, wgts[bi - 1] if bi >= 1 else neutral))
            firsts.append(_peer_first_pass(hms[bi], w))
        peer, wgt = _peer_second_pass(firsts[b - 1], w, neutral)
        peers.append(peer)
        wgts.append(wgt)
        y = jnp.concatenate([output(hms[bi], peers[bi], bi * t) for bi in range(b)], axis=0)
        tail = (peers[-2], firsts[-1][1], wgts[-1], peers[-1])
    else:
        ck, cv, clf = fox_cache
        past = ck.shape[1]
        lf = jnp.pad(clf.astype(F32), ((0, 0), (0, 0), (COL_FF, GATE_COLS - COL_FF - H_F)))
        lf = jnp.concatenate([lf, gates3], axis=1)
        pad_t = (-lf.shape[1]) % 256
        c = _cumsum_tokens(jnp.pad(lf, ((0, 0), (0, pad_t), (0, 0))))
        ct = _ct_rows(c)
        keys_minor = lambda cache: jnp.transpose(cache.astype(F32), (0, 2, 3, 1))
        yf = _fox_sample(fq.reshape(b, t, W_F), keys_minor(ck), keys_minor(cv),
                         fk.reshape(b, t, W_F), fv.reshape(b, t, W_F),
                         c[:, past:past + t, :], ct[..., :past], ct[..., past:past + t], after[0])
        hm = merged(yf.reshape(n, W_F), 0, after[1])
        peer, wgt = _peer_second_pass(_peer_first_pass(hm, w), w, neutral)
        tail = (peer, wgt, wgt, peer)
        y = output(hm, peer, 0)

    state = (fk_state, fv_state,
             gates3[None, :, :, COL_FF:COL_FF + H_F],
             c_new[None], n_new[None, :, :, 0, :], m_new[None, :, :, 0, 0])
    return y.reshape(b, t, d), state, tail


def kernel(x_prompt, x_sample, p_prompt, p_sample, cache_fox_k, cache_fox_v, cache_fox_logf, state_mlstm_C, state_mlstm_n, state_mlstm_m, norm_mix, w_in, mlstm_b_i, mlstm_b_f, fox_b_f, w_br_m, w_br_f, w_out, norm_ffn, peer_w_q, peer_keys1, peer_keys2, peer_u, peer_v, norm_ple, w_ple_gate, w_ple_proj, norm_final):
    assert w_in.shape[0] == 1, "single-layer trunk"
    w = _prep_weights(norm_mix[0], w_in[0], mlstm_b_i[0], mlstm_b_f[0], fox_b_f[0], w_br_m[0], w_br_f[0],
                      w_out[0], norm_ffn[0], peer_w_q[0], peer_keys1[0], peer_keys2[0], peer_u[0], peer_v[0],
                      norm_ple[0], w_ple_gate[0], w_ple_proj[0], norm_final)
    bp = x_prompt.shape[0]
    zeros = (jnp.zeros((bp, H_M, DH_M, DH_M), F32), jnp.zeros((bp, H_M, DH_M), F32), jnp.zeros((bp, H_M), F32))
    y_p, sp, tail = _layer(x_prompt, p_prompt[0], w, zeros, None, None)
    bs, ts, d = x_sample.shape
    x_s = _ordered(x_sample.reshape(bs * ts, d), tail[0], tail[1]).reshape(bs, ts, d)
    y_s, ss, _ = _layer(x_s, p_sample[0], w,
                        (state_mlstm_C[0], state_mlstm_n[0], state_mlstm_m[0]),
                        (cache_fox_k[0], cache_fox_v[0], cache_fox_logf[0]), tail[2:])
    return (y_p, y_s) + sp + ss
```

```python
import functools

import jax
import jax.numpy as jnp
from jax import lax
from jax.experimental import pallas as pl
from jax.experimental.pallas import tpu as pltpu
from jax.experimental.pallas import tpu_sc as plsc

D_MODEL = 1024
CHUNK = 64
H_M = 4
DH_M = 256
W_M = H_M * DH_M
H_F = 16
DH_F = 64
W_F = H_F * DH_F
D_PLE = 256
PEER_HEADS = 8
PEER_KEYS = 128
PEER_DQ = 256
PEER_TOPK = 16
PEER_SEL = PEER_HEADS * PEER_TOPK
EPS = 1e-6

LANES = 128
GATE_COLS = LANES
COL_I, COL_F, COL_FF = 0, H_M, 2 * H_M
VMEM_LIMIT = 56 * 1024 * 1024
HIGHEST = lax.Precision.HIGHEST
F32 = jnp.float32
BF16 = jnp.bfloat16
NEG_INF = float("-inf")


def _cparams(*sem):
    return pltpu.CompilerParams(dimension_semantics=sem, vmem_limit_bytes=VMEM_LIMIT)


ORDERED = pl.BlockSpec(memory_space=pl.ANY)


def _ordered_kernel(x_ref, after_a, after_b, o_ref):
    del after_a, after_b
    o_ref[...] = x_ref[...]


def _ordered(x, after_a, after_b):
    n, d = x.shape
    spec = pl.BlockSpec((n, d), lambda i: (0, 0))
    return pl.pallas_call(
        _ordered_kernel, grid=(1,), in_specs=[spec, ORDERED, ORDERED], out_specs=spec,
        out_shape=jax.ShapeDtypeStruct((n, d), x.dtype),
        compiler_params=_cparams("arbitrary"), name="ordered_copy",
    )(x, after_a, after_b)


def _rms(x, g):
    return x * lax.rsqrt(jnp.mean(x * x, axis=-1, keepdims=True) + EPS) * g


def _sigmoid(x):
    return 1.0 / (1.0 + jnp.exp(-x))


def _norm_matmul_kernel(x_ref, g_ref, w_ref, o_ref, a_ref):
    @pl.when(pl.program_id(1) == 0)
    def _():
        a_ref[...] = _rms(x_ref[...], g_ref[...]).astype(a_ref.dtype)

    o_ref[...] = jnp.dot(a_ref[...], w_ref[...], preferred_element_type=F32).astype(o_ref.dtype)


def _norm_matmul(x, g, w, out_dtype, name):
    n, d = x.shape
    cols = w.shape[1]
    tm = min(1024, n)
    tn = 1024
    return pl.pallas_call(
        _norm_matmul_kernel,
        grid=(n // tm, cols // tn),
        in_specs=[pl.BlockSpec((tm, d), lambda i, j: (i, 0)),
                  pl.BlockSpec((1, d), lambda i, j: (0, 0)),
                  pl.BlockSpec((d, tn), lambda i, j: (0, j))],
        out_specs=pl.BlockSpec((tm, tn), lambda i, j: (i, j)),
        out_shape=jax.ShapeDtypeStruct((n, cols), out_dtype),
        scratch_shapes=[pltpu.VMEM((tm, d), BF16)],
        compiler_params=_cparams("parallel", "arbitrary"),
        name=name,
    )(x, g, w)


def _norm_matmul_heads_kernel(x_ref, g_ref, w_ref, o_ref, oh_ref):
    a = _rms(x_ref[...], g_ref[...]).astype(BF16)
    z = jnp.dot(a, w_ref[...], preferred_element_type=F32)
    o_ref[...] = z.astype(o_ref.dtype)
    for h in range(H_F):
        oh_ref[:, h, :] = z[:, h * DH_F:(h + 1) * DH_F]


def _norm_matmul_tokens_minor_kernel(x_ref, g_ref, w_ref, o_ref, ot_ref):
    a = _rms(x_ref[...], g_ref[...]).astype(BF16)
    z = jnp.dot(a, w_ref[...], preferred_element_type=F32)
    o_ref[...] = z.astype(o_ref.dtype)
    ot_ref[...] = z.T.reshape(ot_ref.shape)


def _norm_matmul_tokens_minor(x, g, w, seq_len, name):
    n, d = x.shape
    tm = 512
    per_seq = seq_len // tm
    assert per_seq * tm == seq_len and n % seq_len == 0
    return pl.pallas_call(
        _norm_matmul_tokens_minor_kernel,
        grid=(n // tm,),
        in_specs=[pl.BlockSpec((tm, d), lambda i: (i, 0)),
                  pl.BlockSpec((1, d), lambda i: (0, 0)),
                  pl.BlockSpec((d, W_F), lambda i: (0, 0))],
        out_specs=[pl.BlockSpec((tm, W_F), lambda i: (i, 0)),
                   pl.BlockSpec((None, H_F, DH_F, tm), lambda i: (i // per_seq, 0, 0, i % per_seq))],
        out_shape=[jax.ShapeDtypeStruct((n, W_F), BF16),
                   jax.ShapeDtypeStruct((n // seq_len, H_F, DH_F, seq_len), F32)],
        compiler_params=_cparams("parallel"),
        name=name,
    )(x, g, w)


def _norm_matmul_heads(x, g, w, name):
    n, d = x.shape
    tm = min(512, n)
    return pl.pallas_call(
        _norm_matmul_heads_kernel,
        grid=(n // tm,),
        in_specs=[pl.BlockSpec((tm, d), lambda i: (i, 0)),
                  pl.BlockSpec((1, d), lambda i: (0, 0)),
                  pl.BlockSpec((d, W_F), lambda i: (0, 0))],
        out_specs=[pl.BlockSpec((tm, W_F), lambda i: (i, 0)),
                   pl.BlockSpec((tm, H_F, DH_F), lambda i: (i, 0, 0))],
        out_shape=[jax.ShapeDtypeStruct((n, W_F), BF16),
                   jax.ShapeDtypeStruct((n, H_F, DH_F), F32)],
        compiler_params=_cparams("parallel"),
        name=name,
    )(x, g, w)


def _gate_kernel(x_ref, g_ref, w_ref, b_ref, o_ref):
    a = _rms(x_ref[...], g_ref[...])
    z = jnp.dot(a, w_ref[...], precision=HIGHEST, preferred_element_type=F32) + b_ref[...]
    col = lax.broadcasted_iota(jnp.int32, z.shape, 1)
    log_sig = jnp.minimum(z, 0.0) - jnp.log1p(jnp.exp(-jnp.abs(z)))
    o_ref[...] = jnp.where(col < COL_F, z, log_sig)


def _gates(x, g, w, b):
    n, d = x.shape
    tm = min(512, n)
    return pl.pallas_call(
        _gate_kernel,
        grid=(n // tm,),
        in_specs=[pl.BlockSpec((tm, d), lambda i: (i, 0)),
                  pl.BlockSpec((1, d), lambda i: (0, 0)),
                  pl.BlockSpec((d, GATE_COLS), lambda i: (0, 0)),
                  pl.BlockSpec((1, GATE_COLS), lambda i: (0, 0))],
        out_specs=pl.BlockSpec((tm, GATE_COLS), lambda i: (i, 0)),
        out_shape=jax.ShapeDtypeStruct((n, GATE_COLS), F32),
        compiler_params=_cparams("parallel"),
        name="gates",
    )(x, g, w, b)


def _cumsum_kernel(x_ref, o_ref, carry_ref):
    @pl.when(pl.program_id(1) == 0)
    def _():
        carry_ref[...] = jnp.zeros_like(carry_ref)

    x = x_ref[...]
    tb = x.shape[0]
    row = lax.broadcasted_iota(jnp.int32, (tb, tb), 0)
    col = lax.broadcasted_iota(jnp.int32, (tb, tb), 1)
    tril = jnp.where(col <= row, 1.0, 0.0).astype(F32)
    c = jnp.dot(tril, x, precision=HIGHEST, preferred_element_type=F32) + carry_ref[...]
    o_ref[...] = c
    carry_ref[...] = c[tb - 1:tb, :]


def _cumsum_tokens(x):
    b, t, w = x.shape
    tb = 256
    return pl.pallas_call(
        _cumsum_kernel,
        grid=(b, t // tb),
        in_specs=[pl.BlockSpec((None, tb, w), lambda i, j: (i, j, 0))],
        out_specs=pl.BlockSpec((None, tb, w), lambda i, j: (i, j, 0)),
        out_shape=jax.ShapeDtypeStruct((b, t, w), F32),
        scratch_shapes=[pltpu.VMEM((1, w), F32)],
        compiler_params=_cparams("parallel", "arbitrary"),
        name="cumsum",
    )(x)


def _mlstm_kernel(qkv_ref, og_ref, g_ref, c0_ref, n0_ref, m0_ref,
                  y_ref, cn_ref, nn_ref, mn_ref, c_s, n_s, m_s, *, bb_n, blk):
    step = pl.program_id(1)

    @pl.when(step == 0)
    def _():
        c_s[...] = c0_ref[...]
        n_s[...] = n0_ref[...]
        m_s[...] = m0_ref[...]

    row = lax.broadcasted_iota(jnp.int32, (blk, blk), 0)
    col = lax.broadcasted_iota(jnp.int32, (blk, blk), 1)
    tril = col <= row
    triu = row <= col
    eye = col == row

    def to_row(x_col):
        return jnp.sum(jnp.where(eye, x_col, 0.0), axis=0, keepdims=True)

    for bb in range(bb_n):
        g = g_ref[bb]
        for h in range(H_M):
            q = qkv_ref[bb, :, h * DH_M:(h + 1) * DH_M]
            k = qkv_ref[bb, :, (H_M + h) * DH_M:(H_M + h + 1) * DH_M] * (DH_M ** -0.5)
            v = qkv_ref[bb, :, (2 * H_M + h) * DH_M:(2 * H_M + h + 1) * DH_M]
            i_col = g[:, COL_I + h:COL_I + h + 1]
            f_col = g[:, COL_F + h:COL_F + h + 1]
            i_row = to_row(i_col)
            f_row = to_row(f_col)
            b_col = jnp.sum(jnp.where(tril, f_row, 0.0), axis=1, keepdims=True)
            b_row = jnp.sum(jnp.where(triu, f_col, 0.0), axis=0, keepdims=True)
            m_prev = m_s[bb, h][:, 0:1]
            dmat = jnp.where(tril, b_col - b_row + i_row, NEG_INF)
            g_col = b_col + m_prev
            mt = jnp.maximum(g_col, jnp.max(dmat, axis=1, keepdims=True))
            w_d = jnp.exp(dmat - mt)
            w_g = jnp.exp(g_col - mt)
            qk = lax.dot_general(q, k, (((1,), (1,)), ((), ())), preferred_element_type=F32) * w_d
            c_prev = c_s[bb, h]
            n_prev = n_s[bb, h]
            num = (w_g * jnp.dot(q, c_prev.astype(BF16), preferred_element_type=F32)
                   + jnp.dot(qk.astype(BF16), v, preferred_element_type=F32))
            den = (w_g * jnp.sum(q.astype(F32) * n_prev, axis=1, keepdims=True)
                   + jnp.sum(qk, axis=1, keepdims=True))
            hid = num / jnp.maximum(jnp.abs(den), jnp.exp(-mt))
            o_gate = _sigmoid(og_ref[bb, :, h * DH_M:(h + 1) * DH_M])
            y_ref[bb, :, h * DH_M:(h + 1) * DH_M] = (o_gate * hid).astype(y_ref.dtype)
            m_new = mt[blk - 1:blk, :]
            b_last = b_col[blk - 1:blk, :]
            w_c = jnp.exp(b_last + m_prev - m_new)
            w_s = jnp.exp(b_last - b_col + i_col - m_new)
            kw = k.astype(F32) * w_s
            c_s[bb, h] = w_c * c_prev + lax.dot_general(
                kw.astype(BF16), v, (((0,), (0,)), ((), ())), preferred_element_type=F32)
            n_s[bb, h] = w_c * n_prev + jnp.sum(kw, axis=0, keepdims=True)
            m_s[bb, h] = jnp.broadcast_to(m_new, (1, LANES))

    @pl.when(step == pl.num_programs(1) - 1)
    def _():
        cn_ref[...] = c_s[...]
        nn_ref[...] = n_s[...]
        mn_ref[...] = m_s[...]


def _mlstm(qkv, ogate, gates, c0, n0, m0):
    b, t, _ = qkv.shape
    blk = min(CHUNK, t)
    bb_n = 4
    state_spec = lambda shape: pl.BlockSpec((bb_n,) + shape, lambda i, j: (i, 0, 0, 0))
    return pl.pallas_call(
        functools.partial(_mlstm_kernel, bb_n=bb_n, blk=blk),
        grid=(b // bb_n, t // blk),
        in_specs=[pl.BlockSpec((bb_n, blk, 3 * W_M), lambda i, j: (i, j, 0)),
                  pl.BlockSpec((bb_n, blk, W_M), lambda i, j: (i, j, 0)),
                  pl.BlockSpec((bb_n, blk, GATE_COLS), lambda i, j: (i, j, 0)),
                  state_spec((H_M, DH_M, DH_M)),
                  state_spec((H_M, 1, DH_M)),
                  state_spec((H_M, 1, LANES))],
        out_specs=[pl.BlockSpec((bb_n, blk, W_M), lambda i, j: (i, j, 0)),
                   state_spec((H_M, DH_M, DH_M)),
                   state_spec((H_M, 1, DH_M)),
                   state_spec((H_M, 1, LANES))],
        out_shape=[jax.ShapeDtypeStruct((b, t, W_M), BF16),
                   jax.ShapeDtypeStruct((b, H_M, DH_M, DH_M), F32),
                   jax.ShapeDtypeStruct((b, H_M, 1, DH_M), F32),
                   jax.ShapeDtypeStruct((b, H_M, 1, LANES), F32)],
        scratch_shapes=[pltpu.VMEM((bb_n, H_M, DH_M, DH_M), F32),
                        pltpu.VMEM((bb_n, H_M, 1, DH_M), F32),
                        pltpu.VMEM((bb_n, H_M, 1, LANES), F32)],
        compiler_params=_cparams("parallel", "arbitrary"),
        name="mlstm",
    )(qkv, ogate, gates, c0, n0, m0)


def _online_softmax_step(s, pv, m_ref, l_ref, acc_ref):
    m_old = m_ref[...]
    m_new = jnp.maximum(m_old, jnp.max(s, axis=1, keepdims=True))
    alpha = jnp.exp(m_old - m_new)
    p = jnp.exp(s - m_new)
    l_ref[...] = alpha * l_ref[...] + jnp.sum(p, axis=1, keepdims=True)
    acc_ref[...] = alpha * acc_ref[...] + pv(p.astype(BF16))
    m_ref[...] = m_new


def _split3(x):
    hi = x.astype(BF16).astype(F32)
    mid = (x - hi).astype(BF16).astype(F32)
    lo = (x - hi - mid).astype(BF16).astype(F32)
    return hi, mid, lo


FOX_V_ROWS = DH_F + 16
AUG_CQ = 6


def _fox_prompt_kernel(q_ref, k_ref, v_ref, c_ref, after_ref, o_ref, kaug_s, vt_s, m0_s, m1_s, acc0_s,
                       acc1_s, *, tq, q_first, n_blocks):
    del after_ref
    hp = pl.program_id(1)
    qi = pl.program_id(2) + q_first
    lane = lax.broadcasted_iota(jnp.int32, (tq, LANES), 1)
    m_s, acc_s = (m0_s, m1_s), (acc0_s, acc1_s)

    def head_col(c_blk, x):
        return jnp.sum(jnp.where(lane == COL_FF + 2 * hp + x, c_blk, 0.0), axis=1, keepdims=True)

    def aug_tile(entries):
        tile = jnp.zeros((tq, LANES), F32)
        for l, val in entries:
            tile = jnp.where(lane == l, val, tile)
        return tile.astype(BF16)

    @pl.when(pl.program_id(2) == 0)
    def _():
        ones_rows = jnp.where(lax.broadcasted_iota(jnp.int32, (FOX_V_ROWS - DH_F, tq), 0) == 0, 1.0, 0.0)

        def chunk(i, carry):
            rs = pl.multiple_of(i * tq, tq)
            c_blk = c_ref[pl.ds(rs, tq), :]
            pieces = _split3(head_col(c_blk, 0)) + _split3(head_col(c_blk, 1))
            entries = list(enumerate(pieces)) + [(AUG_CQ + j, 1.0) for j in range(3)]
            kaug_s[pl.ds(rs, tq), 0:LANES] = k_ref[pl.ds(rs, tq), :]
            kaug_s[pl.ds(rs, tq), LANES:2 * LANES] = aug_tile(entries)
            vt = v_ref[pl.ds(rs, tq), :].astype(F32).T
            for x in range(2):
                vt_s[x, 0:DH_F, pl.ds(rs, tq)] = vt[x * DH_F:(x + 1) * DH_F, :].astype(BF16)
                vt_s[x, DH_F:FOX_V_ROWS, pl.ds(rs, tq)] = ones_rows.astype(BF16)
            return carry

        lax.fori_loop(0, n_blocks, chunk, 0)

    qs = pl.multiple_of(qi * tq, tq)
    c_q = c_ref[pl.ds(qs, tq), :]
    q2 = q_ref[...] * (DH_F ** -0.5)
    q_aug = []
    for x in range(2):
        cq3 = _split3(head_col(c_q, x))
        entries = [(3 * x + j, -1.0) for j in range(3)] + [(AUG_CQ + j, cq3[j]) for j in range(3)]
        q_head = jnp.where((lane < DH_F) == (x == 0), q2, jnp.zeros_like(q2))
        q_aug.append(jnp.concatenate([q_head, aug_tile(entries)], axis=1))
        m_s[x][...] = jnp.full(m_s[x].shape, NEG_INF, F32)
        acc_s[x][...] = jnp.zeros(acc_s[x].shape, F32)

    def block(kb, masked):
        ks = pl.multiple_of(kb * tq, tq)
        k_blk = kaug_s[pl.ds(ks, tq), :]
        scores = [lax.dot_general(k_blk, q_aug[x], (((1,), (1,)), ((), ())), preferred_element_type=F32)
                  for x in range(2)]
        for x in range(2):
            st = scores[x]
            if masked:
                k_pos = lax.broadcasted_iota(jnp.int32, (tq, tq), 0)
                q_pos = lax.broadcasted_iota(jnp.int32, (tq, tq), 1)
                st = jnp.where(k_pos <= q_pos, st, NEG_INF)
            m_old = m_s[x][...]
            m_new = jnp.maximum(m_old, jnp.max(st, axis=0, keepdims=True))
            p = jnp.exp(st - m_new).astype(BF16)
            acc_s[x][...] = (jnp.exp(m_old - m_new) * acc_s[x][...]
                             + jnp.dot(vt_s[x, :, pl.ds(ks, tq)], p, preferred_element_type=F32))
            m_s[x][...] = m_new

    def body(kb, carry):
        block(kb, False)
        return carry

    lax.fori_loop(0, qi, body, 0)
    block(qi, True)
    out_t = jnp.concatenate([acc_s[x][0:DH_F, :] / acc_s[x][DH_F:DH_F + 1, :] for x in range(2)], axis=0)
    o_ref[...] = out_t.T.astype(o_ref.dtype)


def _fox_prompt(q, k, v, c, batch, first_token, tokens, after):
    _, t, _ = q.shape
    tq = 512
    q_first, n_q = first_token // tq, tokens // tq
    head_pair = lambda i, h, j: (batch, 0, h)
    return pl.pallas_call(
        functools.partial(_fox_prompt_kernel, tq=tq, q_first=q_first, n_blocks=q_first + n_q),
        grid=(1, H_F // 2, n_q),
        in_specs=[pl.BlockSpec((None, tq, LANES), lambda i, h, j: (batch, q_first + j, h)),
                  pl.BlockSpec((None, t, LANES), head_pair),
                  pl.BlockSpec((None, t, LANES), head_pair),
                  pl.BlockSpec((None, t, LANES), lambda i, h, j: (batch, 0, 0)),
                  ORDERED],
        out_specs=pl.BlockSpec((tq, LANES), lambda i, h, j: (j, h)),
        out_shape=jax.ShapeDtypeStruct((tokens, W_F), BF16),
        scratch_shapes=[pltpu.VMEM((t, 2 * LANES), BF16),
                        pltpu.VMEM((2, FOX_V_ROWS, t), BF16),
                        pltpu.VMEM((1, tq), F32), pltpu.VMEM((1, tq), F32),
                        pltpu.VMEM((FOX_V_ROWS, tq), F32), pltpu.VMEM((FOX_V_ROWS, tq), F32)],
        compiler_params=_cparams("parallel", "parallel", "arbitrary"),
        name="fox_prompt",
    )(q, k, v, c, after)


def _fox_sample_kernel(q_ref, kc_ref, vc_ref, kn_ref, vn_ref, cq_ref, ctc_ref, ctn_ref, after_ref, o_ref,
                       m_s, l_s, acc_s, *, tn):
    del after_ref
    kb = pl.program_id(1)

    @pl.when(kb == 0)
    def _():
        m_s[...] = jnp.full_like(m_s, NEG_INF)
        l_s[...] = jnp.zeros_like(l_s)
        acc_s[...] = jnp.zeros_like(acc_s)

    lane = lax.broadcasted_iota(jnp.int32, (tn, LANES), 1)
    c_blk = cq_ref[...]
    nt = (((1,), (1,)), ((), ()))

    def head_query(hd):
        q = q_ref[:, hd * DH_F:(hd + 1) * DH_F] * (DH_F ** -0.5)
        cq = jnp.sum(jnp.where(lane == COL_FF + hd, c_blk, 0.0), axis=1, keepdims=True)
        return q, cq

    for hd in range(H_F):
        q, cq = head_query(hd)
        k_t = kc_ref[hd].astype(BF16)
        v_t = vc_ref[hd].astype(BF16)
        s = jnp.dot(q, k_t, preferred_element_type=F32) + cq - ctc_ref[hd // 2, hd % 2:hd % 2 + 1, :]
        _online_softmax_step(s, lambda p: lax.dot_general(p, v_t, nt, preferred_element_type=F32),
                             m_s.at[hd], l_s.at[hd], acc_s.at[hd])

    @pl.when(kb == pl.num_programs(1) - 1)
    def _():
        q_pos = lax.broadcasted_iota(jnp.int32, (tn, tn), 0)
        k_pos = lax.broadcasted_iota(jnp.int32, (tn, tn), 1)
        for hd in range(H_F):
            q, cq = head_query(hd)
            k_new = kn_ref[:, hd * DH_F:(hd + 1) * DH_F]
            v_new = vn_ref[:, hd * DH_F:(hd + 1) * DH_F]
            s = (lax.dot_general(q, k_new, nt, preferred_element_type=F32)
                 + cq - ctn_ref[hd // 2, hd % 2:hd % 2 + 1, :])
            s = jnp.where(k_pos <= q_pos, s, NEG_INF)
            _online_softmax_step(s, lambda p: jnp.dot(p, v_new, preferred_element_type=F32),
                                 m_s.at[hd], l_s.at[hd], acc_s.at[hd])
            o_ref[:, hd * DH_F:(hd + 1) * DH_F] = (acc_s[hd] / l_s[hd]).astype(o_ref.dtype)


def _fox_sample(q, k_cache_t, v_cache_t, k_new, v_new, cq, ct_cache, ct_new, after):
    b, tn, _ = q.shape
    p = k_cache_t.shape[3]
    tk = 1024
    new = pl.BlockSpec((None, tn, W_F), lambda i, j: (i, 0, 0))
    cache = pl.BlockSpec((None, H_F, DH_F, tk), lambda i, j: (i, 0, 0, j))
    return pl.pallas_call(
        functools.partial(_fox_sample_kernel, tn=tn),
        grid=(b, p // tk),
        in_specs=[new, cache, cache, new, new,
                  pl.BlockSpec((None, tn, LANES), lambda i, j: (i, 0, 0)),
                  pl.BlockSpec((None, H_F // 2, 2, tk), lambda i, j: (i, 0, 0, j)),
                  pl.BlockSpec((None, H_F // 2, 2, tn), lambda i, j: (i, 0, 0, 0)),
                  ORDERED],
        out_specs=new,
        out_shape=jax.ShapeDtypeStruct((b, tn, W_F), BF16),
        scratch_shapes=[pltpu.VMEM((H_F, tn, 1), F32),
                        pltpu.VMEM((H_F, tn, 1), F32),
                        pltpu.VMEM((H_F, tn, DH_F), F32)],
        compiler_params=_cparams("parallel", "arbitrary"),
        name="fox_sample",
    )(q, k_cache_t, v_cache_t, k_new, v_new, cq, ct_cache, ct_new, after)


def _merge_kernel(h_ref, ym_ref, yf_ref, gm_ref, gf_ref, wm_ref, wf_ref, wo_ref, after_ref, o_ref):
    del after_ref
    a = jnp.dot(ym_ref[...], wm_ref[...], preferred_element_type=F32)
    b = jnp.dot(yf_ref[...], wf_ref[...], preferred_element_type=F32)
    merge = _sigmoid(gm_ref[...]) * a + _sigmoid(gf_ref[...]) * b
    o_ref[...] = h_ref[...] + jnp.dot(merge.astype(BF16), wo_ref[...], preferred_element_type=F32)


def _merge(h, ym, yf, gates, wm, wf, wo, first_token, after):
    n, d = yf.shape
    tm = min(512, n)
    first = first_token // tm
    tok = lambda c: pl.BlockSpec((tm, d), lambda i: (first + i, c))
    local = pl.BlockSpec((tm, d), lambda i: (i, 0))
    wspec = pl.BlockSpec((d, d), lambda i: (0, 0))
    return pl.pallas_call(
        _merge_kernel,
        grid=(n // tm,),
        in_specs=[tok(0), tok(0), local, tok(1), tok(2), wspec, wspec, wspec, ORDERED],
        out_specs=local,
        out_shape=jax.ShapeDtypeStruct((n, d), F32),
        compiler_params=_cparams("parallel"),
        name="merge",
    )(h, ym, yf, gates, gates, wm, wf, wo, after)


def _top_rows(s, count, ids=None, payload=None):
    if ids is None:
        ids = lax.broadcasted_iota(jnp.int32, s.shape, 0)
    big = jnp.int32(2 ** 30)
    vals, sel, pay = [], [], []
    for _ in range(count):
        m = jnp.max(s, axis=0, keepdims=True)
        am = jnp.min(jnp.where(s == m, ids, big), axis=0, keepdims=True)
        hit = ids == am
        vals.append(m)
        sel.append(am)
        if payload is not None:
            pay.append(jnp.max(jnp.where(hit, payload, -1), axis=0, keepdims=True))
        s = jnp.where(hit, NEG_INF, s)
    cat = lambda xs: jnp.concatenate(xs, axis=0)
    return cat(vals), cat(sel), (cat(pay) if payload is not None else None)


def _pair_candidates(v1, i1, v2, i2):
    t = v1.shape[1]
    half = PEER_TOPK // 2
    r16 = lax.broadcasted_iota(jnp.int32, (PEER_TOPK, t), 0)
    r8 = lax.broadcasted_iota(jnp.int32, (half, t), 0)
    sums = [v1[0:1, :] + v2]
    flat = [r16]
    expert = [i1[0:1, :] * PEER_KEYS + i2]
    for a in range(1, half):
        sums.append(v1[a:a + 1, :] + v2[0:half, :])
        flat.append(r8 + a * PEER_TOPK)
        expert.append(i1[a:a + 1, :] * PEER_KEYS + i2[0:half, :])
    sums.append(v1[half:, :] + v2[0:1, :])
    flat.append((r8 + half) * PEER_TOPK)
    expert.append(i1[half:, :] * PEER_KEYS + i2[0:1, :])
    cat = lambda xs: jnp.concatenate(xs, axis=0)
    return cat(sums), cat(flat), cat(expert)


def _peer_score_kernel(h_ref, g_ref, wqt_ref, k1_ref, k2_ref, xn_ref, idx_ref, gate_ref):
    xn = _rms(h_ref[...], g_ref[...])
    xn_ref[...] = xn.astype(xn_ref.dtype)
    qt = lax.dot_general(wqt_ref[...], xn, (((1,), (1,)), ((), ())),
                         precision=HIGHEST, preferred_element_type=F32)
    half = PEER_DQ // 2
    idx_rows, gate_rows = [], []
    for hd in range(PEER_HEADS):
        q1 = qt[hd * PEER_DQ:hd * PEER_DQ + half, :]
        q2 = qt[hd * PEER_DQ + half:(hd + 1) * PEER_DQ, :]
        s1 = jnp.dot(k1_ref[hd], q1, precision=HIGHEST, preferred_element_type=F32)
        s2 = jnp.dot(k2_ref[hd], q2, precision=HIGHEST, preferred_element_type=F32)
        v1, i1, _ = _top_rows(s1, PEER_TOPK)
        v2, i2, _ = _top_rows(s2, PEER_TOPK)
        cand, flat, expert = _pair_candidates(v1, i1, v2, i2)
        sc, _, ex = _top_rows(cand, PEER_TOPK, ids=flat, payload=expert)
        e = jnp.exp(sc - sc[0:1, :])
        gate_rows.append(e / jnp.sum(e, axis=0, keepdims=True))
        idx_rows.append(ex)
    gate_ref[...] = jnp.concatenate(gate_rows, axis=0).T
    idx_ref[...] = jnp.concatenate(idx_rows, axis=0).T


def _peer_score(h, g, wqt, k1, k2):
    n, d = h.shape
    tt = min(256, n)
    return pl.pallas_call(
        _peer_score_kernel,
        grid=(n // tt,),
        in_specs=[pl.BlockSpec((tt, d), lambda i: (i, 0)),
                  pl.BlockSpec((1, d), lambda i: (0, 0)),
                  pl.BlockSpec(wqt.shape, lambda i: (0, 0)),
                  pl.BlockSpec(k1.shape, lambda i: (0, 0, 0)),
                  pl.BlockSpec(k2.shape, lambda i: (0, 0, 0))],
        out_specs=[pl.BlockSpec((tt, d), lambda i: (i, 0)),
                   pl.BlockSpec((tt, PEER_SEL), lambda i: (i, 0)),
                   pl.BlockSpec((tt, PEER_SEL), lambda i: (i, 0))],
        out_shape=[jax.ShapeDtypeStruct((n, d), F32),
                   jax.ShapeDtypeStruct((n, PEER_SEL), jnp.int32),
                   jax.ShapeDtypeStruct((n, PEER_SEL), F32)],
        compiler_params=_cparams("parallel"),
        name="peer_score",
    )(h, g, wqt, k1, k2)


SC_CORES = 2
SC_SUBCORES = 16
SC_WORKERS = SC_CORES * SC_SUBCORES
SC_ROWS = 64


def _pack_table(tab):
    half = tab.shape[1] // 2
    bits = lax.bitcast_convert_type(tab.astype(BF16), jnp.uint16).astype(jnp.uint32)
    word = bits[:, :half] | (bits[:, half:] << 16)
    return lax.bitcast_convert_type(word, jnp.int32)


SC_TOK = 8
SC_LANES = 16
STEPS_PER_TOKEN = PEER_SEL // SC_ROWS
SC_PARAMS = pltpu.CompilerParams(needs_layout_passes=False)


def _row_source(table_hbm, idx_v, local_step, global_step):
    del global_step
    return table_hbm.at[idx_v.at[local_step]]


def _sc_unpack(wd):
    lo = lax.bitcast_convert_type(wd << 16, F32)
    hi = lax.bitcast_convert_type(wd & jnp.int32(-65536), F32)
    return lo, hi


def _sc_token_blocks(n, body_block):
    per_worker = n // SC_WORKERS
    assert per_worker * SC_WORKERS == n and per_worker % SC_TOK == 0
    wid = lax.axis_index("s") * SC_CORES + lax.axis_index("c")

    @pl.loop(0, per_worker // SC_TOK)
    def _(blk):
        body_block(wid * per_worker + blk * SC_TOK)


def _sc_pipelined_steps(table_hbm, idx_v, rows_v, sems, first_step, consume):
    n_steps = SC_TOK * STEPS_PER_TOKEN

    def row_gather(j, slot):
        return pltpu.make_async_copy(_row_source(table_hbm, idx_v, j, first_step + j),
                                     rows_v.at[slot], sems.at[slot])

    row_gather(0, 0).start()

    @pl.loop(0, n_steps, step=2)
    def _(i):
        for slot in range(2):
            j = i + slot

            @pl.when(j + 1 < n_steps)
            def _():
                row_gather(j + 1, 1 - slot).start()

            row_gather(j, slot).wait()
            consume(slot, i // 2, slot)


def _sc_expert_dot(table, idx_steps, x):
    n, d = x.shape
    w = d // 2
    assert STEPS_PER_TOKEN == 2 and table.shape[1] == w
    mesh = plsc.VectorSubcoreMesh(core_axis_name="c", subcore_axis_name="s")
    n_steps = SC_TOK * STEPS_PER_TOKEN
    group = 4

    @functools.partial(
        pl.kernel, mesh=mesh,
        out_type=jax.ShapeDtypeStruct((n, PEER_SEL), F32),
        scratch_types=[pltpu.VMEM((n_steps, SC_ROWS), jnp.int32),
                       pltpu.VMEM((SC_TOK, d), F32),
                       pltpu.VMEM((2, SC_ROWS, w), jnp.int32),
                       pltpu.VMEM((SC_TOK, PEER_SEL), F32),
                       pltpu.SemaphoreType.DMA((2,))],
        compiler_params=SC_PARAMS,
    )
    def expert_dot(table_hbm, idx_hbm, x_hbm, act_hbm, idx_v, x_v, rows_v, act_v, sems):
        lanes = lax.iota(jnp.int32, SC_LANES)
        zero = jnp.zeros((SC_LANES,), F32)

        def consume(slot, tl, half):
            @pl.loop(0, SC_ROWS // SC_LANES)
            def _(g):
                act_vec = zero
                for q in range(SC_LANES // group):
                    r0 = g * SC_LANES + q * group

                    def chunk(j, accs):
                        c = pl.multiple_of(j * SC_LANES, SC_LANES)
                        x_lo = x_v[tl, pl.ds(c, SC_LANES)]
                        x_hi = x_v[tl, pl.ds(w + c, SC_LANES)]
                        out = []
                        for rr in range(group):
                            lo, hi = _sc_unpack(rows_v[slot, r0 + rr, pl.ds(c, SC_LANES)])
                            out.append(accs[rr] + lo * x_lo + hi * x_hi)
                        return tuple(out)

                    accs = lax.fori_loop(0, w // SC_LANES, chunk, (zero,) * group, unroll=2)
                    for rr in range(group):
                        act_vec = jnp.where(lanes == q * group + rr, jnp.sum(accs[rr]), act_vec)
                act_v[tl, pl.ds(half * SC_ROWS + g * SC_LANES, SC_LANES)] = act_vec

        def block(t0):
            s0 = t0 * STEPS_PER_TOKEN
            pltpu.sync_copy(idx_hbm.at[pl.ds(s0, n_steps)], idx_v)
            pltpu.sync_copy(x_hbm.at[pl.ds(t0, SC_TOK)], x_v)
            _sc_pipelined_steps(table_hbm, idx_v, rows_v, sems, s0, consume)
            pltpu.sync_copy(act_v, act_hbm.at[pl.ds(t0, SC_TOK)])

        _sc_token_blocks(n, block)

    return expert_dot(table, idx_steps, x)


def _sc_expert_sum(table, idx_steps, wgt):
    n = wgt.shape[0]
    w = table.shape[1]
    d = 2 * w
    assert STEPS_PER_TOKEN == 2
    mesh = plsc.VectorSubcoreMesh(core_axis_name="c", subcore_axis_name="s")
    n_steps = SC_TOK * STEPS_PER_TOKEN
    cols = 8

    @functools.partial(
        pl.kernel, mesh=mesh,
        out_type=jax.ShapeDtypeStruct((n, d), F32),
        scratch_types=[pltpu.VMEM((n_steps, SC_ROWS), jnp.int32),
                       pltpu.VMEM((SC_TOK, PEER_SEL), F32),
                       pltpu.VMEM((2, SC_ROWS, w), jnp.int32),
                       pltpu.VMEM((SC_TOK, d), F32),
                       pltpu.VMEM((SC_ROWS, SC_LANES), F32),
                       pltpu.SemaphoreType.DMA((2,))],
        compiler_params=SC_PARAMS,
    )
    def expert_sum(table_hbm, idx_hbm, wgt_hbm, out_hbm, idx_v, wgt_v, rows_v, out_v, splat_v, sems):
        zero = jnp.zeros((SC_LANES,), F32)
        lanes = lax.iota(jnp.int32, SC_LANES)

        def consume(slot, tl, half):
            @pl.loop(0, SC_ROWS // SC_LANES)
            def _(g):
                w16 = wgt_v[tl, pl.ds(half * SC_ROWS + g * SC_LANES, SC_LANES)]
                for rr in range(SC_LANES):
                    one = jnp.sum(jnp.where(lanes == rr, w16, 0.0))
                    splat_v[g * SC_LANES + rr, :] = jnp.full((SC_LANES,), one, F32)

            for cb in range(w // (cols * SC_LANES)):
                base = cb * cols * SC_LANES

                def row(r, accs):
                    wv = splat_v[r, :]
                    out = []
                    for jj in range(cols):
                        lo, hi = _sc_unpack(rows_v[slot, r, pl.ds(base + jj * SC_LANES, SC_LANES)])
                        out.append(accs[2 * jj] + wv * lo)
                        out.append(accs[2 * jj + 1] + wv * hi)
                    return tuple(out)

                accs = lax.fori_loop(0, SC_ROWS, row, (zero,) * (2 * cols), unroll=2)
                for jj in range(cols):
                    c = base + jj * SC_LANES
                    if half == 0:
                        out_v[tl, pl.ds(c, SC_LANES)] = accs[2 * jj]
                        out_v[tl, pl.ds(w + c, SC_LANES)] = accs[2 * jj + 1]
                    else:
                        out_v[tl, pl.ds(c, SC_LANES)] = out_v[tl, pl.ds(c, SC_LANES)] + accs[2 * jj]
                        out_v[tl, pl.ds(w + c, SC_LANES)] = out_v[tl, pl.ds(w + c, SC_LANES)] + accs[2 * jj + 1]

        def block(t0):
            s0 = t0 * STEPS_PER_TOKEN
            pltpu.sync_copy(idx_hbm.at[pl.ds(s0, n_steps)], idx_v)
            pltpu.sync_copy(wgt_hbm.at[pl.ds(t0, SC_TOK)], wgt_v)
            _sc_pipelined_steps(table_hbm, idx_v, rows_v, sems, s0, consume)
            pltpu.sync_copy(out_v, out_hbm.at[pl.ds(t0, SC_TOK)])

        _sc_token_blocks(n, block)

    return expert_sum(table, idx_steps, wgt)


def _gelu_exact(x):
    return 0.5 * x * (1.0 + lax.erf(x * (2.0 ** -0.5)))


def _peer_weight_kernel(act_ref, gate_ref, after_ref, o_ref):
    del after_ref
    o_ref[...] = gate_ref[...] * _gelu_exact(act_ref[...])


def _peer_weight(act, gate, after):
    n = act.shape[0]
    tt = min(2048, n)
    spec = pl.BlockSpec((tt, PEER_SEL), lambda i: (i, 0))
    return pl.pallas_call(
        _peer_weight_kernel,
        grid=(n // tt,),
        in_specs=[spec, spec, ORDERED],
        out_specs=spec,
        out_shape=jax.ShapeDtypeStruct((n, PEER_SEL), F32),
        compiler_params=_cparams("parallel"),
        name="peer_weight",
    )(act, gate, after)


def _ple_kernel(h_ref, peer_ref, p_ref, gp_ref, gfin_ref, wg_ref, wp_ref, o_ref):
    h = h_ref[...] + peer_ref[...]
    e = _rms(h, gp_ref[...]).astype(BF16)
    gate = _sigmoid(jnp.dot(e, wg_ref[...], preferred_element_type=F32))
    proj = jnp.dot(p_ref[...].astype(BF16), wp_ref[...], preferred_element_type=F32)
    o_ref[...] = _rms(h + gate * proj, gfin_ref[...])


def _ple_final(h, peer, p, g_ple, g_final, wg, wp, first_token):
    n, d = h.shape
    tm = min(512, n)
    first = first_token // tm
    return pl.pallas_call(
        _ple_kernel,
        grid=(n // tm,),
        in_specs=[pl.BlockSpec((tm, d), lambda i: (i, 0)),
                  pl.BlockSpec((tm, d), lambda i: (i, 0)),
                  pl.BlockSpec((tm, D_PLE), lambda i: (first + i, 0)),
                  pl.BlockSpec((1, d), lambda i: (0, 0)),
                  pl.BlockSpec((1, d), lambda i: (0, 0)),
                  pl.BlockSpec((d, d), lambda i: (0, 0)),
                  pl.BlockSpec((D_PLE, d), lambda i: (0, 0))],
        out_specs=pl.BlockSpec((tm, d), lambda i: (i, 0)),
        out_shape=jax.ShapeDtypeStruct((n, d), F32),
        compiler_params=_cparams("parallel"),
        name="ple_final",
    )(h, peer, p, g_ple, g_final, wg, wp)


def _prep_weights(norm_mix, w_in, mlstm_b_i, mlstm_b_f, fox_b_f, w_br_m, w_br_f, w_out, norm_ffn,
                  peer_w_q, peer_keys1, peer_keys2, peer_u, peer_v, norm_ple, w_ple_gate, w_ple_proj,
                  norm_final):
    o = [0]
    for s in (W_M, W_M, W_M, W_M, H_M, H_M, W_F, W_F, W_F, H_F, D_MODEL, D_MODEL):
        o.append(o[-1] + s)
    seg = lambda a, b: w_in[:, o[a]:o[b]]
    w_gate = jnp.concatenate([seg(4, 6), seg(9, 10)], axis=1)
    w_gate = jnp.pad(w_gate, ((0, 0), (0, GATE_COLS - w_gate.shape[1])))
    b_gate = jnp.concatenate([mlstm_b_i, mlstm_b_f, fox_b_f]).astype(F32)
    b_gate = jnp.pad(b_gate, (0, GATE_COLS - b_gate.shape[0]))[None, :]
    row = lambda v: v.astype(F32)[None, :]
    return dict(
        norm_mix=row(norm_mix),
        w_mqkv=seg(0, 3).astype(BF16),
        w_og=jnp.concatenate([seg(3, 4), seg(10, 12)], axis=1).astype(BF16),
        w_fq=seg(6, 7).astype(BF16), w_fk=seg(7, 8).astype(BF16), w_fv=seg(8, 9).astype(BF16),
        w_gate=w_gate.astype(F32), b_gate=b_gate,
        w_br_m=w_br_m.astype(BF16), w_br_f=w_br_f.astype(BF16), w_out=w_out.astype(BF16),
        norm_ffn=row(norm_ffn), wqt=peer_w_q.T.astype(F32),
        k1=peer_keys1.astype(F32), k2=peer_keys2.astype(F32),
        u_pk=_pack_table(peer_u), v_pk=_pack_table(peer_v),
        norm_ple=row(norm_ple), w_ple_gate=w_ple_gate.astype(BF16), w_ple_proj=w_ple_proj.astype(BF16),
        norm_final=row(norm_final),
    )


def _ct_rows(c):
    b, t, _ = c.shape
    return jnp.swapaxes(c[:, :, COL_FF:COL_FF + H_F], 1, 2).reshape(b, H_F // 2, 2, t)


def _peer_first_pass(h, w):
    n = h.shape[0]
    xn, idx, gate = _peer_score(h, w["norm_ffn"], w["wqt"], w["k1"], w["k2"])
    steps = idx.reshape(n * STEPS_PER_TOKEN, SC_ROWS)
    return steps, gate, _sc_expert_dot(w["u_pk"], steps, xn)


def _peer_second_pass(first, w, after):
    steps, gate, act = first
    wgt = _peer_weight(act, gate, after)
    return _sc_expert_sum(w["v_pk"], steps, wgt), wgt


def _layer(x, p, w, mstate, fox_cache, after):
    b, t, d = x.shape
    n = b * t
    h = x.reshape(n, d)
    g = w["norm_mix"]
    qkv = _norm_matmul(h, g, w["w_mqkv"], BF16, "proj_mlstm_qkv")
    og = _norm_matmul(h, g, w["w_og"], F32, "proj_gates")
    fq = _norm_matmul(h, g, w["w_fq"], BF16, "proj_fox_q")
    if fox_cache is None:
        fk, fk_t = _norm_matmul_tokens_minor(h, g, w["w_fk"], t, "proj_fox_k")
        fv, fv_t = _norm_matmul_tokens_minor(h, g, w["w_fv"], t, "proj_fox_v")
        fk_state, fv_state = (jnp.transpose(a, (0, 3, 1, 2))[None] for a in (fk_t, fv_t))
    else:
        fk, fk_heads = _norm_matmul_heads(h, g, w["w_fk"], "proj_fox_k")
        fv, fv_heads = _norm_matmul_heads(h, g, w["w_fv"], "proj_fox_v")
        fk_state, fv_state = (a.reshape(1, b, t, H_F, DH_F) for a in (fk_heads, fv_heads))
    gates = _gates(h, g, w["w_gate"], w["b_gate"])

    c0, n0, m0 = mstate
    ym, c_new, n_new, m_new = _mlstm(
        qkv.reshape(b, t, 3 * W_M), og.reshape(b, t, 3 * D_MODEL), gates.reshape(b, t, GATE_COLS),
        c0.astype(F32), n0.astype(F32)[:, :, None, :],
        jnp.broadcast_to(m0.astype(F32)[:, :, None, None], (b, H_M, 1, LANES)))

    def merged(yf, first_token, after):
        return _merge(h, ym.reshape(n, W_M), yf, og, w["w_br_m"], w["w_br_f"], w["w_out"], first_token, after)

    def output(hm, peer, first_token):
        return _ple_final(hm, peer, p.reshape(n, D_PLE), w["norm_ple"], w["norm_final"],
                          w["w_ple_gate"], w["w_ple_proj"], first_token)

    neutral = w["norm_ffn"]
    gates3 = gates.reshape(b, t, GATE_COLS)
    if fox_cache is None:
        c = _cumsum_tokens(gates3)
        seqs = (fq.reshape(b, t, W_F), fk.reshape(b, t, W_F), fv.reshape(b, t, W_F))
        segs = [(0, 0, t // 2), (0, t // 2, t // 2)] + [(bi, 0, t) for bi in range(1, b)]
        hms, firsts, peers, wgts = [], [], [], []
        for s, (bi, start, count) in enumerate(segs):
            yf = _fox_prompt(*seqs, c, bi, start, count, peers[s - 2] if s >= 2 else neutral)
            if s >= 1:
                peer, wgt = _peer_second_pass(firsts[s - 1], w, yf)
                peers.append(peer)
                wgts.append(wgt)
            hms.append(merged(yf, bi * t + start, wgts[s - 1] if s >= 1 else neutral))
            firsts.append(_peer_first_pass(hms[s], w))
        peer, wgt = _peer_second_pass(firsts[-1], w, neutral)
        peers.append(peer)
        wgts.append(wgt)
        y = jnp.concatenate([output(hms[s], peers[s], bi * t + start)
                             for s, (bi, start, _) in enumerate(segs)], axis=0)
        tail = (peers[-2], firsts[-1][1], wgts[-1], peers[-1])
    else:
        ck, cv, clf = fox_cache
        past = ck.shape[1]
        lf = jnp.pad(clf.astype(F32), ((0, 0), (0, 0), (COL_FF, GATE_COLS - COL_FF - H_F)))
        lf = jnp.concatenate([lf, gates3], axis=1)
        pad_t = (-lf.shape[1]) % 256
        c = _cumsum_tokens(jnp.pad(lf, ((0, 0), (0, pad_t), (0, 0))))
        ct = _ct_rows(c)
        keys_minor = lambda cache: jnp.transpose(cache.astype(F32), (0, 2, 3, 1))
        yf = _fox_sample(fq.reshape(b, t, W_F), keys_minor(ck), keys_minor(cv),
                         fk.reshape(b, t, W_F), fv.reshape(b, t, W_F),
                         c[:, past:past + t, :], ct[..., :past], ct[..., past:past + t], after[0])
        hm = merged(yf.reshape(n, W_F), 0, after[1])
        peer, wgt = _peer_second_pass(_peer_first_pass(hm, w), w, neutral)
        tail = (peer, wgt, wgt, peer)
        y = output(hm, peer, 0)

    state = (fk_state, fv_state,
             gates3[None, :, :, COL_FF:COL_FF + H_F],
             c_new[None], n_new[None, :, :, 0, :], m_new[None, :, :, 0, 0])
    return y.reshape(b, t, d), state, tail


def kernel(x_prompt, x_sample, p_prompt, p_sample, cache_fox_k, cache_fox_v, cache_fox_logf, state_mlstm_C, state_mlstm_n, state_mlstm_m, norm_mix, w_in, mlstm_b_i, mlstm_b_f, fox_b_f, w_br_m, w_br_f, w_out, norm_ffn, peer_w_q, peer_keys1, peer_keys2, peer_u, peer_v, norm_ple, w_ple_gate, w_ple_proj, norm_final):
    assert w_in.shape[0] == 1, "single-layer trunk"
    w = _prep_weights(norm_mix[0], w_in[0], mlstm_b_i[0], mlstm_b_f[0], fox_b_f[0], w_br_m[0], w_br_f[0],
                      w_out[0], norm_ffn[0], peer_w_q[0], peer_keys1[0], peer_keys2[0], peer_u[0], peer_v[0],
                      norm_ple[0], w_ple_gate[0], w_ple_proj[0], norm_final)
    bp = x_prompt.shape[0]
    zeros = (jnp.zeros((bp, H_M, DH_M, DH_M), F32), jnp.zeros((bp, H_M, DH_M), F32), jnp.zeros((bp, H_M), F32))
    y_p, sp, tail = _layer(x_prompt, p_prompt[0], w, zeros, None, None)
    bs, ts, d = x_sample.shape
    x_s = _ordered(x_sample.reshape(bs * ts, d), tail[0], tail[1]).reshape(bs, ts, d)
    y_s, ss, _ = _layer(x_s, p_sample[0], w,
                        (state_mlstm_C[0], state_mlstm_n[0], state_mlstm_m[0]),
                        (cache_fox_k[0], cache_fox_v[0], cache_fox_logf[0]), tail[2:])
    return (y_p, y_s) + sp + ss
```

```python
import functools

import jax
import jax.numpy as jnp
from jax import lax
from jax.experimental import pallas as pl
from jax.experimental.pallas import tpu as pltpu
from jax.experimental.pallas import tpu_sc as plsc

D_MODEL = 1024
CHUNK = 64
H_M = 4
DH_M = 256
W_M = H_M * DH_M
H_F = 16
DH_F = 64
W_F = H_F * DH_F
D_PLE = 256
PEER_HEADS = 8
PEER_KEYS = 128
PEER_DQ = 256
PEER_TOPK = 16
PEER_SEL = PEER_HEADS * PEER_TOPK
EPS = 1e-6

LANES = 128
GATE_COLS = LANES
COL_I, COL_F, COL_FF = 0, H_M, 2 * H_M
VMEM_LIMIT = 56 * 1024 * 1024
HIGHEST = lax.Precision.HIGHEST
F32 = jnp.float32
BF16 = jnp.bfloat16
NEG_INF = float("-inf")


def _cparams(*sem):
    return pltpu.CompilerParams(dimension_semantics=sem, vmem_limit_bytes=VMEM_LIMIT)


ORDERED = pl.BlockSpec(memory_space=pl.ANY)


def _ordered_kernel(x_ref, after_a, after_b, o_ref):
    del after_a, after_b
    o_ref[...] = x_ref[...]


def _ordered(x, after_a, after_b):
    n, d = x.shape
    spec = pl.BlockSpec((n, d), lambda i: (0, 0))
    return pl.pallas_call(
        _ordered_kernel, grid=(1,), in_specs=[spec, ORDERED, ORDERED], out_specs=spec,
        out_shape=jax.ShapeDtypeStruct((n, d), x.dtype),
        compiler_params=_cparams("arbitrary"), name="ordered_copy",
    )(x, after_a, after_b)


def _rms(x, g):
    return x * lax.rsqrt(jnp.mean(x * x, axis=-1, keepdims=True) + EPS) * g


def _sigmoid(x):
    return 1.0 / (1.0 + jnp.exp(-x))


def _norm_matmul_kernel(x_ref, g_ref, w_ref, o_ref, a_ref):
    @pl.when(pl.program_id(1) == 0)
    def _():
        a_ref[...] = _rms(x_ref[...], g_ref[...]).astype(a_ref.dtype)

    o_ref[...] = jnp.dot(a_ref[...], w_ref[...], preferred_element_type=F32).astype(o_ref.dtype)


def _norm_matmul(x, g, w, out_dtype, name):
    n, d = x.shape
    cols = w.shape[1]
    tm = min(1024, n)
    tn = 1024
    return pl.pallas_call(
        _norm_matmul_kernel,
        grid=(n // tm, cols // tn),
        in_specs=[pl.BlockSpec((tm, d), lambda i, j: (i, 0)),
                  pl.BlockSpec((1, d), lambda i, j: (0, 0)),
                  pl.BlockSpec((d, tn), lambda i, j: (0, j))],
        out_specs=pl.BlockSpec((tm, tn), lambda i, j: (i, j)),
        out_shape=jax.ShapeDtypeStruct((n, cols), out_dtype),
        scratch_shapes=[pltpu.VMEM((tm, d), BF16)],
        compiler_params=_cparams("parallel", "arbitrary"),
        name=name,
    )(x, g, w)


def _norm_matmul_heads_kernel(x_ref, g_ref, w_ref, o_ref, oh_ref):
    a = _rms(x_ref[...], g_ref[...]).astype(BF16)
    z = jnp.dot(a, w_ref[...], preferred_element_type=F32)
    o_ref[...] = z.astype(o_ref.dtype)
    for h in range(H_F):
        oh_ref[:, h, :] = z[:, h * DH_F:(h + 1) * DH_F]


def _norm_matmul_tokens_minor_kernel(x_ref, g_ref, w_ref, o_ref, ot_ref):
    a = _rms(x_ref[...], g_ref[...]).astype(BF16)
    z = jnp.dot(a, w_ref[...], preferred_element_type=F32)
    o_ref[...] = z.astype(o_ref.dtype)
    ot_ref[...] = z.T.reshape(ot_ref.shape)


def _norm_matmul_tokens_minor(x, g, w, seq_len, name):
    n, d = x.shape
    tm = 512
    per_seq = seq_len // tm
    assert per_seq * tm == seq_len and n % seq_len == 0
    return pl.pallas_call(
        _norm_matmul_tokens_minor_kernel,
        grid=(n // tm,),
        in_specs=[pl.BlockSpec((tm, d), lambda i: (i, 0)),
                  pl.BlockSpec((1, d), lambda i: (0, 0)),
                  pl.BlockSpec((d, W_F), lambda i: (0, 0))],
        out_specs=[pl.BlockSpec((tm, W_F), lambda i: (i, 0)),
                   pl.BlockSpec((None, H_F, DH_F, tm), lambda i: (i // per_seq, 0, 0, i % per_seq))],
        out_shape=[jax.ShapeDtypeStruct((n, W_F), BF16),
                   jax.ShapeDtypeStruct((n // seq_len, H_F, DH_F, seq_len), F32)],
        compiler_params=_cparams("parallel"),
        name=name,
    )(x, g, w)


def _norm_matmul_heads(x, g, w, name):
    n, d = x.shape
    tm = min(512, n)
    return pl.pallas_call(
        _norm_matmul_heads_kernel,
        grid=(n // tm,),
        in_specs=[pl.BlockSpec((tm, d), lambda i: (i, 0)),
                  pl.BlockSpec((1, d), lambda i: (0, 0)),
                  pl.BlockSpec((d, W_F), lambda i: (0, 0))],
        out_specs=[pl.BlockSpec((tm, W_F), lambda i: (i, 0)),
                   pl.BlockSpec((tm, H_F, DH_F), lambda i: (i, 0, 0))],
        out_shape=[jax.ShapeDtypeStruct((n, W_F), BF16),
                   jax.ShapeDtypeStruct((n, H_F, DH_F), F32)],
        compiler_params=_cparams("parallel"),
        name=name,
    )(x, g, w)


def _gate_kernel(x_ref, g_ref, w_ref, b_ref, o_ref):
    a = _rms(x_ref[...], g_ref[...])
    z = jnp.dot(a, w_ref[...], precision=HIGHEST, preferred_element_type=F32) + b_ref[...]
    col = lax.broadcasted_iota(jnp.int32, z.shape, 1)
    log_sig = jnp.minimum(z, 0.0) - jnp.log1p(jnp.exp(-jnp.abs(z)))
    o_ref[...] = jnp.where(col < COL_F, z, log_sig)


def _gates(x, g, w, b):
    n, d = x.shape
    tm = min(512, n)
    return pl.pallas_call(
        _gate_kernel,
        grid=(n // tm,),
        in_specs=[pl.BlockSpec((tm, d), lambda i: (i, 0)),
                  pl.BlockSpec((1, d), lambda i: (0, 0)),
                  pl.BlockSpec((d, GATE_COLS), lambda i: (0, 0)),
                  pl.BlockSpec((1, GATE_COLS), lambda i: (0, 0))],
        out_specs=pl.BlockSpec((tm, GATE_COLS), lambda i: (i, 0)),
        out_shape=jax.ShapeDtypeStruct((n, GATE_COLS), F32),
        compiler_params=_cparams("parallel"),
        name="gates",
    )(x, g, w, b)


def _cumsum_kernel(x_ref, o_ref, carry_ref):
    @pl.when(pl.program_id(1) == 0)
    def _():
        carry_ref[...] = jnp.zeros_like(carry_ref)

    x = x_ref[...]
    tb = x.shape[0]
    row = lax.broadcasted_iota(jnp.int32, (tb, tb), 0)
    col = lax.broadcasted_iota(jnp.int32, (tb, tb), 1)
    tril = jnp.where(col <= row, 1.0, 0.0).astype(F32)
    c = jnp.dot(tril, x, precision=HIGHEST, preferred_element_type=F32) + carry_ref[...]
    o_ref[...] = c
    carry_ref[...] = c[tb - 1:tb, :]


def _cumsum_tokens(x):
    b, t, w = x.shape
    tb = 256
    return pl.pallas_call(
        _cumsum_kernel,
        grid=(b, t // tb),
        in_specs=[pl.BlockSpec((None, tb, w), lambda i, j: (i, j, 0))],
        out_specs=pl.BlockSpec((None, tb, w), lambda i, j: (i, j, 0)),
        out_shape=jax.ShapeDtypeStruct((b, t, w), F32),
        scratch_shapes=[pltpu.VMEM((1, w), F32)],
        compiler_params=_cparams("parallel", "arbitrary"),
        name="cumsum",
    )(x)


def _mlstm_kernel(qkv_ref, og_ref, g_ref, c0_ref, n0_ref, m0_ref,
                  y_ref, cn_ref, nn_ref, mn_ref, c_s, n_s, m_s, *, bb_n, blk):
    step = pl.program_id(1)

    @pl.when(step == 0)
    def _():
        c_s[...] = c0_ref[...]
        n_s[...] = n0_ref[...]
        m_s[...] = m0_ref[...]

    row = lax.broadcasted_iota(jnp.int32, (blk, blk), 0)
    col = lax.broadcasted_iota(jnp.int32, (blk, blk), 1)
    tril = col <= row
    triu = row <= col
    eye = col == row

    def to_row(x_col):
        return jnp.sum(jnp.where(eye, x_col, 0.0), axis=0, keepdims=True)

    for bb in range(bb_n):
        g = g_ref[bb]
        for h in range(H_M):
            q = qkv_ref[bb, :, h * DH_M:(h + 1) * DH_M]
            k = qkv_ref[bb, :, (H_M + h) * DH_M:(H_M + h + 1) * DH_M] * (DH_M ** -0.5)
            v = qkv_ref[bb, :, (2 * H_M + h) * DH_M:(2 * H_M + h + 1) * DH_M]
            i_col = g[:, COL_I + h:COL_I + h + 1]
            f_col = g[:, COL_F + h:COL_F + h + 1]
            i_row = to_row(i_col)
            f_row = to_row(f_col)
            b_col = jnp.sum(jnp.where(tril, f_row, 0.0), axis=1, keepdims=True)
            b_row = jnp.sum(jnp.where(triu, f_col, 0.0), axis=0, keepdims=True)
            m_prev = m_s[bb, h][:, 0:1]
            dmat = jnp.where(tril, b_col - b_row + i_row, NEG_INF)
            g_col = b_col + m_prev
            mt = jnp.maximum(g_col, jnp.max(dmat, axis=1, keepdims=True))
            w_d = jnp.exp(dmat - mt)
            w_g = jnp.exp(g_col - mt)
            qk = lax.dot_general(q, k, (((1,), (1,)), ((), ())), preferred_element_type=F32) * w_d
            c_prev = c_s[bb, h]
            n_prev = n_s[bb, h]
            num = (w_g * jnp.dot(q, c_prev.astype(BF16), preferred_element_type=F32)
                   + jnp.dot(qk.astype(BF16), v, preferred_element_type=F32))
            den = (w_g * jnp.sum(q.astype(F32) * n_prev, axis=1, keepdims=True)
                   + jnp.sum(qk, axis=1, keepdims=True))
            hid = num / jnp.maximum(jnp.abs(den), jnp.exp(-mt))
            o_gate = _sigmoid(og_ref[bb, :, h * DH_M:(h + 1) * DH_M])
            y_ref[bb, :, h * DH_M:(h + 1) * DH_M] = (o_gate * hid).astype(y_ref.dtype)
            m_new = mt[blk - 1:blk, :]
            b_last = b_col[blk - 1:blk, :]
            w_c = jnp.exp(b_last + m_prev - m_new)
            w_s = jnp.exp(b_last - b_col + i_col - m_new)
            kw = k.astype(F32) * w_s
            c_s[bb, h] = w_c * c_prev + lax.dot_general(
                kw.astype(BF16), v, (((0,), (0,)), ((), ())), preferred_element_type=F32)
            n_s[bb, h] = w_c * n_prev + jnp.sum(kw, axis=0, keepdims=True)
            m_s[bb, h] = jnp.broadcast_to(m_new, (1, LANES))

    @pl.when(step == pl.num_programs(1) - 1)
    def _():
        cn_ref[...] = c_s[...]
        nn_ref[...] = n_s[...]
        mn_ref[...] = m_s[...]


def _mlstm(qkv, ogate, gates, c0, n0, m0):
    b, t, _ = qkv.shape
    blk = min(CHUNK, t)
    bb_n = 4
    state_spec = lambda shape: pl.BlockSpec((bb_n,) + shape, lambda i, j: (i, 0, 0, 0))
    return pl.pallas_call(
        functools.partial(_mlstm_kernel, bb_n=bb_n, blk=blk),
        grid=(b // bb_n, t // blk),
        in_specs=[pl.BlockSpec((bb_n, blk, 3 * W_M), lambda i, j: (i, j, 0)),
                  pl.BlockSpec((bb_n, blk, W_M), lambda i, j: (i, j, 0)),
                  pl.BlockSpec((bb_n, blk, GATE_COLS), lambda i, j: (i, j, 0)),
                  state_spec((H_M, DH_M, DH_M)),
                  state_spec((H_M, 1, DH_M)),
                  state_spec((H_M, 1, LANES))],
        out_specs=[pl.BlockSpec((bb_n, blk, W_M), lambda i, j: (i, j, 0)),
                   state_spec((H_M, DH_M, DH_M)),
                   state_spec((H_M, 1, DH_M)),
                   state_spec((H_M, 1, LANES))],
        out_shape=[jax.ShapeDtypeStruct((b, t, W_M), BF16),
                   jax.ShapeDtypeStruct((b, H_M, DH_M, DH_M), F32),
                   jax.ShapeDtypeStruct((b, H_M, 1, DH_M), F32),
                   jax.ShapeDtypeStruct((b, H_M, 1, LANES), F32)],
        scratch_shapes=[pltpu.VMEM((bb_n, H_M, DH_M, DH_M), F32),
                        pltpu.VMEM((bb_n, H_M, 1, DH_M), F32),
                        pltpu.VMEM((bb_n, H_M, 1, LANES), F32)],
        compiler_params=_cparams("parallel", "arbitrary"),
        name="mlstm",
    )(qkv, ogate, gates, c0, n0, m0)


def _online_softmax_step(s, pv, m_ref, l_ref, acc_ref):
    m_old = m_ref[...]
    m_new = jnp.maximum(m_old, jnp.max(s, axis=1, keepdims=True))
    alpha = jnp.exp(m_old - m_new)
    p = jnp.exp(s - m_new)
    l_ref[...] = alpha * l_ref[...] + jnp.sum(p, axis=1, keepdims=True)
    acc_ref[...] = alpha * acc_ref[...] + pv(p.astype(BF16))
    m_ref[...] = m_new


def _split3(x):
    hi = x.astype(BF16).astype(F32)
    mid = (x - hi).astype(BF16).astype(F32)
    lo = (x - hi - mid).astype(BF16).astype(F32)
    return hi, mid, lo


FOX_V_ROWS = DH_F + 16
AUG_CQ = 6


def _fox_prompt_kernel(q_ref, k_ref, v_ref, c_ref, after_ref, o_ref, kaug_s, vt_s, m0_s, m1_s, acc0_s,
                       acc1_s, *, tq, q_first, n_blocks):
    del after_ref
    hp = pl.program_id(1)
    qi = pl.program_id(2) + q_first
    lane = lax.broadcasted_iota(jnp.int32, (tq, LANES), 1)
    m_s, acc_s = (m0_s, m1_s), (acc0_s, acc1_s)

    def head_col(c_blk, x):
        return jnp.sum(jnp.where(lane == COL_FF + 2 * hp + x, c_blk, 0.0), axis=1, keepdims=True)

    def aug_tile(entries):
        tile = jnp.zeros((tq, LANES), F32)
        for l, val in entries:
            tile = jnp.where(lane == l, val, tile)
        return tile.astype(BF16)

    @pl.when(pl.program_id(2) == 0)
    def _():
        ones_rows = jnp.where(lax.broadcasted_iota(jnp.int32, (FOX_V_ROWS - DH_F, tq), 0) == 0, 1.0, 0.0)

        def chunk(i, carry):
            rs = pl.multiple_of(i * tq, tq)
            c_blk = c_ref[pl.ds(rs, tq), :]
            pieces = _split3(head_col(c_blk, 0)) + _split3(head_col(c_blk, 1))
            entries = list(enumerate(pieces)) + [(AUG_CQ + j, 1.0) for j in range(3)]
            kaug_s[pl.ds(rs, tq), 0:LANES] = k_ref[pl.ds(rs, tq), :]
            kaug_s[pl.ds(rs, tq), LANES:2 * LANES] = aug_tile(entries)
            vt = v_ref[pl.ds(rs, tq), :].astype(F32).T
            for x in range(2):
                vt_s[x, 0:DH_F, pl.ds(rs, tq)] = vt[x * DH_F:(x + 1) * DH_F, :].astype(BF16)
                vt_s[x, DH_F:FOX_V_ROWS, pl.ds(rs, tq)] = ones_rows.astype(BF16)
            return carry

        lax.fori_loop(0, n_blocks, chunk, 0)

    qs = pl.multiple_of(qi * tq, tq)
    c_q = c_ref[pl.ds(qs, tq), :]
    q2 = q_ref[...] * (DH_F ** -0.5)
    q_aug = []
    for x in range(2):
        cq3 = _split3(head_col(c_q, x))
        entries = [(3 * x + j, -1.0) for j in range(3)] + [(AUG_CQ + j, cq3[j]) for j in range(3)]
        q_head = jnp.where((lane < DH_F) == (x == 0), q2, jnp.zeros_like(q2))
        q_aug.append(jnp.concatenate([q_head, aug_tile(entries)], axis=1))
        m_s[x][...] = jnp.full(m_s[x].shape, NEG_INF, F32)
        acc_s[x][...] = jnp.zeros(acc_s[x].shape, F32)

    def block(kb, masked):
        ks = pl.multiple_of(kb * tq, tq)
        k_blk = kaug_s[pl.ds(ks, tq), :]
        scores = [lax.dot_general(k_blk, q_aug[x], (((1,), (1,)), ((), ())), preferred_element_type=F32)
                  for x in range(2)]
        for x in range(2):
            st = scores[x]
            if masked:
                k_pos = lax.broadcasted_iota(jnp.int32, (tq, tq), 0)
                q_pos = lax.broadcasted_iota(jnp.int32, (tq, tq), 1)
                st = jnp.where(k_pos <= q_pos, st, NEG_INF)
            m_old = m_s[x][...]
            m_new = jnp.maximum(m_old, jnp.max(st, axis=0, keepdims=True))
            p = jnp.exp(st - m_new).astype(BF16)
            acc_s[x][...] = (jnp.exp(m_old - m_new) * acc_s[x][...]
                             + jnp.dot(vt_s[x, :, pl.ds(ks, tq)], p, preferred_element_type=F32))
            m_s[x][...] = m_new

    def body(kb, carry):
        block(kb, False)
        return carry

    lax.fori_loop(0, qi, body, 0)
    block(qi, True)
    out_t = jnp.concatenate([acc_s[x][0:DH_F, :] / acc_s[x][DH_F:DH_F + 1, :] for x in range(2)], axis=0)
    o_ref[...] = out_t.T.astype(o_ref.dtype)


def _fox_prompt(q, k, v, c, batch, first_token, tokens, after):
    _, t, _ = q.shape
    tq = 512
    q_first, n_q = first_token // tq, tokens // tq
    head_pair = lambda i, h, j: (batch, 0, h)
    return pl.pallas_call(
        functools.partial(_fox_prompt_kernel, tq=tq, q_first=q_first, n_blocks=q_first + n_q),
        grid=(1, H_F // 2, n_q),
        in_specs=[pl.BlockSpec((None, tq, LANES), lambda i, h, j: (batch, q_first + j, h)),
                  pl.BlockSpec((None, t, LANES), head_pair),
                  pl.BlockSpec((None, t, LANES), head_pair),
                  pl.BlockSpec((None, t, LANES), lambda i, h, j: (batch, 0, 0)),
                  ORDERED],
        out_specs=pl.BlockSpec((tq, LANES), lambda i, h, j: (j, h)),
        out_shape=jax.ShapeDtypeStruct((tokens, W_F), BF16),
        scratch_shapes=[pltpu.VMEM((t, 2 * LANES), BF16),
                        pltpu.VMEM((2, FOX_V_ROWS, t), BF16),
                        pltpu.VMEM((1, tq), F32), pltpu.VMEM((1, tq), F32),
                        pltpu.VMEM((FOX_V_ROWS, tq), F32), pltpu.VMEM((FOX_V_ROWS, tq), F32)],
        compiler_params=_cparams("parallel", "parallel", "arbitrary"),
        name="fox_prompt",
    )(q, k, v, c, after)


def _fox_sample_kernel(q_ref, kc_ref, vc_ref, kn_ref, vn_ref, cq_ref, ctc_ref, ctn_ref, after_ref, o_ref,
                       m_s, l_s, acc_s, *, tn):
    del after_ref
    kb = pl.program_id(1)

    @pl.when(kb == 0)
    def _():
        m_s[...] = jnp.full_like(m_s, NEG_INF)
        l_s[...] = jnp.zeros_like(l_s)
        acc_s[...] = jnp.zeros_like(acc_s)

    lane = lax.broadcasted_iota(jnp.int32, (tn, LANES), 1)
    c_blk = cq_ref[...]
    nt = (((1,), (1,)), ((), ()))

    def head_query(hd):
        q = q_ref[:, hd * DH_F:(hd + 1) * DH_F] * (DH_F ** -0.5)
        cq = jnp.sum(jnp.where(lane == COL_FF + hd, c_blk, 0.0), axis=1, keepdims=True)
        return q, cq

    for hd in range(H_F):
        q, cq = head_query(hd)
        k_t = kc_ref[hd].astype(BF16)
        v_t = vc_ref[hd].astype(BF16)
        s = jnp.dot(q, k_t, preferred_element_type=F32) + cq - ctc_ref[hd // 2, hd % 2:hd % 2 + 1, :]
        _online_softmax_step(s, lambda p: lax.dot_general(p, v_t, nt, preferred_element_type=F32),
                             m_s.at[hd], l_s.at[hd], acc_s.at[hd])

    @pl.when(kb == pl.num_programs(1) - 1)
    def _():
        q_pos = lax.broadcasted_iota(jnp.int32, (tn, tn), 0)
        k_pos = lax.broadcasted_iota(jnp.int32, (tn, tn), 1)
        for hd in range(H_F):
            q, cq = head_query(hd)
            k_new = kn_ref[:, hd * DH_F:(hd + 1) * DH_F]
            v_new = vn_ref[:, hd * DH_F:(hd + 1) * DH_F]
            s = (lax.dot_general(q, k_new, nt, preferred_element_type=F32)
                 + cq - ctn_ref[hd // 2, hd % 2:hd % 2 + 1, :])
            s = jnp.where(k_pos <= q_pos, s, NEG_INF)
            _online_softmax_step(s, lambda p: jnp.dot(p, v_new, preferred_element_type=F32),
                                 m_s.at[hd], l_s.at[hd], acc_s.at[hd])
            o_ref[:, hd * DH_F:(hd + 1) * DH_F] = (acc_s[hd] / l_s[hd]).astype(o_ref.dtype)


def _fox_sample(q, k_cache_t, v_cache_t, k_new, v_new, cq, ct_cache, ct_new, after):
    b, tn, _ = q.shape
    p = k_cache_t.shape[3]
    tk = 1024
    new = pl.BlockSpec((None, tn, W_F), lambda i, j: (i, 0, 0))
    cache = pl.BlockSpec((None, H_F, DH_F, tk), lambda i, j: (i, 0, 0, j))
    return pl.pallas_call(
        functools.partial(_fox_sample_kernel, tn=tn),
        grid=(b, p // tk),
        in_specs=[new, cache, cache, new, new,
                  pl.BlockSpec((None, tn, LANES), lambda i, j: (i, 0, 0)),
                  pl.BlockSpec((None, H_F // 2, 2, tk), lambda i, j: (i, 0, 0, j)),
                  pl.BlockSpec((None, H_F // 2, 2, tn), lambda i, j: (i, 0, 0, 0)),
                  ORDERED],
        out_specs=new,
        out_shape=jax.ShapeDtypeStruct((b, tn, W_F), BF16),
        scratch_shapes=[pltpu.VMEM((H_F, tn, 1), F32),
                        pltpu.VMEM((H_F, tn, 1), F32),
                        pltpu.VMEM((H_F, tn, DH_F), F32)],
        compiler_params=_cparams("parallel", "arbitrary"),
        name="fox_sample",
    )(q, k_cache_t, v_cache_t, k_new, v_new, cq, ct_cache, ct_new, after)


def _merge_kernel(h_ref, ym_ref, yf_ref, gm_ref, gf_ref, wm_ref, wf_ref, wo_ref, after_ref, o_ref):
    del after_ref
    a = jnp.dot(ym_ref[...], wm_ref[...], preferred_element_type=F32)
    b = jnp.dot(yf_ref[...], wf_ref[...], preferred_element_type=F32)
    merge = _sigmoid(gm_ref[...]) * a + _sigmoid(gf_ref[...]) * b
    o_ref[...] = h_ref[...] + jnp.dot(merge.astype(BF16), wo_ref[...], preferred_element_type=F32)


def _merge(h, ym, yf, gates, wm, wf, wo, first_token, after):
    n, d = yf.shape
    tm = min(512, n)
    first = first_token // tm
    tok = lambda c: pl.BlockSpec((tm, d), lambda i: (first + i, c))
    local = pl.BlockSpec((tm, d), lambda i: (i, 0))
    wspec = pl.BlockSpec((d, d), lambda i: (0, 0))
    return pl.pallas_call(
        _merge_kernel,
        grid=(n // tm,),
        in_specs=[tok(0), tok(0), local, tok(1), tok(2), wspec, wspec, wspec, ORDERED],
        out_specs=local,
        out_shape=jax.ShapeDtypeStruct((n, d), F32),
        compiler_params=_cparams("parallel"),
        name="merge",
    )(h, ym, yf, gates, gates, wm, wf, wo, after)


def _top_rows(s, count, ids=None, payload=None):
    if ids is None:
        ids = lax.broadcasted_iota(jnp.int32, s.shape, 0)
    big = jnp.int32(2 ** 30)
    vals, sel, pay = [], [], []
    for _ in range(count):
        m = jnp.max(s, axis=0, keepdims=True)
        am = jnp.min(jnp.where(s == m, ids, big), axis=0, keepdims=True)
        hit = ids == am
        vals.append(m)
        sel.append(am)
        if payload is not None:
            pay.append(jnp.max(jnp.where(hit, payload, -1), axis=0, keepdims=True))
        s = jnp.where(hit, NEG_INF, s)
    cat = lambda xs: jnp.concatenate(xs, axis=0)
    return cat(vals), cat(sel), (cat(pay) if payload is not None else None)


def _pair_candidates(v1, i1, v2, i2):
    t = v1.shape[1]
    half = PEER_TOPK // 2
    r16 = lax.broadcasted_iota(jnp.int32, (PEER_TOPK, t), 0)
    r8 = lax.broadcasted_iota(jnp.int32, (half, t), 0)
    sums = [v1[0:1, :] + v2]
    flat = [r16]
    expert = [i1[0:1, :] * PEER_KEYS + i2]
    for a in range(1, half):
        sums.append(v1[a:a + 1, :] + v2[0:half, :])
        flat.append(r8 + a * PEER_TOPK)
        expert.append(i1[a:a + 1, :] * PEER_KEYS + i2[0:half, :])
    sums.append(v1[half:, :] + v2[0:1, :])
    flat.append((r8 + half) * PEER_TOPK)
    expert.append(i1[half:, :] * PEER_KEYS + i2[0:1, :])
    cat = lambda xs: jnp.concatenate(xs, axis=0)
    return cat(sums), cat(flat), cat(expert)


def _peer_score_kernel(h_ref, g_ref, wqt_ref, k1_ref, k2_ref, xn_ref, idx_ref, gate_ref):
    xn = _rms(h_ref[...], g_ref[...])
    xn_ref[...] = xn.astype(xn_ref.dtype)
    qt = lax.dot_general(wqt_ref[...], xn, (((1,), (1,)), ((), ())),
                         precision=HIGHEST, preferred_element_type=F32)
    half = PEER_DQ // 2
    idx_rows, gate_rows = [], []
    for hd in range(PEER_HEADS):
        q1 = qt[hd * PEER_DQ:hd * PEER_DQ + half, :]
        q2 = qt[hd * PEER_DQ + half:(hd + 1) * PEER_DQ, :]
        s1 = jnp.dot(k1_ref[hd], q1, precision=HIGHEST, preferred_element_type=F32)
        s2 = jnp.dot(k2_ref[hd], q2, precision=HIGHEST, preferred_element_type=F32)
        v1, i1, _ = _top_rows(s1, PEER_TOPK)
        v2, i2, _ = _top_rows(s2, PEER_TOPK)
        cand, flat, expert = _pair_candidates(v1, i1, v2, i2)
        sc, _, ex = _top_rows(cand, PEER_TOPK, ids=flat, payload=expert)
        e = jnp.exp(sc - sc[0:1, :])
        gate_rows.append(e / jnp.sum(e, axis=0, keepdims=True))
        idx_rows.append(ex)
    gate_ref[...] = jnp.concatenate(gate_rows, axis=0).T
    idx_ref[...] = jnp.concatenate(idx_rows, axis=0).T


def _peer_score(h, g, wqt, k1, k2):
    n, d = h.shape
    tt = min(256, n)
    return pl.pallas_call(
        _peer_score_kernel,
        grid=(n // tt,),
        in_specs=[pl.BlockSpec((tt, d), lambda i: (i, 0)),
                  pl.BlockSpec((1, d), lambda i: (0, 0)),
                  pl.BlockSpec(wqt.shape, lambda i: (0, 0)),
                  pl.BlockSpec(k1.shape, lambda i: (0, 0, 0)),
                  pl.BlockSpec(k2.shape, lambda i: (0, 0, 0))],
        out_specs=[pl.BlockSpec((tt, d), lambda i: (i, 0)),
                   pl.BlockSpec((tt, PEER_SEL), lambda i: (i, 0)),
                   pl.BlockSpec((tt, PEER_SEL), lambda i: (i, 0))],
        out_shape=[jax.ShapeDtypeStruct((n, d), F32),
                   jax.ShapeDtypeStruct((n, PEER_SEL), jnp.int32),
                   jax.ShapeDtypeStruct((n, PEER_SEL), F32)],
        compiler_params=_cparams("parallel"),
        name="peer_score",
    )(h, g, wqt, k1, k2)


SC_CORES = 2
SC_SUBCORES = 16
SC_WORKERS = SC_CORES * SC_SUBCORES
SC_ROWS = 64


def _pack_table(tab):
    half = tab.shape[1] // 2
    bits = lax.bitcast_convert_type(tab.astype(BF16), jnp.uint16).astype(jnp.uint32)
    word = bits[:, :half] | (bits[:, half:] << 16)
    return lax.bitcast_convert_type(word, jnp.int32)


SC_TOK = 8
SC_LANES = 16
STEPS_PER_TOKEN = PEER_SEL // SC_ROWS
SC_PARAMS = pltpu.CompilerParams(needs_layout_passes=False)


def _row_source(table_hbm, idx_v, local_step, global_step):
    del global_step
    return table_hbm.at[idx_v.at[local_step]]


def _sc_unpack(wd):
    lo = lax.bitcast_convert_type(wd << 16, F32)
    hi = lax.bitcast_convert_type(wd & jnp.int32(-65536), F32)
    return lo, hi


def _sc_token_blocks(n, body_block):
    per_worker = n // SC_WORKERS
    assert per_worker * SC_WORKERS == n and per_worker % SC_TOK == 0
    wid = lax.axis_index("s") * SC_CORES + lax.axis_index("c")

    @pl.loop(0, per_worker // SC_TOK)
    def _(blk):
        body_block(wid * per_worker + blk * SC_TOK)


def _sc_pipelined_steps(table_hbm, idx_v, rows_v, sems, first_step, consume):
    n_steps = SC_TOK * STEPS_PER_TOKEN

    def row_gather(j, slot):
        return pltpu.make_async_copy(_row_source(table_hbm, idx_v, j, first_step + j),
                                     rows_v.at[slot], sems.at[slot])

    row_gather(0, 0).start()

    @pl.loop(0, n_steps, step=2)
    def _(i):
        for slot in range(2):
            j = i + slot

            @pl.when(j + 1 < n_steps)
            def _():
                row_gather(j + 1, 1 - slot).start()

            row_gather(j, slot).wait()
            consume(slot, i // 2, slot)


def _sc_expert_dot(table, idx_steps, x):
    n, d = x.shape
    w = d // 2
    assert STEPS_PER_TOKEN == 2 and table.shape[1] == w
    mesh = plsc.VectorSubcoreMesh(core_axis_name="c", subcore_axis_name="s")
    n_steps = SC_TOK * STEPS_PER_TOKEN
    group = 4

    @functools.partial(
        pl.kernel, mesh=mesh,
        out_type=jax.ShapeDtypeStruct((n, PEER_SEL), F32),
        scratch_types=[pltpu.VMEM((n_steps, SC_ROWS), jnp.int32),
                       pltpu.VMEM((SC_TOK, d), F32),
                       pltpu.VMEM((2, SC_ROWS, w), jnp.int32),
                       pltpu.VMEM((SC_TOK, PEER_SEL), F32),
                       pltpu.SemaphoreType.DMA((2,))],
        compiler_params=SC_PARAMS,
    )
    def expert_dot(table_hbm, idx_hbm, x_hbm, act_hbm, idx_v, x_v, rows_v, act_v, sems):
        lanes = lax.iota(jnp.int32, SC_LANES)
        zero = jnp.zeros((SC_LANES,), F32)

        def consume(slot, tl, half):
            @pl.loop(0, SC_ROWS // SC_LANES)
            def _(g):
                act_vec = zero
                for q in range(SC_LANES // group):
                    r0 = g * SC_LANES + q * group

                    def chunk(j, accs):
                        c = pl.multiple_of(j * SC_LANES, SC_LANES)
                        x_lo = x_v[tl, pl.ds(c, SC_LANES)]
                        x_hi = x_v[tl, pl.ds(w + c, SC_LANES)]
                        out = []
                        for rr in range(group):
                            lo, hi = _sc_unpack(rows_v[slot, r0 + rr, pl.ds(c, SC_LANES)])
                            out.append(accs[rr] + lo * x_lo + hi * x_hi)
                        return tuple(out)

                    accs = lax.fori_loop(0, w // SC_LANES, chunk, (zero,) * group, unroll=2)
                    for rr in range(group):
                        act_vec = jnp.where(lanes == q * group + rr, jnp.sum(accs[rr]), act_vec)
                act_v[tl, pl.ds(half * SC_ROWS + g * SC_LANES, SC_LANES)] = act_vec

        def block(t0):
            s0 = t0 * STEPS_PER_TOKEN
            pltpu.sync_copy(idx_hbm.at[pl.ds(s0, n_steps)], idx_v)
            pltpu.sync_copy(x_hbm.at[pl.ds(t0, SC_TOK)], x_v)
            _sc_pipelined_steps(table_hbm, idx_v, rows_v, sems, s0, consume)
            pltpu.sync_copy(act_v, act_hbm.at[pl.ds(t0, SC_TOK)])

        _sc_token_blocks(n, block)

    return expert_dot(table, idx_steps, x)


def _sc_expert_sum(table, idx_steps, wgt):
    n = wgt.shape[0]
    w = table.shape[1]
    d = 2 * w
    assert STEPS_PER_TOKEN == 2
    mesh = plsc.VectorSubcoreMesh(core_axis_name="c", subcore_axis_name="s")
    n_steps = SC_TOK * STEPS_PER_TOKEN
    cols = 8

    @functools.partial(
        pl.kernel, mesh=mesh,
        out_type=jax.ShapeDtypeStruct((n, d), F32),
        scratch_types=[pltpu.VMEM((n_steps, SC_ROWS), jnp.int32),
                       pltpu.VMEM((SC_TOK, PEER_SEL), F32),
                       pltpu.VMEM((2, SC_ROWS, w), jnp.int32),
                       pltpu.VMEM((SC_TOK, d), F32),
                       pltpu.VMEM((SC_ROWS, SC_LANES), F32),
                       pltpu.SemaphoreType.DMA((2,))],
        compiler_params=SC_PARAMS,
    )
    def expert_sum(table_hbm, idx_hbm, wgt_hbm, out_hbm, idx_v, wgt_v, rows_v, out_v, splat_v, sems):
        zero = jnp.zeros((SC_LANES,), F32)
        lanes = lax.iota(jnp.int32, SC_LANES)

        def consume(slot, tl, half):
            @pl.loop(0, SC_ROWS // SC_LANES)
            def _(g):
                w16 = wgt_v[tl, pl.ds(half * SC_ROWS + g * SC_LANES, SC_LANES)]
                for rr in range(SC_LANES):
                    one = jnp.sum(jnp.where(lanes == rr, w16, 0.0))
                    splat_v[g * SC_LANES + rr, :] = jnp.full((SC_LANES,), one, F32)

            for cb in range(w // (cols * SC_LANES)):
                base = cb * cols * SC_LANES

                def row(r, accs):
                    wv = splat_v[r, :]
                    out = []
                    for jj in range(cols):
                        lo, hi = _sc_unpack(rows_v[slot, r, pl.ds(base + jj * SC_LANES, SC_LANES)])
                        out.append(accs[2 * jj] + wv * lo)
                        out.append(accs[2 * jj + 1] + wv * hi)
                    return tuple(out)

                accs = lax.fori_loop(0, SC_ROWS, row, (zero,) * (2 * cols), unroll=2)
                for jj in range(cols):
                    c = base + jj * SC_LANES
                    if half == 0:
                        out_v[tl, pl.ds(c, SC_LANES)] = accs[2 * jj]
                        out_v[tl, pl.ds(w + c, SC_LANES)] = accs[2 * jj + 1]
                    else:
                        out_v[tl, pl.ds(c, SC_LANES)] = out_v[tl, pl.ds(c, SC_LANES)] + accs[2 * jj]
                        out_v[tl, pl.ds(w + c, SC_LANES)] = out_v[tl, pl.ds(w + c, SC_LANES)] + accs[2 * jj + 1]

        def block(t0):
            s0 = t0 * STEPS_PER_TOKEN
            pltpu.sync_copy(idx_hbm.at[pl.ds(s0, n_steps)], idx_v)
            pltpu.sync_copy(wgt_hbm.at[pl.ds(t0, SC_TOK)], wgt_v)
            _sc_pipelined_steps(table_hbm, idx_v, rows_v, sems, s0, consume)
            pltpu.sync_copy(out_v, out_hbm.at[pl.ds(t0, SC_TOK)])

        _sc_token_blocks(n, block)

    return expert_sum(table, idx_steps, wgt)


def _gelu_exact(x):
    return 0.5 * x * (1.0 + lax.erf(x * (2.0 ** -0.5)))


def _peer_weight_kernel(act_ref, gate_ref, after_ref, o_ref):
    del after_ref
    o_ref[...] = gate_ref[...] * _gelu_exact(act_ref[...])


def _peer_weight(act, gate, after):
    n = act.shape[0]
    tt = min(2048, n)
    spec = pl.BlockSpec((tt, PEER_SEL), lambda i: (i, 0))
    return pl.pallas_call(
        _peer_weight_kernel,
        grid=(n // tt,),
        in_specs=[spec, spec, ORDERED],
        out_specs=spec,
        out_shape=jax.ShapeDtypeStruct((n, PEER_SEL), F32),
        compiler_params=_cparams("parallel"),
        name="peer_weight",
    )(act, gate, after)


def _ple_kernel(h_ref, peer_ref, p_ref, gp_ref, gfin_ref, wg_ref, wp_ref, o_ref):
    h = h_ref[...] + peer_ref[...]
    e = _rms(h, gp_ref[...]).astype(BF16)
    gate = _sigmoid(jnp.dot(e, wg_ref[...], preferred_element_type=F32))
    proj = jnp.dot(p_ref[...].astype(BF16), wp_ref[...], preferred_element_type=F32)
    o_ref[...] = _rms(h + gate * proj, gfin_ref[...])


def _ple_final(h, peer, p, g_ple, g_final, wg, wp, first_token):
    n, d = h.shape
    tm = min(512, n)
    first = first_token // tm
    return pl.pallas_call(
        _ple_kernel,
        grid=(n // tm,),
        in_specs=[pl.BlockSpec((tm, d), lambda i: (i, 0)),
                  pl.BlockSpec((tm, d), lambda i: (i, 0)),
                  pl.BlockSpec((tm, D_PLE), lambda i: (first + i, 0)),
                  pl.BlockSpec((1, d), lambda i: (0, 0)),
                  pl.BlockSpec((1, d), lambda i: (0, 0)),
                  pl.BlockSpec((d, d), lambda i: (0, 0)),
                  pl.BlockSpec((D_PLE, d), lambda i: (0, 0))],
        out_specs=pl.BlockSpec((tm, d), lambda i: (i, 0)),
        out_shape=jax.ShapeDtypeStruct((n, d), F32),
        compiler_params=_cparams("parallel"),
        name="ple_final",
    )(h, peer, p, g_ple, g_final, wg, wp)


def _prep_weights(norm_mix, w_in, mlstm_b_i, mlstm_b_f, fox_b_f, w_br_m, w_br_f, w_out, norm_ffn,
                  peer_w_q, peer_keys1, peer_keys2, peer_u, peer_v, norm_ple, w_ple_gate, w_ple_proj,
                  norm_final):
    o = [0]
    for s in (W_M, W_M, W_M, W_M, H_M, H_M, W_F, W_F, W_F, H_F, D_MODEL, D_MODEL):
        o.append(o[-1] + s)
    seg = lambda a, b: w_in[:, o[a]:o[b]]
    w_gate = jnp.concatenate([seg(4, 6), seg(9, 10)], axis=1)
    w_gate = jnp.pad(w_gate, ((0, 0), (0, GATE_COLS - w_gate.shape[1])))
    b_gate = jnp.concatenate([mlstm_b_i, mlstm_b_f, fox_b_f]).astype(F32)
    b_gate = jnp.pad(b_gate, (0, GATE_COLS - b_gate.shape[0]))[None, :]
    row = lambda v: v.astype(F32)[None, :]
    return dict(
        norm_mix=row(norm_mix),
        w_mqkv=seg(0, 3).astype(BF16),
        w_og=jnp.concatenate([seg(3, 4), seg(10, 12)], axis=1).astype(BF16),
        w_fq=seg(6, 7).astype(BF16), w_fk=seg(7, 8).astype(BF16), w_fv=seg(8, 9).astype(BF16),
        w_gate=w_gate.astype(F32), b_gate=b_gate,
        w_br_m=w_br_m.astype(BF16), w_br_f=w_br_f.astype(BF16), w_out=w_out.astype(BF16),
        norm_ffn=row(norm_ffn), wqt=peer_w_q.T.astype(F32),
        k1=peer_keys1.astype(F32), k2=peer_keys2.astype(F32),
        u_pk=_pack_table(peer_u), v_pk=_pack_table(peer_v),
        norm_ple=row(norm_ple), w_ple_gate=w_ple_gate.astype(BF16), w_ple_proj=w_ple_proj.astype(BF16),
        norm_final=row(norm_final),
    )


def _ct_rows(c):
    b, t, _ = c.shape
    return jnp.swapaxes(c[:, :, COL_FF:COL_FF + H_F], 1, 2).reshape(b, H_F // 2, 2, t)


def _peer_first_pass(h, w):
    n = h.shape[0]
    xn, idx, gate = _peer_score(h, w["norm_ffn"], w["wqt"], w["k1"], w["k2"])
    steps = idx.reshape(n * STEPS_PER_TOKEN, SC_ROWS)
    return steps, gate, _sc_expert_dot(w["u_pk"], steps, xn)


def _peer_second_pass(first, w, after):
    steps, gate, act = first
    wgt = _peer_weight(act, gate, after)
    return _sc_expert_sum(w["v_pk"], steps, wgt), wgt


def _layer(x, p, w, mstate, fox_cache, after):
    b, t, d = x.shape
    n = b * t
    h = x.reshape(n, d)
    g = w["norm_mix"]
    qkv = _norm_matmul(h, g, w["w_mqkv"], BF16, "proj_mlstm_qkv")
    og = _norm_matmul(h, g, w["w_og"], F32, "proj_gates")
    fq = _norm_matmul(h, g, w["w_fq"], BF16, "proj_fox_q")
    if fox_cache is None:
        fk, fk_t = _norm_matmul_tokens_minor(h, g, w["w_fk"], t, "proj_fox_k")
        fv, fv_t = _norm_matmul_tokens_minor(h, g, w["w_fv"], t, "proj_fox_v")
        fk_state, fv_state = (jnp.transpose(a, (0, 3, 1, 2))[None] for a in (fk_t, fv_t))
    else:
        fk, fk_heads = _norm_matmul_heads(h, g, w["w_fk"], "proj_fox_k")
        fv, fv_heads = _norm_matmul_heads(h, g, w["w_fv"], "proj_fox_v")
        fk_state, fv_state = (a.reshape(1, b, t, H_F, DH_F) for a in (fk_heads, fv_heads))
    gates = _gates(h, g, w["w_gate"], w["b_gate"])

    c0, n0, m0 = mstate
    ym, c_new, n_new, m_new = _mlstm(
        qkv.reshape(b, t, 3 * W_M), og.reshape(b, t, 3 * D_MODEL), gates.reshape(b, t, GATE_COLS),
        c0.astype(F32), n0.astype(F32)[:, :, None, :],
        jnp.broadcast_to(m0.astype(F32)[:, :, None, None], (b, H_M, 1, LANES)))

    def merged(yf, first_token, after):
        return _merge(h, ym.reshape(n, W_M), yf, og, w["w_br_m"], w["w_br_f"], w["w_out"], first_token, after)

    def output(hm, peer, first_token):
        return _ple_final(hm, peer, p.reshape(n, D_PLE), w["norm_ple"], w["norm_final"],
                          w["w_ple_gate"], w["w_ple_proj"], first_token)

    neutral = w["norm_ffn"]
    gates3 = gates.reshape(b, t, GATE_COLS)
    if fox_cache is None:
        c = _cumsum_tokens(gates3)
        seqs = (fq.reshape(b, t, W_F), fk.reshape(b, t, W_F), fv.reshape(b, t, W_F))
        segs = [(0, i * (t // 4), t // 4) for i in range(4)] + [(bi, 0, t) for bi in range(1, b)]
        hms, firsts, peers, wgts = [], [], [], []
        for s, (bi, start, count) in enumerate(segs):
            yf = _fox_prompt(*seqs, c, bi, start, count, peers[s - 2] if s >= 2 else neutral)
            if s >= 1:
                peer, wgt = _peer_second_pass(firsts[s - 1], w, yf)
                peers.append(peer)
                wgts.append(wgt)
            hms.append(merged(yf, bi * t + start, wgts[s - 1] if s >= 1 else neutral))
            firsts.append(_peer_first_pass(hms[s], w))
        peer, wgt = _peer_second_pass(firsts[-1], w, neutral)
        peers.append(peer)
        wgts.append(wgt)
        y = jnp.concatenate([output(hms[s], peers[s], bi * t + start)
                             for s, (bi, start, _) in enumerate(segs)], axis=0)
        tail = (peers[-2], firsts[-1][1], wgts[-1], peers[-1])
    else:
        ck, cv, clf = fox_cache
        past = ck.shape[1]
        lf = jnp.pad(clf.astype(F32), ((0, 0), (0, 0), (COL_FF, GATE_COLS - COL_FF - H_F)))
        lf = jnp.concatenate([lf, gates3], axis=1)
        pad_t = (-lf.shape[1]) % 256
        c = _cumsum_tokens(jnp.pad(lf, ((0, 0), (0, pad_t), (0, 0))))
        ct = _ct_rows(c)
        keys_minor = lambda cache: jnp.transpose(cache.astype(F32), (0, 2, 3, 1))
        yf = _fox_sample(fq.reshape(b, t, W_F), keys_minor(ck), keys_minor(cv),
                         fk.reshape(b, t, W_F), fv.reshape(b, t, W_F),
                         c[:, past:past + t, :], ct[..., :past], ct[..., past:past + t], after[0])
        hm = merged(yf.reshape(n, W_F), 0, after[1])
        peer, wgt = _peer_second_pass(_peer_first_pass(hm, w), w, neutral)
        tail = (peer, wgt, wgt, peer)
        y = output(hm, peer, 0)

    state = (fk_state, fv_state,
             gates3[None, :, :, COL_FF:COL_FF + H_F],
             c_new[None], n_new[None, :, :, 0, :], m_new[None, :, :, 0, 0])
    return y.reshape(b, t, d), state, tail


def kernel(x_prompt, x_sample, p_prompt, p_sample, cache_fox_k, cache_fox_v, cache_fox_logf, state_mlstm_C, state_mlstm_n, state_mlstm_m, norm_mix, w_in, mlstm_b_i, mlstm_b_f, fox_b_f, w_br_m, w_br_f, w_out, norm_ffn, peer_w_q, peer_keys1, peer_keys2, peer_u, peer_v, norm_ple, w_ple_gate, w_ple_proj, norm_final):
    assert w_in.shape[0] == 1, "single-layer trunk"
    w = _prep_weights(norm_mix[0], w_in[0], mlstm_b_i[0], mlstm_b_f[0], fox_b_f[0], w_br_m[0], w_br_f[0],
                      w_out[0], norm_ffn[0], peer_w_q[0], peer_keys1[0], peer_keys2[0], peer_u[0], peer_v[0],
                      norm_ple[0], w_ple_gate[0], w_ple_proj[0], norm_final)
    bp = x_prompt.shape[0]
    zeros = (jnp.zeros((bp, H_M, DH_M, DH_M), F32), jnp.zeros((bp, H_M, DH_M), F32), jnp.zeros((bp, H_M), F32))
    y_p, sp, tail = _layer(x_prompt, p_prompt[0], w, zeros, None, None)
    bs, ts, d = x_sample.shape
    x_s = _ordered(x_sample.reshape(bs * ts, d), tail[0], tail[1]).reshape(bs, ts, d)
    y_s, ss, _ = _layer(x_s, p_sample[0], w,
                        (state_mlstm_C[0], state_mlstm_n[0], state_mlstm_m[0]),
                        (cache_fox_k[0], cache_fox_v[0], cache_fox_logf[0]), tail[2:])
    return (y_p, y_s) + sp + ss
```
